```python
import math
import jax, jax.numpy as jnp
from jax import lax
import numpy as np

D_MODEL = 1024
BATCH = 4
SEQ = 4096
DEPTH = 1
DEC_BATCH = 32
DEC_SEQ = 4
PAST_LEN = 16384
PAGE_SIZE = 128

N_META = 16
DIFF_HEADS = 4
DIFF_DK = 64
DIFF_DV = 128
DIFF_SCALE = DIFF_DK ** -0.5
GLA_HEADS = 4
GLA_DK = 64
GLA_DV = 128
GLA_GATE_RANK = 16
GLA_GATE_NORM = 16.0
GLA_CHUNK = 128
Q_BLOCK = 128
NUM_BUCKETS = 32
MAX_DISTANCE = 128
RMS_EPS = 1e-6
DIFF_WIDTH = DIFF_HEADS * DIFF_DV
GLA_WIDTH = GLA_HEADS * GLA_DV
D_MIX = DIFF_WIDTH + GLA_WIDTH
SPLITS = (DIFF_HEADS * 2 * DIFF_DK, DIFF_HEADS * 2 * DIFF_DK, DIFF_WIDTH, DIFF_WIDTH,
          GLA_HEADS * GLA_DK, GLA_HEADS * GLA_DK, GLA_WIDTH, GLA_WIDTH, GLA_GATE_RANK)
D_IN = sum(SPLITS)

kernel_name = "hymba_diffattn_gla_step"


def rms_norm(x, w):
    xf = x.astype(jnp.float32)
    y = xf * lax.rsqrt(jnp.mean(xf * xf, axis=-1, keepdims=True) + RMS_EPS)
    return y * w.astype(jnp.float32)


def rel_bucket(n):
    n = jnp.maximum(n, 0)
    max_exact = NUM_BUCKETS // 2
    nf = jnp.maximum(n, 1).astype(jnp.float32)
    large = max_exact + (jnp.log(nf / max_exact) / math.log(MAX_DISTANCE / max_exact)
                         * (NUM_BUCKETS - max_exact)).astype(jnp.int32)
    large = jnp.minimum(large, NUM_BUCKETS - 1)
    return jnp.where(n < max_exact, n, large)


def rel_bias_for(rel_bias, n):
    return jnp.transpose(rel_bias[rel_bucket(n)].astype(jnp.float32), (2, 0, 1))


def mixer_inputs(h, norm_w, w_in, q_norm_w, k_norm_w, gla_wa2, gla_ba):
    B, T, _ = h.shape
    hn = rms_norm(h, norm_w).astype(h.dtype)
    z = hn @ w_in
    idx = np.cumsum(SPLITS)[:-1].tolist()
    dq, dk, dv, dg, gq, gk, gv, gg, ga = jnp.split(z, idx, axis=-1)
    dq = rms_norm(dq.reshape(B, T, DIFF_HEADS, 2, DIFF_DK), q_norm_w)
    dk = rms_norm(dk.reshape(B, T, DIFF_HEADS, 2, DIFF_DK), k_norm_w).astype(h.dtype)
    dv = dv.reshape(B, T, DIFF_HEADS, DIFF_DV)
    gq = gq.reshape(B, T, GLA_HEADS, GLA_DK).astype(jnp.float32) * (GLA_DK ** -0.5)
    gk = gk.reshape(B, T, GLA_HEADS, GLA_DK).astype(jnp.float32)
    gv = gv.reshape(B, T, GLA_HEADS, GLA_DV).astype(jnp.float32)
    lg = jax.nn.log_sigmoid((ga @ gla_wa2 + gla_ba).astype(jnp.float32)) / GLA_GATE_NORM
    lg = lg.reshape(B, T, GLA_HEADS, GLA_DK)
    return dq, dk, dv, dg, gq, gk, gv, gg, lg


def diff_lambda(lam_q, lam_k, lam_init):
    lq = lam_q.astype(jnp.float32)
    lk = lam_k.astype(jnp.float32)
    return jnp.exp(jnp.sum(lq[0] * lk[0])) - jnp.exp(jnp.sum(lq[1] * lk[1])) + lam_init


def diff_attn_prompt(q, k, v, lam, rel_bias):
    B, L = q.shape[:2]
    nb = -(-L // Q_BLOCK)
    q_pad = jnp.pad(q, ((0, 0), (0, nb * Q_BLOCK - L), (0, 0), (0, 0), (0, 0)))
    kf = k.astype(jnp.float32)
    vf = v.astype(jnp.float32)
    kpos = jnp.arange(L)

    def block(j):
        q0 = j * Q_BLOCK
        qb = lax.dynamic_slice_in_dim(q_pad, q0, Q_BLOCK, axis=1)
        n = (q0 + jnp.arange(Q_BLOCK))[:, None] - kpos[None, :]
        s = jnp.einsum('bqhmd,bkhmd->bhmqk', qb, kf) * DIFF_SCALE + rel_bias_for(rel_bias, n)[None, :, None]
        s = jnp.where(n >= 0, s, -jnp.inf)
        a = jax.nn.softmax(s, axis=-1)
        p = a[:, :, 0] - lam * a[:, :, 1]
        return jnp.einsum('bhqk,bkhe->bqhe', p, vf)

    o = lax.map(block, jnp.arange(nb))
    o = jnp.transpose(o, (1, 0, 2, 3, 4)).reshape(B, nb * Q_BLOCK, DIFF_HEADS, DIFF_DV)
    return o[:, :L]


def diff_attn_sample(q, k, v, cache_k, cache_v, layer, page_table, lam, rel_bias):
    B, T = q.shape[:2]
    past = page_table.shape[1] * cache_k.shape[2]
    kp = cache_k[layer, page_table].reshape(B, past, DIFF_HEADS, 2, DIFF_DK).astype(jnp.float32)
    vp = cache_v[layer, page_table].reshape(B, past, DIFF_HEADS, DIFF_DV).astype(jnp.float32)
    qpos = past + jnp.arange(T)
    n_past = qpos[:, None] - jnp.arange(past)[None, :]
    n_new = qpos[:, None] - qpos[None, :]
    s_past = jnp.einsum('bqhmd,bkhmd->bhmqk', q, kp) * DIFF_SCALE + rel_bias_for(rel_bias, n_past)[None, :, None]
    s_new = jnp.einsum('bqhmd,bkhmd->bhmqk', q, k.astype(jnp.float32)) * DIFF_SCALE + rel_bias_for(rel_bias, n_new)[None, :, None]
    s_new = jnp.where(n_new >= 0, s_new, -jnp.inf)
    a = jax.nn.softmax(jnp.concatenate([s_past, s_new], axis=-1), axis=-1)
    p = a[:, :, 0] - lam * a[:, :, 1]
    return (jnp.einsum('bhqk,bkhe->bqhe', p[..., :past], vp)
            + jnp.einsum('bhqk,bkhe->bqhe', p[..., past:], v.astype(jnp.float32)))


def gla_chunk(S, q, k, v, lg):
    C = q.shape[1]
    b = jnp.cumsum(lg, axis=1)
    o_inter = jnp.einsum('bthd,bhde->bthe', q * jnp.exp(b), S)
    diff = b[:, :, None] - b[:, None, :]
    causal = (jnp.arange(C)[:, None] >= jnp.arange(C)[None, :])[None, :, :, None, None]
    decay = jnp.exp(jnp.where(causal, diff, -jnp.inf))
    A = jnp.einsum('bthd,btshd,bshd->bths', q, decay, k)
    o = o_inter + jnp.einsum('bths,bshe->bthe', A, v)
    b_last = b[:, -1]
    S_new = jnp.exp(b_last)[..., None] * S + jnp.einsum('bshd,bshe->bhde', k * jnp.exp(b_last[:, None] - b), v)
    return S_new, o


def gla_prompt(q, k, v, lg):
    B, L = q.shape[:2]
    S0 = jnp.zeros((B, GLA_HEADS, GLA_DK, GLA_DV), jnp.float32)
    S, o_meta = gla_chunk(S0, q[:, :N_META], k[:, :N_META], v[:, :N_META], lg[:, :N_META])
    nc = (L - N_META) // GLA_CHUNK

    def blocks(t):
        t = t[:, N_META:]
        return jnp.moveaxis(t.reshape((B, nc, GLA_CHUNK) + t.shape[2:]), 1, 0)

    def step(S, xs):
        return gla_chunk(S, *xs)

    S, o_rest = lax.scan(step, S, (blocks(q), blocks(k), blocks(v), blocks(lg)))
    o_rest = jnp.moveaxis(o_rest, 0, 1).reshape(B, nc * GLA_CHUNK, GLA_HEADS, GLA_DV)
    return jnp.concatenate([o_meta, o_rest], axis=1), S


def merge_out(h, o_d, dg, o_g, gg, diff_norm_w, gla_norm_w, lam_init, w_out):
    B, T = o_d.shape[:2]
    od = (rms_norm(o_d, diff_norm_w) * (1.0 - lam_init)).reshape(B, T, DIFF_WIDTH) * jax.nn.silu(dg.astype(jnp.float32))
    og = rms_norm(o_g, gla_norm_w).reshape(B, T, GLA_WIDTH) * jax.nn.silu(gg.astype(jnp.float32))
    mix = jnp.concatenate([od, og], axis=-1).astype(h.dtype)
    return h + mix @ w_out


def setup_inputs(seed: int = 0) -> dict:
    key = jax.random.key(seed)
    ks = jax.random.split(key, 20)
    n_pages = PAST_LEN // PAGE_SIZE
    n_used = DEC_BATCH * n_pages
    n_pool = n_used + max(1, n_used // 4)
    f32 = jnp.float32
    x_prompt = jax.random.normal(ks[0], (BATCH, SEQ, D_MODEL), f32)
    x_sample = jax.random.normal(ks[1], (DEC_BATCH, DEC_SEQ, D_MODEL), f32)
    cache_k = jax.random.normal(ks[2], (DEPTH, n_pool, PAGE_SIZE, DIFF_HEADS, 2 * DIFF_DK), f32)
    cache_v = jax.random.normal(ks[3], (DEPTH, n_pool, PAGE_SIZE, DIFF_HEADS, DIFF_DV), f32)
    state_gla = 0.5 * jax.random.normal(ks[4], (DEPTH, DEC_BATCH, GLA_HEADS, GLA_DK, GLA_DV), f32)
    page_table = jax.random.permutation(ks[5], n_pool)[:n_used].reshape(DEC_BATCH, n_pages).astype(jnp.int32)
    meta_tokens = jax.random.normal(ks[6], (N_META, D_MODEL), f32)
    rel_bias = 0.1 * jax.random.normal(ks[7], (NUM_BUCKETS, DIFF_HEADS), f32)
    norm_w = 1.0 + 0.02 * jax.random.normal(ks[8], (DEPTH, D_MODEL), f32)
    w_in = jax.random.normal(ks[9], (DEPTH, D_MODEL, D_IN), f32) * D_MODEL ** -0.5
    q_norm_w = 1.0 + 0.02 * jax.random.normal(ks[10], (DEPTH, 2, DIFF_DK), f32)
    k_norm_w = 1.0 + 0.02 * jax.random.normal(ks[11], (DEPTH, 2, DIFF_DK), f32)
    lam_q = 0.1 * jax.random.normal(ks[12], (DEPTH, 2, DIFF_DK), f32)
    lam_k = 0.1 * jax.random.normal(ks[13], (DEPTH, 2, DIFF_DK), f32)
    diff_norm_w = 1.0 + 0.02 * jax.random.normal(ks[14], (DEPTH, DIFF_DV), f32)
    gla_wa2 = jax.random.normal(ks[15], (DEPTH, GLA_GATE_RANK, GLA_HEADS * GLA_DK), f32) * GLA_GATE_RANK ** -0.5
    gla_ba = 0.1 * jax.random.normal(ks[16], (DEPTH, GLA_HEADS * GLA_DK), f32)
    gla_norm_w = 1.0 + 0.02 * jax.random.normal(ks[17], (DEPTH, GLA_DV), f32)
    w_out = jax.random.normal(ks[18], (DEPTH, D_MIX, D_MODEL), f32) * D_MIX ** -0.5
    return {"x_prompt": x_prompt, "x_sample": x_sample, "cache_k": cache_k, "cache_v": cache_v,
            "state_gla": state_gla, "page_table": page_table, "meta_tokens": meta_tokens,
            "rel_bias": rel_bias, "norm_w": norm_w, "w_in": w_in, "q_norm_w": q_norm_w,
            "k_norm_w": k_norm_w, "lam_q": lam_q, "lam_k": lam_k, "diff_norm_w": diff_norm_w,
            "gla_wa2": gla_wa2, "gla_ba": gla_ba, "gla_norm_w": gla_norm_w, "w_out": w_out}


def reference(x_prompt, x_sample, cache_k, cache_v, state_gla, page_table, meta_tokens, rel_bias,
              norm_w, w_in, q_norm_w, k_norm_w, lam_q, lam_k, diff_norm_w, gla_wa2, gla_ba,
              gla_norm_w, w_out):
    B = x_prompt.shape[0]
    L = x_prompt.shape[1] + N_META
    DB, T = x_sample.shape[:2]
    meta = jnp.broadcast_to(meta_tokens[None].astype(x_prompt.dtype), (B, N_META, D_MODEL))
    h_p = jnp.concatenate([meta, x_prompt], axis=1)
    h_s = x_sample
    kp_rows, vp_rows, sp_list, ks_rows, vs_rows, ss_list = [], [], [], [], [], []
    for l in range(DEPTH):
        lam_init = 0.8 - 0.6 * math.exp(-0.3 * l)
        lam = diff_lambda(lam_q[l], lam_k[l], lam_init)
        dq, dk, dv, dg, gq, gk, gv, gg, lg = mixer_inputs(h_p, norm_w[l], w_in[l], q_norm_w[l], k_norm_w[l], gla_wa2[l], gla_ba[l])
        o_d = diff_attn_prompt(dq, dk, dv, lam, rel_bias)
        o_g, s_fin = gla_prompt(gq, gk, gv, lg)
        h_p = merge_out(h_p, o_d, dg, o_g, gg, diff_norm_w[l], gla_norm_w[l], lam_init, w_out[l])
        kp_rows.append(dk.reshape(B, L, DIFF_HEADS, 2 * DIFF_DK))
        vp_rows.append(dv)
        sp_list.append(s_fin.astype(x_prompt.dtype))
        dq, dk, dv, dg, gq, gk, gv, gg, lg = mixer_inputs(h_s, norm_w[l], w_in[l], q_norm_w[l], k_norm_w[l], gla_wa2[l], gla_ba[l])
        o_d = diff_attn_sample(dq, dk, dv, cache_k, cache_v, l, page_table, lam, rel_bias)
        s_new, o_g = gla_chunk(state_gla[l].astype(jnp.float32), gq, gk, gv, lg)
        h_s = merge_out(h_s, o_d, dg, o_g, gg, diff_norm_w[l], gla_norm_w[l], lam_init, w_out[l])
        ks_rows.append(dk.reshape(DB, T, DIFF_HEADS, 2 * DIFF_DK).astype(cache_k.dtype))
        vs_rows.append(dv.astype(cache_v.dtype))
        ss_list.append(s_new.astype(state_gla.dtype))
    y_prompt = h_p[:, N_META:]
    y_sample = h_s
    k_prompt = jnp.stack(kp_rows)
    v_prompt = jnp.stack(vp_rows)
    s_prompt = jnp.stack(sp_list)
    k_sample = jnp.stack(ks_rows)
    v_sample = jnp.stack(vs_rows)
    s_sample = jnp.stack(ss_list)
    return (y_prompt, y_sample, k_prompt, v_prompt, s_prompt, k_sample, v_sample, s_sample)
```

```python
import functools
import math

import numpy as np
import jax
import jax.numpy as jnp
from jax import lax
from jax.experimental import pallas as pl
from jax.experimental.pallas import tpu as pltpu

D_MODEL = 1024
N_META = 16
HEADS = 4
DK = 64
DV = 128
DIFF_SCALE = DK ** -0.5
GLA_GATE_RANK = 16
GLA_GATE_NORM = 16.0
NUM_BUCKETS = 32
MAX_DISTANCE = 128
RMS_EPS = 1e-6
LAM_INIT = 0.8 - 0.6 * math.exp(-0.3 * 0)
QKW = HEADS * 2 * DK
VW = HEADS * DV
GW = HEADS * DK
Z_MAIN = 3 * QKW + VW + 2 * GW + 2 * VW
LANES = 128
CHUNK = 128
NEG = -1e30
VMEM_LIMIT = 56 * 1024 * 1024

F32 = jnp.float32
BF16 = jnp.bfloat16


def _dot(a, b):
    return jnp.dot(a, b, preferred_element_type=F32)


def _dot_nt(a, b):
    return lax.dot_general(a, b, (((1,), (1,)), ((), ())), preferred_element_type=F32)


def _lam_value(lamq_ref, lamk_ref):
    e = jnp.exp(jnp.sum(lamq_ref[...] * lamk_ref[...], axis=-1, keepdims=True))
    return e[0:1, :] - e[1:2, :] + LAM_INIT


def _inproj_kernel(x_ref, nw_ref, wm_ref, wa1_ref, wa2_ref, ba_ref, qw_ref, kw_ref, g_ref,
                   qn_ref, kn_ref, knb_ref, v_ref, vb_ref, gate_ref, gq_ref, gk_ref, gv_ref, lg_ref):
    x = x_ref[...]
    ms = jnp.mean(x * x, axis=-1, keepdims=True)
    hn = (x * lax.rsqrt(ms + RMS_EPS) * nw_ref[...]).astype(BF16)

    def proj(lo, hi):
        return _dot(hn, wm_ref[:, lo:hi])

    def group_norm(z, w):
        sq = (z * z).astype(BF16)
        ss = jnp.concatenate([_dot(sq[:, c:c + 256], g_ref[...]) for c in range(0, QKW, 256)], axis=-1)
        return z * lax.rsqrt(ss * (1.0 / DK) + RMS_EPS) * w

    qn_ref[...] = group_norm(proj(0, QKW), qw_ref[...]).astype(BF16)
    kn = group_norm(proj(QKW, 2 * QKW), kw_ref[...])
    kn_ref[...] = kn
    knb_ref[...] = kn.astype(BF16)
    v = proj(2 * QKW, 2 * QKW + VW)
    v_ref[...] = v
    vb_ref[...] = v.astype(BF16)
    o = 2 * QKW + VW
    dg = proj(o, o + VW)
    gate_ref[:, 0:VW] = (dg * jax.nn.sigmoid(dg)).astype(BF16)
    o += VW
    gq_ref[...] = proj(o, o + GW) * (DK ** -0.5)
    gk_ref[...] = proj(o + GW, o + 2 * GW)
    o += 2 * GW
    gv_ref[...] = proj(o, o + VW).astype(BF16)
    o += VW
    gg = proj(o, o + VW)
    gate_ref[:, VW:2 * VW] = (gg * jax.nn.sigmoid(gg)).astype(BF16)
    ga = _dot(hn, wa1_ref[...])
    xg = _dot(ga.astype(BF16), wa2_ref[...]) + ba_ref[...]
    lg_ref[...] = (jnp.minimum(xg, 0.0) - jnp.log(1.0 + jnp.exp(-jnp.abs(xg)))) * (1.0 / GLA_GATE_NORM)


def _inproj(x, tr, consts):
    rows = x.shape[0]
    nw, wm, wa1, wa2, ba, qw, kw, gmat = consts
    full = lambda a: pl.BlockSpec(a.shape, lambda i: (0,) * a.ndim)
    row = lambda w: pl.BlockSpec((tr, w), lambda i: (i, 0))
    outs = [(QKW, BF16), (QKW, F32), (QKW, BF16), (VW, F32), (VW, BF16), (2 * VW, BF16),
            (GW, F32), (GW, F32), (VW, BF16), (GW, F32)]
    return pl.pallas_call(
        _inproj_kernel,
        grid=(rows // tr,),
        in_specs=[row(D_MODEL)] + [full(a) for a in consts],
        out_specs=[row(w) for w, _ in outs],
        out_shape=[jax.ShapeDtypeStruct((rows, w), dt) for w, dt in outs],
        compiler_params=pltpu.CompilerParams(dimension_semantics=("arbitrary",), vmem_limit_bytes=VMEM_LIMIT),
        name="inproj",
    )(x, *consts)


def _attn_prompt_kernel(lamq_ref, lamk_ref, q_ref, k_ref, v_ref, km_ref, vm_ref, dtab_ref, utab_ref, mtab_ref,
                        o_ref, m_sc, l_sc, acc_sc, *, tile):
    i = pl.program_id(2)
    q = q_ref[...]
    lane = lax.broadcasted_iota(jnp.int32, (1, LANES), 1)
    zero = jnp.zeros_like(q)
    qs = jnp.concatenate([jnp.where(lane < DK, q, zero), jnp.where(lane >= DK, q, zero)], axis=0)

    m_sc[...] = jnp.full(m_sc.shape, NEG, F32)
    l_sc[...] = jnp.zeros(l_sc.shape, F32)
    acc_sc[...] = jnp.zeros(acc_sc.shape, F32)

    def step(kt, vt, bias):
        s = _dot_nt(qs, kt)
        if bias is not None:
            s = s + jnp.concatenate([bias, bias], axis=0)
        m_prev = m_sc[...]
        m_new = jnp.maximum(m_prev, jnp.max(s, axis=-1, keepdims=True))
        alpha = jnp.exp(m_prev - m_new)
        p = jnp.exp(s - m_new)
        l_sc[...] = alpha * l_sc[...] + jnp.sum(p, axis=-1, keepdims=True)
        acc_sc[...] = alpha * acc_sc[...] + _dot(p.astype(BF16), vt)
        m_sc[...] = m_new

    step(km_ref[...], vm_ref[...], mtab_ref[...])

    def far(j, carry):
        o = pl.multiple_of(j * tile, tile)
        step(k_ref[pl.ds(o, tile), :], v_ref[pl.ds(o, tile), :], None)
        return carry

    lax.fori_loop(0, jnp.maximum(i - 1, 0), far, 0)

    @pl.when(i >= 1)
    def _():
        o = pl.multiple_of((i - 1) * tile, tile)
        step(k_ref[pl.ds(o, tile), :], v_ref[pl.ds(o, tile), :], utab_ref[...])

    o = pl.multiple_of(i * tile, tile)
    step(k_ref[pl.ds(o, tile), :], v_ref[pl.ds(o, tile), :], dtab_ref[...])

    lam = _lam_value(lamq_ref, lamk_ref)
    acc = acc_sc[...]
    inv = 1.0 / l_sc[...]
    o_ref[...] = acc[0:tile] * inv[0:tile] - (lam * inv[tile:]) * acc[tile:]


def _attn_prompt(lam_q, lam_k, qn, knb, vb, km, vm, dtab, utab, mtab, tile):
    batch, seq, _ = qn.shape
    nq = seq // tile
    small = lambda a: pl.BlockSpec(a.shape, lambda b, h, i: (0,) * a.ndim)
    kv = pl.BlockSpec((None, seq, LANES), lambda b, h, i: (b, 0, h))
    tab = pl.BlockSpec((None, tile, tile), lambda b, h, i: (h, 0, 0))
    return pl.pallas_call(
        functools.partial(_attn_prompt_kernel, tile=tile),
        grid=(batch, HEADS, nq),
        in_specs=[small(lam_q), small(lam_k),
                  pl.BlockSpec((None, tile, LANES), lambda b, h, i: (b, i, h)),
                  kv, kv,
                  pl.BlockSpec((LANES, LANES), lambda b, h, i: (0, h)),
                  pl.BlockSpec((LANES, LANES), lambda b, h, i: (0, h)),
                  tab, tab,
                  pl.BlockSpec((None, None, tile, LANES), lambda b, h, i: (h, jnp.minimum(i, 1), 0, 0))],
        out_specs=pl.BlockSpec((None, tile, LANES), lambda b, h, i: (b, i, h)),
        out_shape=jax.ShapeDtypeStruct((batch, seq, VW), F32),
        scratch_shapes=[pltpu.VMEM((2 * tile, 1), F32), pltpu.VMEM((2 * tile, 1), F32),
                        pltpu.VMEM((2 * tile, LANES), F32)],
        compiler_params=pltpu.CompilerParams(dimension_semantics=("arbitrary",) * 3, vmem_limit_bytes=VMEM_LIMIT),
        name="attn_prompt",
    )(lam_q, lam_k, qn, knb, vb, km, vm, dtab, utab, mtab)


def _attn_sample_kernel(pt_ref, lamq_ref, lamk_ref, q_ref, kn_ref, vn_ref, ntab_ref, ptab_ref, *rest, pages, tq):
    k_refs = rest[0:pages]
    v_refs = rest[pages:2 * pages]
    o_ref, w_sc, m_sc, l_sc, acc_sc = rest[2 * pages:]
    g = pl.program_id(1)
    ng = pl.num_programs(1)
    nrow = 2 * HEADS * tq

    def update(s, pv):
        m_prev = m_sc[...]
        m_new = jnp.maximum(m_prev, jnp.max(s, axis=-1, keepdims=True))
        alpha = jnp.exp(m_prev - m_new)
        p = jnp.exp(s - m_new)
        l_sc[...] = alpha * l_sc[...] + jnp.sum(p, axis=-1, keepdims=True)
        acc_sc[...] = alpha * acc_sc[...] + pv(p.astype(BF16))
        m_sc[...] = m_new

    @pl.when(g == 0)
    def _():
        q = q_ref[...].astype(F32)
        hm = lax.broadcasted_iota(jnp.int32, (2 * HEADS, QKW), 0)
        grp = lax.broadcasted_iota(jnp.int32, (2 * HEADS, QKW), 1) // DK
        sel = hm == grp
        w = jnp.concatenate([jnp.where(sel, jnp.broadcast_to(q[t:t + 1, :], (2 * HEADS, QKW)), 0.0)
                             for t in range(tq)], axis=0)
        w_sc[...] = w.astype(BF16)
        m_sc[...] = jnp.full(m_sc.shape, NEG, F32)
        l_sc[...] = jnp.zeros(l_sc.shape, F32)
        acc_sc[...] = jnp.zeros(acc_sc.shape, F32)
        pad = jnp.zeros((LANES - kn_ref.shape[0], QKW), F32)
        kn = jnp.concatenate([kn_ref[...], pad], axis=0).astype(BF16)
        vn = jnp.concatenate([vn_ref[...], pad], axis=0).astype(BF16)
        s = _dot_nt(w_sc[...], kn) + ntab_ref[...]
        update(s, lambda p: _dot(p, vn))

    w = w_sc[...]
    last = g == ng - 1
    parts = []
    for u in range(pages):
        s = _dot_nt(w, k_refs[u][...].astype(BF16))
        if u == pages - 1:
            s = s + jnp.where(last, ptab_ref[...], 0.0)
        parts.append(s)
    s = jnp.concatenate(parts, axis=-1)

    def pv(p):
        acc = _dot(p[:, 0:LANES], v_refs[0][...].astype(BF16))
        for u in range(1, pages):
            acc += _dot(p[:, u * LANES:(u + 1) * LANES], v_refs[u][...].astype(BF16))
        return acc

    update(s, pv)

    @pl.when(last)
    def _():
        lam = _lam_value(lamq_ref, lamk_ref)
        accn = acc_sc[...] * (1.0 / l_sc[...])
        r = lax.broadcasted_iota(jnp.int32, (2 * HEADS, VW), 0)
        head = lax.broadcasted_iota(jnp.int32, (2 * HEADS, VW), 1) // DV
        coef = jnp.where(r == 2 * head, 1.0, 0.0) - jnp.where(r == 2 * head + 1, 1.0, 0.0) * lam
        rows = [jnp.sum(accn[t * 2 * HEADS:(t + 1) * 2 * HEADS, :] * coef, axis=0, keepdims=True) for t in range(tq)]
        rows.append(jnp.zeros((o_ref.shape[0] - tq, VW), F32))
        o_ref[...] = jnp.concatenate(rows, axis=0)


def _attn_sample(page_table, lam_q, lam_k, qn, kn, v, ntab, ptab, cache_k, cache_v, tq, pages):
    nb, n_pages = page_table.shape
    rpb = qn.shape[0] // nb
    page_size = cache_k.shape[1]
    ng = n_pages // pages
    nrow = 2 * HEADS * tq
    small = lambda a: pl.BlockSpec(a.shape, lambda b, g, pt: (0,) * a.ndim)
    rowblk = lambda w: pl.BlockSpec((rpb, w), lambda b, g, pt: (b, 0))

    def page_spec(u):
        return pl.BlockSpec((None, page_size, QKW), lambda b, g, pt: (pt[b, g * pages + u], 0, 0))

    grid_spec = pltpu.PrefetchScalarGridSpec(
        num_scalar_prefetch=1,
        grid=(nb, ng),
        in_specs=[small(lam_q), small(lam_k), rowblk(QKW), rowblk(QKW), rowblk(VW), small(ntab), small(ptab)]
                 + [page_spec(u) for u in range(pages)] + [page_spec(u) for u in range(pages)],
        out_specs=pl.BlockSpec((rpb, VW), lambda b, g, pt: (b, 0)),
        scratch_shapes=[pltpu.VMEM((nrow, QKW), BF16), pltpu.VMEM((nrow, 1), F32), pltpu.VMEM((nrow, 1), F32),
                        pltpu.VMEM((nrow, VW), F32)],
    )
    return pl.pallas_call(
        functools.partial(_attn_sample_kernel, pages=pages, tq=tq),
        grid_spec=grid_spec,
        out_shape=jax.ShapeDtypeStruct((nb * rpb, VW), F32),
        compiler_params=pltpu.CompilerParams(dimension_semantics=("arbitrary", "arbitrary"),
                                             vmem_limit_bytes=VMEM_LIMIT),
        name="attn_sample",
    )(page_table, lam_q, lam_k, qn, kn, v, ntab, ptab, *([cache_k] * pages), *([cache_v] * pages))


def _gla_kernel(s0_ref, q_ref, k_ref, v_ref, lg_ref, tri_ref, o_ref, sout_ref, s_sc, *, valid):
    c = pl.program_id(1)
    rows_in = q_ref.shape[0]

    @pl.when(c == 0)
    def _():
        s_sc[...] = s0_ref[...]

    def padded(a):
        if rows_in == CHUNK:
            return a
        return jnp.concatenate([a, jnp.zeros((CHUNK - rows_in, a.shape[1]), a.dtype)], axis=0)

    q = padded(q_ref[...])
    k = padded(k_ref[...])
    lg = padded(lg_ref[...])
    vb = padded(v_ref[...])
    if valid < CHUNK:
        live = lax.broadcasted_iota(jnp.int32, (CHUNK, 1), 0) < valid
        k = jnp.where(live, k, 0.0)
        lg = jnp.where(live, lg, 0.0)
        vb = jnp.where(live, vb, jnp.zeros_like(vb))

    lg_hi = lg.astype(BF16)
    lg_lo = (lg - lg_hi.astype(F32)).astype(BF16)
    tri = tri_ref[...]
    b = _dot(tri, lg_hi) + _dot(tri, lg_lo)
    b_mid = b[CHUNK // 2 - 1:CHUNK // 2, :]
    qs = q * jnp.exp(b)
    qt = q * jnp.exp(b - b_mid)
    kt = (k * jnp.exp(b_mid - b)).astype(BF16)
    b_t = b.T
    b_last = b_t[:, CHUNK - 1:CHUNK]
    kl_t = (k.T * jnp.exp(b_last - b_t)).astype(BF16)
    decay = jnp.exp(b_last)

    s_old = s_sc[...]
    s_bf = s_old.astype(BF16)
    row = lax.broadcasted_iota(jnp.int32, (CHUNK, CHUNK), 0)
    col = lax.broadcasted_iota(jnp.int32, (CHUNK, CHUNK), 1)
    causal = row >= col
    lane = lax.broadcasted_iota(jnp.int32, (1, LANES), 1)
    for h in range(HEADS):
        pair = (h // 2) * LANES
        mine = (lane // DK) == (h % 2)
        qs_h = jnp.where(mine, qs[:, pair:pair + LANES], 0.0).astype(BF16)
        qt_h = jnp.where(mine, qt[:, pair:pair + LANES], 0.0).astype(BF16)
        a = jnp.where(causal, _dot_nt(qt_h, kt[:, pair:pair + LANES]), 0.0)
        v_h = vb[:, h * DV:(h + 1) * DV]
        o_ref[:, h * DV:(h + 1) * DV] = (_dot(qs_h, s_bf[pair:pair + LANES, :]) + _dot(a.astype(BF16), v_h))[0:rows_in]
        hs = slice(h * DK, (h + 1) * DK)
        s_sc[hs, :] = decay[hs, :] * s_old[hs, :] + _dot(kl_t[hs, :], v_h)

    @pl.when(c == pl.num_programs(1) - 1)
    def _():
        sout_ref[...] = s_sc[...]


def _gla(s0, gq, gk, gv, lg, tri, nb, rows_in, valid):
    nc = gq.shape[0] // (nb * rows_in)
    s0_map = (lambda b, c: (b, 0, 0)) if s0.shape[0] == nb else (lambda b, c: (0, 0, 0))
    rowblk = lambda w: pl.BlockSpec((rows_in, w), lambda b, c: (b * nc + c, 0))
    return pl.pallas_call(
        functools.partial(_gla_kernel, valid=valid),
        grid=(nb, nc),
        in_specs=[pl.BlockSpec((None, GW, DV), s0_map), rowblk(GW), rowblk(GW), rowblk(VW), rowblk(GW),
                  pl.BlockSpec(tri.shape, lambda b, c: (0, 0))],
        out_specs=[rowblk(VW), pl.BlockSpec((None, GW, DV), lambda b, c: (b, 0, 0))],
        out_shape=[jax.ShapeDtypeStruct((gq.shape[0], VW), F32), jax.ShapeDtypeStruct((nb, GW, DV), F32)],
        scratch_shapes=[pltpu.VMEM((GW, DV), F32)],
        compiler_params=pltpu.CompilerParams(dimension_semantics=("arbitrary", "arbitrary"),
                                             vmem_limit_bytes=VMEM_LIMIT),
        name="gla",
    )(s0, gq, gk, gv, lg, tri)


def _merge_kernel(x_ref, od_ref, og_ref, gate_ref, dnw_ref, gnw_ref, wtop_ref, wbot_ref, y_ref):
    def head_norm(o, w):
        parts = []
        for h in range(HEADS):
            sl = o[:, h * DV:(h + 1) * DV]
            parts.append(sl * lax.rsqrt(jnp.mean(sl * sl, axis=-1, keepdims=True) + RMS_EPS))
        return jnp.concatenate(parts, axis=-1) * w

    gate = gate_ref[...].astype(F32)
    mix_d = (head_norm(od_ref[...], dnw_ref[...]) * gate[:, 0:VW]).astype(BF16)
    mix_g = (head_norm(og_ref[...], gnw_ref[...]) * gate[:, VW:]).astype(BF16)
    y_ref[...] = x_ref[...] + _dot(mix_d, wtop_ref[...]) + _dot(mix_g, wbot_ref[...])


def _merge(x, od, og, gates, dnw, gnw, wtop, wbot, tr):
    rows = x.shape[0]
    full = lambda a: pl.BlockSpec(a.shape, lambda i: (0,) * a.ndim)
    row = lambda w: pl.BlockSpec((tr, w), lambda i: (i, 0))
    return pl.pallas_call(
        _merge_kernel,
        grid=(rows // tr,),
        in_specs=[row(D_MODEL), row(VW), row(VW), row(2 * VW), full(dnw), full(gnw), full(wtop), full(wbot)],
        out_specs=row(D_MODEL),
        out_shape=jax.ShapeDtypeStruct((rows, D_MODEL), F32),
        compiler_params=pltpu.CompilerParams(dimension_semantics=("arbitrary",), vmem_limit_bytes=VMEM_LIMIT),
        name="merge",
    )(x, od, og, gates, dnw, gnw, wtop, wbot)


def _rel_bucket(n):
    n = jnp.maximum(n, 0)
    max_exact = NUM_BUCKETS // 2
    nf = jnp.maximum(n, 1).astype(F32)
    large = max_exact + (jnp.log(nf / max_exact) / math.log(MAX_DISTANCE / max_exact)
                         * (NUM_BUCKETS - max_exact)).astype(jnp.int32)
    return jnp.where(n < max_exact, n, jnp.minimum(large, NUM_BUCKETS - 1))


def _bias_of(bvec, n):
    val = bvec[:, jnp.clip(n, 0, 2 * MAX_DISTANCE - 1)]
    return jnp.where(n[None] >= 0, val, NEG)


def kernel(x_prompt, x_sample, cache_k, cache_v, state_gla, page_table, meta_tokens, rel_bias, norm_w, w_in,
           q_norm_w, k_norm_w, lam_q, lam_k, diff_norm_w, gla_wa2, gla_ba, gla_norm_w, w_out):
    batch, seq, _ = x_prompt.shape
    nb, tq, _ = x_sample.shape
    past = page_table.shape[1] * cache_k.shape[2]
    tile = 512
    rpb = 16

    w = w_in[0]
    wm = w[:, :Z_MAIN].astype(BF16)
    wa1 = jnp.pad(w[:, Z_MAIN:], ((0, 0), (0, LANES - GLA_GATE_RANK))).astype(BF16)
    wa2 = jnp.pad(gla_wa2[0], ((0, LANES - GLA_GATE_RANK), (0, 0))).astype(BF16)
    ba = gla_ba[0][None]
    nw = norm_w[0][None]
    qw = jnp.tile(q_norm_w[0].reshape(-1), HEADS)[None] * DIFF_SCALE
    kw = jnp.tile(k_norm_w[0].reshape(-1), HEADS)[None]
    grp = np.arange(256) // DK
    gmat = jnp.asarray(grp[:, None] == grp[None, :], BF16)
    consts = (nw, wm, wa1, wa2, ba, qw, kw, gmat)
    dnw = jnp.tile(diff_norm_w[0], HEADS)[None] * (1.0 - LAM_INIT)
    gnw = jnp.tile(gla_norm_w[0], HEADS)[None]
    wtop = w_out[0][:VW].astype(BF16)
    wbot = w_out[0][VW:].astype(BF16)
    t_idx = np.arange(CHUNK)
    tri = jnp.asarray(t_idx[:, None] >= t_idx[None, :], BF16)
    lq, lk = lam_q[0], lam_k[0]

    dist = jnp.arange(2 * MAX_DISTANCE)
    by_dist = rel_bias[_rel_bucket(dist)].astype(F32)
    bvec = (by_dist - by_dist[-1][None, :]).T

    xp = x_prompt.reshape(batch * seq, D_MODEL)
    p_qn, p_kn, p_knb, p_v, p_vb, p_gate, p_gq, p_gk, p_gv, p_lg = _inproj(xp, 512, consts)

    xs = jnp.pad(x_sample, ((0, 0), (0, rpb - tq), (0, 0))).reshape(nb * rpb, D_MODEL)
    xm = jnp.pad(meta_tokens, ((0, CHUNK - N_META), (0, 0)))
    x_small = jnp.concatenate([xs, xm], axis=0)
    ns = nb * rpb
    small = _inproj(x_small, x_small.shape[0], consts)
    s_qn, s_kn, s_knb, s_v, s_vb, s_gate, s_gq, s_gk, s_gv, s_lg = [a[:ns] for a in small]
    m_qn, m_kn, m_knb, m_v, m_vb, m_gate, m_gq, m_gk, m_gv, m_lg = [a[ns:] for a in small]

    r = jnp.arange(tile)
    n_diag = r[:, None] - r[None, :]
    dtab = _bias_of(bvec, n_diag)
    utab = _bias_of(bvec, n_diag + tile)
    mcol = jnp.arange(LANES)
    n_meta = N_META + r[:, None] - mcol[None, :]
    mvalid = (mcol < N_META)[None, None, :]
    mtab = jnp.stack([jnp.where(mvalid, _bias_of(bvec, n_meta), NEG),
                      jnp.where(mvalid, _bias_of(bvec, n_meta + tile), NEG)], axis=1)
    o_d = _attn_prompt(lq, lk, p_qn.reshape(batch, seq, QKW), p_knb.reshape(batch, seq, QKW),
                       p_vb.reshape(batch, seq, VW), m_knb, m_vb, dtab, utab, mtab, tile)

    zero_state = jnp.zeros((1, GW, DV), F32)
    _, s_meta = _gla(zero_state, m_gq, m_gk, m_gv, m_lg, tri, 1, CHUNK, N_META)
    o_g, s_fin = _gla(s_meta, p_gq, p_gk, p_gv, p_lg, tri, batch, CHUNK, CHUNK)
    os_g, s_new = _gla(state_gla[0].reshape(nb, GW, DV), s_gq, s_gk, s_gv, s_lg, tri, nb, rpb, tq)

    hm = jnp.arange(2 * HEADS * tq) % (2 * HEADS)
    tok = jnp.arange(2 * HEADS * tq) // (2 * HEADS)
    ccol = jnp.arange(LANES)
    bv_rows = bvec[hm // 2]
    n_new = tok[:, None] - ccol[None, :]
    ntab = jnp.where((n_new >= 0) & (ccol[None, :] < tq),
                     jnp.take_along_axis(bv_rows, jnp.clip(n_new, 0, 2 * MAX_DISTANCE - 1), axis=1), NEG)
    n_last = cache_k.shape[2] + tok[:, None] - ccol[None, :]
    ptab = jnp.take_along_axis(bv_rows, jnp.clip(n_last, 0, 2 * MAX_DISTANCE - 1), axis=1)
    ck = cache_k[0].reshape(cache_k.shape[1], cache_k.shape[2], QKW)
    cv = cache_v[0].reshape(cache_v.shape[1], cache_v.shape[2], VW)
    os_d = _attn_sample(page_table, lq, lk, s_qn, s_kn, s_v, ntab, ptab, ck, cv, tq, 8)

    y_prompt = _merge(xp, o_d.reshape(batch * seq, VW), o_g, p_gate, dnw, gnw, wtop, wbot, 512)
    y_small = _merge(xs, os_d, os_g, s_gate, dnw, gnw, wtop, wbot, ns)

    y_prompt = y_prompt.reshape(batch, seq, D_MODEL)
    y_sample = y_small.reshape(nb, rpb, D_MODEL)[:, :tq]
    k_meta = jnp.broadcast_to(m_kn[:N_META][None], (batch, N_META, QKW))
    v_meta = jnp.broadcast_to(m_v[:N_META][None], (batch, N_META, VW))
    k_prompt = jnp.concatenate([k_meta, p_kn.reshape(batch, seq, QKW)], axis=1)
    v_prompt = jnp.concatenate([v_meta, p_v.reshape(batch, seq, VW)], axis=1)
    k_prompt = k_prompt.reshape(1, batch, seq + N_META, HEADS, 2 * DK)
    v_prompt = v_prompt.reshape(1, batch, seq + N_META, HEADS, DV)
    s_prompt = s_fin.reshape(1, batch, HEADS, DK, DV)
    k_sample = s_kn.reshape(nb, rpb, HEADS, 2 * DK)[None, :, :tq]
    v_sample = s_v.reshape(nb, rpb, HEADS, DV)[None, :, :tq]
    s_sample = s_new.reshape(1, nb, HEADS, DK, DV)
    return (y_prompt, y_sample, k_prompt, v_prompt, s_prompt, k_sample, v_sample, s_sample)
```

```python
import functools
import math

import numpy as np
import jax
import jax.numpy as jnp
from jax import lax
from jax.experimental import pallas as pl
from jax.experimental.pallas import tpu as pltpu

D_MODEL = 1024
N_META = 16
HEADS = 4
DK = 64
DV = 128
DIFF_SCALE = DK ** -0.5
GLA_GATE_RANK = 16
GLA_GATE_NORM = 16.0
NUM_BUCKETS = 32
MAX_DISTANCE = 128
RMS_EPS = 1e-6
LAM_INIT = 0.8 - 0.6 * math.exp(-0.3 * 0)
QKW = HEADS * 2 * DK
VW = HEADS * DV
GW = HEADS * DK
Z_MAIN = 3 * QKW + VW + 2 * GW + 2 * VW
LANES = 128
CHUNK = 128
NEG = -1e30
LOG2E = math.log2(math.e)
VMEM_LIMIT = 56 * 1024 * 1024

F32 = jnp.float32
BF16 = jnp.bfloat16


def _dot(a, b):
    return jnp.dot(a, b, preferred_element_type=F32)


def _dot_nt(a, b):
    return lax.dot_general(a, b, (((1,), (1,)), ((), ())), preferred_element_type=F32)


def _dot_tn(a, b):
    return lax.dot_general(a, b, (((0,), (0,)), ((), ())), preferred_element_type=F32)


def _lam_value(lamq_ref, lamk_ref):
    e = jnp.exp(jnp.sum(lamq_ref[...] * lamk_ref[...], axis=-1, keepdims=True))
    return e[0:1, :] - e[1:2, :] + LAM_INIT


def _inproj_kernel(x_ref, nw_ref, wm_ref, wa1_ref, wa2_ref, ba_ref, qw_ref, kw_ref, g_ref,
                   qn_ref, kn_ref, knb_ref, v_ref, vb_ref, gate_ref, gq_ref, gk_ref, gv_ref, lg_ref):
    x = x_ref[...]
    ms = jnp.mean(x * x, axis=-1, keepdims=True)
    hn = (x * lax.rsqrt(ms + RMS_EPS) * nw_ref[...]).astype(BF16)

    def proj(lo, hi):
        return _dot(hn, wm_ref[:, lo:hi])

    def group_norm(z, w):
        sq = (z * z).astype(BF16)
        ss = jnp.concatenate([_dot(sq[:, c:c + 256], g_ref[...]) for c in range(0, QKW, 256)], axis=-1)
        return z * lax.rsqrt(ss * (1.0 / DK) + RMS_EPS) * w

    qn_ref[...] = group_norm(proj(0, QKW), qw_ref[...]).astype(BF16)
    kn = group_norm(proj(QKW, 2 * QKW), kw_ref[...])
    kn_ref[...] = kn
    knb_ref[...] = kn.astype(BF16)
    v = proj(2 * QKW, 2 * QKW + VW)
    v_ref[...] = v
    vb_ref[...] = v.astype(BF16)
    o = 2 * QKW + VW
    dg = proj(o, o + VW)
    gate_ref[:, 0:VW] = (dg * jax.nn.sigmoid(dg)).astype(BF16)
    o += VW
    gq_ref[...] = proj(o, o + GW) * (DK ** -0.5)
    gk_ref[...] = proj(o + GW, o + 2 * GW)
    o += 2 * GW
    gv_ref[...] = proj(o, o + VW).astype(BF16)
    o += VW
    gg = proj(o, o + VW)
    gate_ref[:, VW:2 * VW] = (gg * jax.nn.sigmoid(gg)).astype(BF16)
    ga = _dot(hn, wa1_ref[...])
    xg = _dot(ga.astype(BF16), wa2_ref[...]) + ba_ref[...]
    lg_ref[...] = (jnp.minimum(xg, 0.0) - jnp.log(1.0 + jnp.exp(-jnp.abs(xg)))) * (1.0 / GLA_GATE_NORM)


def _inproj(x, tr, consts):
    rows = x.shape[0]
    full = lambda a: pl.BlockSpec(a.shape, lambda i: (0,) * a.ndim)
    row = lambda w: pl.BlockSpec((tr, w), lambda i: (i, 0))
    outs = [(QKW, BF16), (QKW, F32), (QKW, BF16), (VW, F32), (VW, BF16), (2 * VW, BF16),
            (GW, F32), (GW, F32), (VW, BF16), (GW, F32)]
    return pl.pallas_call(
        _inproj_kernel,
        grid=(rows // tr,),
        in_specs=[row(D_MODEL)] + [full(a) for a in consts],
        out_specs=[row(w) for w, _ in outs],
        out_shape=[jax.ShapeDtypeStruct((rows, w), dt) for w, dt in outs],
        compiler_params=pltpu.CompilerParams(dimension_semantics=("arbitrary",), vmem_limit_bytes=VMEM_LIMIT),
        name="inproj",
    )(x, *consts)


def _bucket_ranges():
    n = np.arange(MAX_DISTANCE)
    max_exact = NUM_BUCKETS // 2
    nf = np.maximum(n, 1).astype(np.float32)
    large = max_exact + (np.log(nf / np.float32(max_exact)) / np.float32(math.log(MAX_DISTANCE / max_exact))
                         * np.float32(NUM_BUCKETS - max_exact)).astype(np.int32)
    bucket = np.where(n < max_exact, n, np.minimum(large, NUM_BUCKETS - 1))
    return [(int(n[bucket == b].min()), int(n[bucket == b].max())) for b in range(NUM_BUCKETS)]


_BUCKET_RANGES = _bucket_ranges()


def _bias_table(n, rb_ref, h):
    far = rb_ref[NUM_BUCKETS - 1, h]
    t = jnp.zeros(n.shape, F32)
    for b, (lo, hi) in enumerate(_BUCKET_RANGES[:-1]):
        val = (rb_ref[b, h] - far) * LOG2E
        cond = (n == lo) if lo == hi else ((n >= lo) & (n <= hi))
        t = jnp.where(cond, val, t)
    return jnp.where(n < 0, NEG, t)


def _attn_prompt_kernel(rb_ref, lamq_ref, lamk_ref, q_ref, k_ref, v_ref, km_ref, vm_ref, o_ref,
                        dt_sc, ut_sc, mt_sc, m_sc, l_sc, acc_sc, *, tile):
    h = pl.program_id(0)
    b = pl.program_id(1)
    i = pl.program_id(2)

    @pl.when((b == 0) & (i == 0))
    def _():
        kk = lax.broadcasted_iota(jnp.int32, (tile, tile), 0)
        qq = lax.broadcasted_iota(jnp.int32, (tile, tile), 1)
        dt_sc[...] = _bias_table(qq - kk, rb_ref, h)
        ut_sc[...] = _bias_table(qq - kk + tile, rb_ref, h)
        km = lax.broadcasted_iota(jnp.int32, (N_META, tile), 0)
        qm = lax.broadcasted_iota(jnp.int32, (N_META, tile), 1)
        mt_sc[...] = _bias_table(N_META + qm - km, rb_ref, h)

    q = q_ref[...]
    lane = lax.broadcasted_iota(jnp.int32, (1, LANES), 1)
    zero = jnp.zeros_like(q)
    qs = jnp.concatenate([jnp.where(lane < DK, q, zero), jnp.where(lane >= DK, q, zero)], axis=0)

    m_sc[...] = jnp.full(m_sc.shape, NEG, F32)
    l_sc[...] = jnp.zeros(l_sc.shape, F32)
    acc_sc[...] = jnp.zeros(acc_sc.shape, F32)

    def step(kt, vt, bias):
        s = _dot_nt(kt, qs)
        if bias is not None:
            s = s + jnp.concatenate([bias, bias], axis=1)
        m_prev = m_sc[...]
        m_new = jnp.maximum(m_prev, jnp.max(s, axis=0, keepdims=True))
        alpha = jnp.exp2(m_prev - m_new)
        p = jnp.exp2(s - m_new)
        l_sc[...] = alpha * l_sc[...] + jnp.sum(p, axis=0, keepdims=True)
        acc_sc[...] = alpha * acc_sc[...] + _dot_tn(vt, p.astype(BF16))
        m_sc[...] = m_new

    step(km_ref[...], vm_ref[...], jnp.where(i == 0, mt_sc[...], 0.0))

    def far(j, carry):
        o = pl.multiple_of(j * tile, tile)
        step(k_ref[pl.ds(o, tile), :], v_ref[pl.ds(o, tile), :], None)
        return carry

    lax.fori_loop(0, jnp.maximum(i - 1, 0), far, 0)

    @pl.when(i >= 1)
    def _():
        o = pl.multiple_of((i - 1) * tile, tile)
        step(k_ref[pl.ds(o, tile), :], v_ref[pl.ds(o, tile), :], ut_sc[...])

    o = pl.multiple_of(i * tile, tile)
    step(k_ref[pl.ds(o, tile), :], v_ref[pl.ds(o, tile), :], dt_sc[...])

    lam = _lam_value(lamq_ref, lamk_ref)
    acc = acc_sc[...]
    inv = 1.0 / l_sc[...]
    o_t = acc[:, 0:tile] * inv[:, 0:tile] - (lam * inv[:, tile:]) * acc[:, tile:]
    o_ref[...] = o_t.T


def _attn_prompt(rel_bias, lam_q, lam_k, qn, knb, vb, km, vm, tile):
    batch, seq, _ = qn.shape
    nq = seq // tile
    small = lambda a: pl.BlockSpec(a.shape, lambda h, b, i: (0,) * a.ndim)
    kv = pl.BlockSpec((None, seq, LANES), lambda h, b, i: (b, 0, h))
    meta = pl.BlockSpec((N_META, LANES), lambda h, b, i: (0, h))
    return pl.pallas_call(
        functools.partial(_attn_prompt_kernel, tile=tile),
        grid=(HEADS, batch, nq),
        in_specs=[pl.BlockSpec(memory_space=pltpu.SMEM), small(lam_q), small(lam_k),
                  pl.BlockSpec((None, tile, LANES), lambda h, b, i: (b, i, h)),
                  kv, kv, meta, meta],
        out_specs=pl.BlockSpec((None, tile, LANES), lambda h, b, i: (b, i, h)),
        out_shape=jax.ShapeDtypeStruct((batch, seq, VW), F32),
        scratch_shapes=[pltpu.VMEM((tile, tile), F32), pltpu.VMEM((tile, tile), F32), pltpu.VMEM((N_META, tile), F32),
                        pltpu.VMEM((1, 2 * tile), F32), pltpu.VMEM((1, 2 * tile), F32),
                        pltpu.VMEM((LANES, 2 * tile), F32)],
        compiler_params=pltpu.CompilerParams(dimension_semantics=("arbitrary",) * 3, vmem_limit_bytes=VMEM_LIMIT),
        name="attn_prompt",
    )(rel_bias, lam_q, lam_k, qn, knb, vb, km, vm)


def _attn_sample_kernel(pt_ref, rb_ref, lamq_ref, lamk_ref, q_ref, kn_ref, vn_ref, *rest, pages, tq):
    k_refs = rest[0:pages]
    v_refs = rest[pages:2 * pages]
    o_ref, w_sc, mask_sc, ptab_sc, ntab_sc, m_sc, l_sc, acc_sc = rest[2 * pages:]
    b = pl.program_id(0)
    g = pl.program_id(1)
    ng = pl.num_programs(1)
    nrow = 2 * HEADS * tq
    pcols = k_refs[0].shape[0]
    page_size = pcols // HEADS

    @pl.when((b == 0) & (g == 0))
    def _():
        def tables(cols, offset):
            r = lax.broadcasted_iota(jnp.int32, (nrow, cols), 0)
            c = lax.broadcasted_iota(jnp.int32, (nrow, cols), 1)
            tok = r // (2 * HEADS)
            head = (r % (2 * HEADS)) // 2
            n = offset + tok - c // HEADS
            t = jnp.zeros((nrow, cols), F32)
            for h in range(HEADS):
                t = jnp.where(head == h, _bias_table(n, rb_ref, h), t)
            return jnp.where(head == c % HEADS, t, NEG), c // HEADS

        mask_sc[...] = jnp.where(tables(pcols, MAX_DISTANCE)[0] > 0.5 * NEG, 0.0, NEG)
        ptab_sc[...] = tables(pcols, page_size)[0]
        nt, slot = tables(LANES, 0)
        ntab_sc[...] = jnp.where(slot < tq, nt, NEG)

    def update(s, pv):
        m_prev = m_sc[...]
        m_new = jnp.maximum(m_prev, jnp.max(s, axis=-1, keepdims=True))
        alpha = jnp.exp2(m_prev - m_new)
        p = jnp.exp2(s - m_new)
        l_sc[...] = alpha * l_sc[...] + jnp.sum(p, axis=-1, keepdims=True)
        acc_sc[...] = alpha * acc_sc[...] + pv(p.astype(BF16))
        m_sc[...] = m_new

    @pl.when(g == 0)
    def _():
        q = q_ref[...].astype(F32)
        r8 = lax.broadcasted_iota(jnp.int32, (2 * HEADS, LANES), 0)
        lane = lax.broadcasted_iota(jnp.int32, (2 * HEADS, LANES), 1)
        blocks = []
        for t in range(tq):
            blk = jnp.zeros((2 * HEADS, LANES), F32)
            for h in range(HEADS):
                blk = jnp.where(r8 // 2 == h, jnp.broadcast_to(q[t:t + 1, h * LANES:(h + 1) * LANES], blk.shape), blk)
            blocks.append(jnp.where(lane // DK == r8 % 2, blk, 0.0))
        w_sc[...] = jnp.concatenate(blocks, axis=0).astype(BF16)
        m_sc[...] = jnp.full(m_sc.shape, NEG, F32)
        l_sc[...] = jnp.zeros(l_sc.shape, F32)
        acc_sc[...] = jnp.zeros(acc_sc.shape, F32)
        pad = jnp.zeros((LANES - kn_ref.shape[0], LANES), F32)
        kn = jnp.concatenate([kn_ref[...], pad], axis=0).astype(BF16)
        vn = jnp.concatenate([vn_ref[...], pad], axis=0).astype(BF16)
        s = _dot_nt(w_sc[...], kn) + ntab_sc[...]
        update(s, lambda p: _dot(p, vn))

    w = w_sc[...]
    last = g == ng - 1
    parts = []
    for u in range(pages):
        s = _dot_nt(w, k_refs[u][...].astype(BF16))
        if u == pages - 1:
            s = s + jnp.where(last, ptab_sc[...], mask_sc[...])
        else:
            s = s + mask_sc[...]
        parts.append(s)
    s = jnp.concatenate(parts, axis=-1)

    def pv(p):
        acc = _dot(p[:, 0:pcols], v_refs[0][...].astype(BF16))
        for u in range(1, pages):
            acc += _dot(p[:, u * pcols:(u + 1) * pcols], v_refs[u][...].astype(BF16))
        return acc

    update(s, pv)

    @pl.when(last)
    def _():
        lam = _lam_value(lamq_ref, lamk_ref)
        accn = acc_sc[...] * (1.0 / l_sc[...])
        o_ref[...] = jnp.zeros(o_ref.shape, F32)
        for t in range(tq):
            for h in range(HEADS):
                r = t * 2 * HEADS + 2 * h
                o_ref[t:t + 1, h * DV:(h + 1) * DV] = accn[r:r + 1, :] - lam * accn[r + 1:r + 2, :]


def _attn_sample(page_table, rel_bias, lam_q, lam_k, qn, kn, v, cache_k, cache_v, tq, pages):
    nb, n_pages = page_table.shape
    rpb = qn.shape[0] // nb
    pcols = cache_k.shape[1]
    ng = n_pages // pages
    nrow = 2 * HEADS * tq
    small = lambda a: pl.BlockSpec(a.shape, lambda b, g, pt: (0,) * a.ndim)

    def page_spec(u):
        return pl.BlockSpec((None, pcols, LANES), lambda b, g, pt: (pt[b, g * pages + u], 0, 0))

    grid_spec = pltpu.PrefetchScalarGridSpec(
        num_scalar_prefetch=1,
        grid=(nb, ng),
        in_specs=[pl.BlockSpec(memory_space=pltpu.SMEM), small(lam_q), small(lam_k),
                  pl.BlockSpec((rpb, QKW), lambda b, g, pt: (b, 0)),
                  pl.BlockSpec((rpb * HEADS, LANES), lambda b, g, pt: (b, 0)),
                  pl.BlockSpec((rpb * HEADS, LANES), lambda b, g, pt: (b, 0))]
                 + [page_spec(u) for u in range(pages)] + [page_spec(u) for u in range(pages)],
        out_specs=pl.BlockSpec((rpb, VW), lambda b, g, pt: (b, 0)),
        scratch_shapes=[pltpu.VMEM((nrow, LANES), BF16), pltpu.VMEM((nrow, pcols), F32), pltpu.VMEM((nrow, pcols), F32),
                        pltpu.VMEM((nrow, LANES), F32), pltpu.VMEM((nrow, 1), F32), pltpu.VMEM((nrow, 1), F32),
                        pltpu.VMEM((nrow, LANES), F32)],
    )
    return pl.pallas_call(
        functools.partial(_attn_sample_kernel, pages=pages, tq=tq),
        grid_spec=grid_spec,
        out_shape=jax.ShapeDtypeStruct((nb * rpb, VW), F32),
        compiler_params=pltpu.CompilerParams(dimension_semantics=("arbitrary", "arbitrary"),
                                             vmem_limit_bytes=VMEM_LIMIT),
        name="attn_sample",
    )(page_table, rel_bias, lam_q, lam_k, qn, kn, v, *([cache_k] * pages), *([cache_v] * pages))


def _gla_kernel(s0_ref, q_ref, k_ref, v_ref, lg_ref, tri_ref, o_ref, sout_ref, s_sc, *, valid):
    c = pl.program_id(1)
    rows_in = q_ref.shape[0]

    @pl.when(c == 0)
    def _():
        s_sc[...] = s0_ref[...]

    def padded(a):
        if rows_in == CHUNK:
            return a
        return jnp.concatenate([a, jnp.zeros((CHUNK - rows_in, a.shape[1]), a.dtype)], axis=0)

    q = padded(q_ref[...])
    k = padded(k_ref[...])
    lg = padded(lg_ref[...])
    vb = padded(v_ref[...])
    if valid < CHUNK:
        live = lax.broadcasted_iota(jnp.int32, (CHUNK, 1), 0) < valid
        k = jnp.where(live, k, 0.0)
        lg = jnp.where(live, lg, 0.0)
        vb = jnp.where(live, vb, jnp.zeros_like(vb))

    lg_hi = lg.astype(BF16)
    lg_lo = (lg - lg_hi.astype(F32)).astype(BF16)
    tri = tri_ref[...]
    b = _dot(tri, lg_hi) + _dot(tri, lg_lo)
    b_mid = b[CHUNK // 2 - 1:CHUNK // 2, :]
    qs = q * jnp.exp(b)
    qt = q * jnp.exp(b - b_mid)
    kt = (k * jnp.exp(b_mid - b)).astype(BF16)
    b_t = b.T
    b_last = b_t[:, CHUNK - 1:CHUNK]
    kl_t = (k.T * jnp.exp(b_last - b_t)).astype(BF16)
    decay = jnp.exp(b_last)

    s_old = s_sc[...]
    s_bf = s_old.astype(BF16)
    row = lax.broadcasted_iota(jnp.int32, (CHUNK, CHUNK), 0)
    col = lax.broadcasted_iota(jnp.int32, (CHUNK, CHUNK), 1)
    causal = row >= col
    lane = lax.broadcasted_iota(jnp.int32, (1, LANES), 1)
    for h in range(HEADS):
        pair = (h // 2) * LANES
        mine = (lane // DK) == (h % 2)
        qs_h = jnp.where(mine, qs[:, pair:pair + LANES], 0.0).astype(BF16)
        qt_h = jnp.where(mine, qt[:, pair:pair + LANES], 0.0).astype(BF16)
        a = jnp.where(causal, _dot_nt(qt_h, kt[:, pair:pair + LANES]), 0.0)
        v_h = vb[:, h * DV:(h + 1) * DV]
        o_ref[:, h * DV:(h + 1) * DV] = (_dot(qs_h, s_bf[pair:pair + LANES, :]) + _dot(a.astype(BF16), v_h))[0:rows_in]
        hs = slice(h * DK, (h + 1) * DK)
        s_sc[hs, :] = decay[hs, :] * s_old[hs, :] + _dot(kl_t[hs, :], v_h)

    @pl.when(c == pl.num_programs(1) - 1)
    def _():
        sout_ref[...] = s_sc[...]


def _gla(s0, gq, gk, gv, lg, tri, nb, rows_in, valid):
    nc = gq.shape[0] // (nb * rows_in)
    s0_map = (lambda b, c: (b, 0, 0)) if s0.shape[0] == nb else (lambda b, c: (0, 0, 0))
    rowblk = lambda w: pl.BlockSpec((rows_in, w), lambda b, c: (b * nc + c, 0))
    return pl.pallas_call(
        functools.partial(_gla_kernel, valid=valid),
        grid=(nb, nc),
        in_specs=[pl.BlockSpec((None, GW, DV), s0_map), rowblk(GW), rowblk(GW), rowblk(VW), rowblk(GW),
                  pl.BlockSpec(tri.shape, lambda b, c: (0, 0))],
        out_specs=[rowblk(VW), pl.BlockSpec((None, GW, DV), lambda b, c: (b, 0, 0))],
        out_shape=[jax.ShapeDtypeStruct((gq.shape[0], VW), F32), jax.ShapeDtypeStruct((nb, GW, DV), F32)],
        scratch_shapes=[pltpu.VMEM((GW, DV), F32)],
        compiler_params=pltpu.CompilerParams(dimension_semantics=("arbitrary", "arbitrary"),
                                             vmem_limit_bytes=VMEM_LIMIT),
        name="gla",
    )(s0, gq, gk, gv, lg, tri)


def _merge_kernel(x_ref, od_ref, og_ref, gate_ref, dnw_ref, gnw_ref, wtop_ref, wbot_ref, y_ref):
    def head_norm(o, w):
        parts = []
        for h in range(HEADS):
            sl = o[:, h * DV:(h + 1) * DV]
            parts.append(sl * lax.rsqrt(jnp.mean(sl * sl, axis=-1, keepdims=True) + RMS_EPS))
        return jnp.concatenate(parts, axis=-1) * w

    gate = gate_ref[...].astype(F32)
    mix_d = (head_norm(od_ref[...], dnw_ref[...]) * gate[:, 0:VW]).astype(BF16)
    mix_g = (head_norm(og_ref[...], gnw_ref[...]) * gate[:, VW:]).astype(BF16)
    y_ref[...] = x_ref[...] + _dot(mix_d, wtop_ref[...]) + _dot(mix_g, wbot_ref[...])


def _merge(x, od, og, gates, dnw, gnw, wtop, wbot, tr):
    rows = x.shape[0]
    full = lambda a: pl.BlockSpec(a.shape, lambda i: (0,) * a.ndim)
    row = lambda w: pl.BlockSpec((tr, w), lambda i: (i, 0))
    return pl.pallas_call(
        _merge_kernel,
        grid=(rows // tr,),
        in_specs=[row(D_MODEL), row(VW), row(VW), row(2 * VW), full(dnw), full(gnw), full(wtop), full(wbot)],
        out_specs=row(D_MODEL),
        out_shape=jax.ShapeDtypeStruct((rows, D_MODEL), F32),
        compiler_params=pltpu.CompilerParams(dimension_semantics=("arbitrary",), vmem_limit_bytes=VMEM_LIMIT),
        name="merge",
    )(x, od, og, gates, dnw, gnw, wtop, wbot)


def kernel(x_prompt, x_sample, cache_k, cache_v, state_gla, page_table, meta_tokens, rel_bias, norm_w, w_in,
           q_norm_w, k_norm_w, lam_q, lam_k, diff_norm_w, gla_wa2, gla_ba, gla_norm_w, w_out):
    batch, seq, _ = x_prompt.shape
    nb, tq, _ = x_sample.shape
    tile = 512
    rpb = 16

    w = w_in[0]
    wm = w[:, :Z_MAIN].astype(BF16)
    wa1 = jnp.pad(w[:, Z_MAIN:], ((0, 0), (0, LANES - GLA_GATE_RANK))).astype(BF16)
    wa2 = jnp.pad(gla_wa2[0], ((0, LANES - GLA_GATE_RANK), (0, 0))).astype(BF16)
    ba = gla_ba[0][None]
    nw = norm_w[0][None]
    qw = jnp.tile(q_norm_w[0].reshape(-1), HEADS)[None] * (DIFF_SCALE * LOG2E)
    kw = jnp.tile(k_norm_w[0].reshape(-1), HEADS)[None]
    grp = np.arange(256) // DK
    gmat = jnp.asarray(grp[:, None] == grp[None, :], BF16)
    consts = (nw, wm, wa1, wa2, ba, qw, kw, gmat)
    dnw = jnp.tile(diff_norm_w[0], HEADS)[None] * (1.0 - LAM_INIT)
    gnw = jnp.tile(gla_norm_w[0], HEADS)[None]
    wtop = w_out[0][:VW].astype(BF16)
    wbot = w_out[0][VW:].astype(BF16)
    t_idx = np.arange(CHUNK)
    tri = jnp.asarray(t_idx[:, None] >= t_idx[None, :], BF16)
    lq, lk = lam_q[0], lam_k[0]

    xp = x_prompt.reshape(batch * seq, D_MODEL)
    p_qn, p_kn, p_knb, p_v, p_vb, p_gate, p_gq, p_gk, p_gv, p_lg = _inproj(xp, 512, consts)

    xs = jnp.pad(x_sample, ((0, 0), (0, rpb - tq), (0, 0))).reshape(nb * rpb, D_MODEL)
    xm = jnp.pad(meta_tokens, ((0, CHUNK - N_META), (0, 0)))
    x_small = jnp.concatenate([xs, xm], axis=0)
    ns = nb * rpb
    small = _inproj(x_small, x_small.shape[0], consts)
    s_qn, s_kn, s_knb, s_v, s_vb, s_gate, s_gq, s_gk, s_gv, s_lg = [a[:ns] for a in small]
    m_qn, m_kn, m_knb, m_v, m_vb, m_gate, m_gq, m_gk, m_gv, m_lg = [a[ns:] for a in small]

    o_d = _attn_prompt(rel_bias, lq, lk, p_qn.reshape(batch, seq, QKW), p_knb.reshape(batch, seq, QKW),
                       p_vb.reshape(batch, seq, VW), m_knb, m_vb, tile)

    zero_state = jnp.zeros((1, GW, DV), F32)
    _, s_meta = _gla(zero_state, m_gq, m_gk, m_gv, m_lg, tri, 1, CHUNK, N_META)
    o_g, s_fin = _gla(s_meta, p_gq, p_gk, p_gv, p_lg, tri, batch, CHUNK, CHUNK)
    os_g, s_new = _gla(state_gla[0].reshape(nb, GW, DV), s_gq, s_gk, s_gv, s_lg, tri, nb, rpb, tq)

    n_pool, page_size = cache_k.shape[1], cache_k.shape[2]
    ck = cache_k.reshape(n_pool, page_size * HEADS, 2 * DK)
    cv = cache_v.reshape(n_pool, page_size * HEADS, DV)
    os_d = _attn_sample(page_table, rel_bias, lq, lk, s_qn, s_kn.reshape(ns * HEADS, 2 * DK),
                        s_v.reshape(ns * HEADS, DV), ck, cv, tq, 8)

    y_prompt = _merge(xp, o_d.reshape(batch * seq, VW), o_g, p_gate, dnw, gnw, wtop, wbot, 512)
    y_small = _merge(xs, os_d, os_g, s_gate, dnw, gnw, wtop, wbot, ns)

    y_prompt = y_prompt.reshape(batch, seq, D_MODEL)
    y_sample = y_small.reshape(nb, rpb, D_MODEL)[:, :tq]
    k_meta = jnp.broadcast_to(m_kn[:N_META][None], (batch, N_META, QKW))
    v_meta = jnp.broadcast_to(m_v[:N_META][None], (batch, N_META, VW))
    k_prompt = jnp.concatenate([k_meta, p_kn.reshape(batch, seq, QKW)], axis=1)
    v_prompt = jnp.concatenate([v_meta, p_v.reshape(batch, seq, VW)], axis=1)
    k_prompt = k_prompt.reshape(1, batch, seq + N_META, HEADS, 2 * DK)
    v_prompt = v_prompt.reshape(1, batch, seq + N_META, HEADS, DV)
    s_prompt = s_fin.reshape(1, batch, HEADS, DK, DV)
    k_sample = s_kn.reshape(nb, rpb, HEADS, 2 * DK)[None, :, :tq]
    v_sample = s_v.reshape(nb, rpb, HEADS, DV)[None, :, :tq]
    s_sample = s_new.reshape(1, nb, HEADS, DK, DV)
    return (y_prompt, y_sample, k_prompt, v_prompt, s_prompt, k_sample, v_sample, s_sample)
```

```python
import functools
import math

import numpy as np
import jax
import jax.numpy as jnp
from jax import lax
from jax.experimental import pallas as pl
from jax.experimental.pallas import tpu as pltpu

D_MODEL = 1024
N_META = 16
HEADS = 4
DK = 64
DV = 128
DIFF_SCALE = DK ** -0.5
GLA_GATE_RANK = 16
GLA_GATE_NORM = 16.0
NUM_BUCKETS = 32
MAX_DISTANCE = 128
RMS_EPS = 1e-6
LAM_INIT = 0.8 - 0.6 * math.exp(-0.3 * 0)
QKW = HEADS * 2 * DK
VW = HEADS * DV
GW = HEADS * DK
Z_MAIN = 3 * QKW + VW + 2 * GW + 2 * VW
LANES = 128
CHUNK = 128
NEG = -1e30
LOG2E = math.log2(math.e)
RING_SLOTS = 3
GLA_SEQS_PER_STEP = 4
VMEM_LIMIT = 56 * 1024 * 1024

F32 = jnp.float32
BF16 = jnp.bfloat16


def _dot(a, b):
    return jnp.dot(a, b, preferred_element_type=F32)


def _dot_nt(a, b):
    return lax.dot_general(a, b, (((1,), (1,)), ((), ())), preferred_element_type=F32)


def _dot_tn(a, b):
    return lax.dot_general(a, b, (((0,), (0,)), ((), ())), preferred_element_type=F32)


def _lam_value(lamq_ref, lamk_ref):
    e = jnp.exp(jnp.sum(lamq_ref[...] * lamk_ref[...], axis=-1, keepdims=True))
    return e[0:1, :] - e[1:2, :] + LAM_INIT


def _inproj_kernel(x_ref, nw_ref, wm_ref, wa1_ref, wa2_ref, ba_ref, qw_ref, kw_ref, g_ref,
                   qn_ref, kn_ref, knb_ref, v_ref, vb_ref, gate_ref, gq_ref, gk_ref, gv_ref, lg_ref):
    x = x_ref[...]
    ms = jnp.mean(x * x, axis=-1, keepdims=True)
    hn = (x * lax.rsqrt(ms + RMS_EPS) * nw_ref[...]).astype(BF16)

    def proj(lo, hi):
        return _dot(hn, wm_ref[:, lo:hi])

    def group_norm(z, w):
        sq = (z * z).astype(BF16)
        ss = jnp.concatenate([_dot(sq[:, c:c + 256], g_ref[...]) for c in range(0, QKW, 256)], axis=-1)
        return z * lax.rsqrt(ss * (1.0 / DK) + RMS_EPS) * w

    qn_ref[...] = group_norm(proj(0, QKW), qw_ref[...]).astype(BF16)
    kn = group_norm(proj(QKW, 2 * QKW), kw_ref[...])
    kn_ref[...] = kn
    knb_ref[...] = kn.astype(BF16)
    v = proj(2 * QKW, 2 * QKW + VW)
    v_ref[...] = v
    vb_ref[...] = v.astype(BF16)
    o = 2 * QKW + VW
    dg = proj(o, o + VW)
    gate_ref[:, 0:VW] = (dg * jax.nn.sigmoid(dg)).astype(BF16)
    o += VW
    gq_ref[...] = proj(o, o + GW) * (DK ** -0.5)
    gk_ref[...] = proj(o + GW, o + 2 * GW)
    o += 2 * GW
    gv_ref[...] = proj(o, o + VW).astype(BF16)
    o += VW
    gg = proj(o, o + VW)
    gate_ref[:, VW:2 * VW] = (gg * jax.nn.sigmoid(gg)).astype(BF16)
    ga = _dot(hn, wa1_ref[...])
    xg = _dot(ga.astype(BF16), wa2_ref[...]) + ba_ref[...]
    lg_ref[...] = (jnp.minimum(xg, 0.0) - jnp.log(1.0 + jnp.exp(-jnp.abs(xg)))) * (1.0 / GLA_GATE_NORM)


def _inproj(x, tr, consts):
    rows = x.shape[0]
    full = lambda a: pl.BlockSpec(a.shape, lambda i: (0,) * a.ndim)
    row = lambda w: pl.BlockSpec((tr, w), lambda i: (i, 0))
    outs = [(QKW, BF16), (QKW, F32), (QKW, BF16), (VW, F32), (VW, BF16), (2 * VW, BF16),
            (GW, F32), (GW, F32), (VW, BF16), (GW, F32)]
    return pl.pallas_call(
        _inproj_kernel,
        grid=(rows // tr,),
        in_specs=[row(D_MODEL)] + [full(a) for a in consts],
        out_specs=[row(w) for w, _ in outs],
        out_shape=[jax.ShapeDtypeStruct((rows, w), dt) for w, dt in outs],
        compiler_params=pltpu.CompilerParams(dimension_semantics=("arbitrary",), vmem_limit_bytes=VMEM_LIMIT),
        name="inproj",
    )(x, *consts)


def _bucket_ranges():
    n = np.arange(MAX_DISTANCE)
    max_exact = NUM_BUCKETS // 2
    nf = np.maximum(n, 1).astype(np.float32)
    large = max_exact + (np.log(nf / np.float32(max_exact)) / np.float32(math.log(MAX_DISTANCE / max_exact))
                         * np.float32(NUM_BUCKETS - max_exact)).astype(np.int32)
    bucket = np.where(n < max_exact, n, np.minimum(large, NUM_BUCKETS - 1))
    return [(int(n[bucket == b].min()), int(n[bucket == b].max())) for b in range(NUM_BUCKETS)]


_BUCKET_RANGES = _bucket_ranges()


def _bias_table(n, rb_ref, h):
    far = rb_ref[NUM_BUCKETS - 1, h]
    t = jnp.zeros(n.shape, F32)
    for b, (lo, hi) in enumerate(_BUCKET_RANGES[:-1]):
        val = (rb_ref[b, h] - far) * LOG2E
        cond = (n == lo) if lo == hi else ((n >= lo) & (n <= hi))
        t = jnp.where(cond, val, t)
    return jnp.where(n < 0, NEG, t)


def _attn_prompt_kernel(rb_ref, lamq_ref, lamk_ref, q_ref, k_ref, v_ref, km_ref, vm_ref, o_ref,
                        dt_sc, ut_sc, mt_sc, m_sc, l_sc, acc_sc, *, tile):
    h = pl.program_id(0)
    b = pl.program_id(1)
    i = pl.program_id(2)

    @pl.when((b == 0) & (i == 0))
    def _():
        kk = lax.broadcasted_iota(jnp.int32, (tile, tile), 0)
        qq = lax.broadcasted_iota(jnp.int32, (tile, tile), 1)
        dt_sc[...] = _bias_table(qq - kk, rb_ref, h)
        ut_sc[...] = _bias_table(qq - kk + tile, rb_ref, h)
        km = lax.broadcasted_iota(jnp.int32, (N_META, tile), 0)
        qm = lax.broadcasted_iota(jnp.int32, (N_META, tile), 1)
        mt_sc[...] = _bias_table(N_META + qm - km, rb_ref, h)

    q = q_ref[...]
    lane = lax.broadcasted_iota(jnp.int32, (1, LANES), 1)
    zero = jnp.zeros_like(q)
    qs = jnp.concatenate([jnp.where(lane < DK, q, zero), jnp.where(lane >= DK, q, zero)], axis=0)

    m_sc[...] = jnp.full(m_sc.shape, NEG, F32)
    l_sc[...] = jnp.zeros(l_sc.shape, F32)
    acc_sc[...] = jnp.zeros(acc_sc.shape, F32)

    def step(kt, vt, bias):
        s = _dot_nt(kt, qs)
        if bias is not None:
            s = s + jnp.concatenate([bias, bias], axis=1)
        m_prev = m_sc[...]
        m_new = jnp.maximum(m_prev, jnp.max(s, axis=0, keepdims=True))
        alpha = jnp.exp2(m_prev - m_new)
        p = jnp.exp2(s - m_new)
        l_sc[...] = alpha * l_sc[...] + jnp.sum(p, axis=0, keepdims=True)
        acc_sc[...] = alpha * acc_sc[...] + _dot_tn(vt, p.astype(BF16))
        m_sc[...] = m_new

    step(km_ref[...], vm_ref[...], jnp.where(i == 0, mt_sc[...], 0.0))

    def far(j, carry):
        o = pl.multiple_of(j * tile, tile)
        step(k_ref[pl.ds(o, tile), :], v_ref[pl.ds(o, tile), :], None)
        return carry

    lax.fori_loop(0, jnp.maximum(i - 1, 0), far, 0)

    @pl.when(i >= 1)
    def _():
        o = pl.multiple_of((i - 1) * tile, tile)
        step(k_ref[pl.ds(o, tile), :], v_ref[pl.ds(o, tile), :], ut_sc[...])

    o = pl.multiple_of(i * tile, tile)
    step(k_ref[pl.ds(o, tile), :], v_ref[pl.ds(o, tile), :], dt_sc[...])

    lam = _lam_value(lamq_ref, lamk_ref)
    acc = acc_sc[...]
    inv = 1.0 / l_sc[...]
    o_t = acc[:, 0:tile] * inv[:, 0:tile] - (lam * inv[:, tile:]) * acc[:, tile:]
    o_ref[...] = o_t.T


def _attn_prompt(rel_bias, lam_q, lam_k, qn, knb, vb, km, vm, tile):
    batch, seq, _ = qn.shape
    nq = seq // tile
    small = lambda a: pl.BlockSpec(a.shape, lambda h, b, i: (0,) * a.ndim)
    kv = pl.BlockSpec((None, seq, LANES), lambda h, b, i: (b, 0, h))
    meta = pl.BlockSpec((N_META, LANES), lambda h, b, i: (0, h))
    return pl.pallas_call(
        functools.partial(_attn_prompt_kernel, tile=tile),
        grid=(HEADS, batch, nq),
        in_specs=[pl.BlockSpec(memory_space=pltpu.SMEM), small(lam_q), small(lam_k),
                  pl.BlockSpec((None, tile, LANES), lambda h, b, i: (b, i, h)),
                  kv, kv, meta, meta],
        out_specs=pl.BlockSpec((None, tile, LANES), lambda h, b, i: (b, i, h)),
        out_shape=jax.ShapeDtypeStruct((batch, seq, VW), F32),
        scratch_shapes=[pltpu.VMEM((tile, tile), F32), pltpu.VMEM((tile, tile), F32), pltpu.VMEM((N_META, tile), F32),
                        pltpu.VMEM((1, 2 * tile), F32), pltpu.VMEM((1, 2 * tile), F32),
                        pltpu.VMEM((LANES, 2 * tile), F32)],
        compiler_params=pltpu.CompilerParams(dimension_semantics=("arbitrary",) * 3, vmem_limit_bytes=VMEM_LIMIT),
        name="attn_prompt",
    )(rel_bias, lam_q, lam_k, qn, knb, vb, km, vm)


def _attn_sample_kernel(pt_ref, rb_ref, lamq_ref, lamk_ref, q_ref, kn_ref, vn_ref, ck_hbm, cv_hbm, o_ref,
                        kbuf, vbuf, sem, w_sc, mask_sc, ptab_sc, ntab_sc, m_sc, l_sc, acc_sc, *, pages, tq):
    b = pl.program_id(0)
    g = pl.program_id(1)
    ng = pl.num_programs(1)
    step = b * ng + g
    total = pl.num_programs(0) * ng
    nrow = 2 * HEADS * tq
    pcols = kbuf.shape[2]
    page_size = pcols // HEADS

    def page_copies(st):
        slot = st % RING_SLOTS
        out = []
        for u in range(pages):
            page = pt_ref[st // ng, (st % ng) * pages + u]
            out.append(pltpu.make_async_copy(ck_hbm.at[page], kbuf.at[slot, u], sem.at[slot, 0, u]))
            out.append(pltpu.make_async_copy(cv_hbm.at[page], vbuf.at[slot, u], sem.at[slot, 1, u]))
        return out

    @pl.when(step == 0)
    def _():
        for ahead in range(RING_SLOTS - 1):
            for c in page_copies(ahead):
                c.start()

    @pl.when(step + RING_SLOTS - 1 < total)
    def _():
        for c in page_copies(step + RING_SLOTS - 1):
            c.start()

    @pl.when((b == 0) & (g == 0))
    def _():
        def tables(cols, offset):
            r = lax.broadcasted_iota(jnp.int32, (nrow, cols), 0)
            c = lax.broadcasted_iota(jnp.int32, (nrow, cols), 1)
            tok = r // (2 * HEADS)
            head = (r % (2 * HEADS)) // 2
            n = offset + tok - c // HEADS
            t = jnp.zeros((nrow, cols), F32)
            for h in range(HEADS):
                t = jnp.where(head == h, _bias_table(n, rb_ref, h), t)
            return jnp.where(head == c % HEADS, t, NEG), c // HEADS

        mask_sc[...] = jnp.where(tables(pcols, MAX_DISTANCE)[0] > 0.5 * NEG, 0.0, NEG)
        ptab_sc[...] = tables(pcols, page_size)[0]
        nt, slot = tables(LANES, 0)
        ntab_sc[...] = jnp.where(slot < tq, nt, NEG)

    def update(s, pv):
        m_prev = m_sc[...]
        m_new = jnp.maximum(m_prev, jnp.max(s, axis=-1, keepdims=True))
        alpha = jnp.exp2(m_prev - m_new)
        p = jnp.exp2(s - m_new)
        l_sc[...] = alpha * l_sc[...] + jnp.sum(p, axis=-1, keepdims=True)
        acc_sc[...] = alpha * acc_sc[...] + pv(p.astype(BF16))
        m_sc[...] = m_new

    @pl.when(g == 0)
    def _():
        q = q_ref[...].astype(F32)
        r8 = lax.broadcasted_iota(jnp.int32, (2 * HEADS, LANES), 0)
        lane = lax.broadcasted_iota(jnp.int32, (2 * HEADS, LANES), 1)
        blocks = []
        for t in range(tq):
            blk = jnp.zeros((2 * HEADS, LANES), F32)
            for h in range(HEADS):
                blk = jnp.where(r8 // 2 == h, jnp.broadcast_to(q[t:t + 1, h * LANES:(h + 1) * LANES], blk.shape), blk)
            blocks.append(jnp.where(lane // DK == r8 % 2, blk, 0.0))
        w_sc[...] = jnp.concatenate(blocks, axis=0).astype(BF16)
        m_sc[...] = jnp.full(m_sc.shape, NEG, F32)
        l_sc[...] = jnp.zeros(l_sc.shape, F32)
        acc_sc[...] = jnp.zeros(acc_sc.shape, F32)
        pad = jnp.zeros((LANES - kn_ref.shape[0], LANES), F32)
        kn = jnp.concatenate([kn_ref[...], pad], axis=0).astype(BF16)
        vn = jnp.concatenate([vn_ref[...], pad], axis=0).astype(BF16)
        s = _dot_nt(w_sc[...], kn) + ntab_sc[...]
        update(s, lambda p: _dot(p, vn))

    for c in page_copies(step):
        c.wait()
    slot = step % RING_SLOTS
    w = w_sc[...]
    last = g == ng - 1
    parts = []
    for u in range(pages):
        s = _dot_nt(w, kbuf[slot, u].astype(BF16))
        if u == pages - 1:
            s = s + jnp.where(last, ptab_sc[...], mask_sc[...])
        else:
            s = s + mask_sc[...]
        parts.append(s)
    s = jnp.concatenate(parts, axis=-1)

    def pv(p):
        acc = _dot(p[:, 0:pcols], vbuf[slot, 0].astype(BF16))
        for u in range(1, pages):
            acc += _dot(p[:, u * pcols:(u + 1) * pcols], vbuf[slot, u].astype(BF16))
        return acc

    update(s, pv)

    @pl.when(last)
    def _():
        lam = _lam_value(lamq_ref, lamk_ref)
        accn = acc_sc[...] * (1.0 / l_sc[...])
        o_ref[...] = jnp.zeros(o_ref.shape, F32)
        for t in range(tq):
            for h in range(HEADS):
                r = t * 2 * HEADS + 2 * h
                o_ref[t:t + 1, h * DV:(h + 1) * DV] = accn[r:r + 1, :] - lam * accn[r + 1:r + 2, :]


def _attn_sample(page_table, rel_bias, lam_q, lam_k, qn, kn, v, cache_k, cache_v, tq, pages):
    nb, n_pages = page_table.shape
    rpb = qn.shape[0] // nb
    pcols = cache_k.shape[1]
    ng = n_pages // pages
    nrow = 2 * HEADS * tq
    assert nb * ng >= RING_SLOTS - 1
    small = lambda a: pl.BlockSpec(a.shape, lambda b, g, pt: (0,) * a.ndim)
    grid_spec = pltpu.PrefetchScalarGridSpec(
        num_scalar_prefetch=1,
        grid=(nb, ng),
        in_specs=[pl.BlockSpec(memory_space=pltpu.SMEM), small(lam_q), small(lam_k),
                  pl.BlockSpec((rpb, QKW), lambda b, g, pt: (b, 0)),
                  pl.BlockSpec((rpb * HEADS, LANES), lambda b, g, pt: (b, 0)),
                  pl.BlockSpec((rpb * HEADS, LANES), lambda b, g, pt: (b, 0)),
                  pl.BlockSpec(memory_space=pl.ANY), pl.BlockSpec(memory_space=pl.ANY)],
        out_specs=pl.BlockSpec((rpb, VW), lambda b, g, pt: (b, 0)),
        scratch_shapes=[pltpu.VMEM((RING_SLOTS, pages, pcols, LANES), F32),
                        pltpu.VMEM((RING_SLOTS, pages, pcols, LANES), F32),
                        pltpu.SemaphoreType.DMA((RING_SLOTS, 2, pages)),
                        pltpu.VMEM((nrow, LANES), BF16), pltpu.VMEM((nrow, pcols), F32), pltpu.VMEM((nrow, pcols), F32),
                        pltpu.VMEM((nrow, LANES), F32), pltpu.VMEM((nrow, 1), F32), pltpu.VMEM((nrow, 1), F32),
                        pltpu.VMEM((nrow, LANES), F32)],
    )
    return pl.pallas_call(
        functools.partial(_attn_sample_kernel, pages=pages, tq=tq),
        grid_spec=grid_spec,
        out_shape=jax.ShapeDtypeStruct((nb * rpb, VW), F32),
        compiler_params=pltpu.CompilerParams(dimension_semantics=("arbitrary", "arbitrary"),
                                             vmem_limit_bytes=VMEM_LIMIT),
        name="attn_sample",
    )(page_table, rel_bias, lam_q, lam_k, qn, kn, v, cache_k, cache_v)


def _gla_kernel(s0_ref, q_ref, k_ref, v_ref, lg_ref, tri_ref, o_ref, sout_ref, s_sc, *, valid):
    c = pl.program_id(1)
    nseq = q_ref.shape[0]

    @pl.when(c == 0)
    def _():
        for i in range(nseq):
            s_sc[i] = s0_ref[0 if s0_ref.shape[0] != nseq else i]

    rows_in = q_ref.shape[1]

    def padded(a):
        if rows_in == CHUNK:
            return a
        return jnp.concatenate([a, jnp.zeros((CHUNK - rows_in, a.shape[1]), a.dtype)], axis=0)

    wide = lambda ref: jnp.concatenate([padded(ref[i]) for i in range(nseq)], axis=1)
    q = wide(q_ref)
    k = wide(k_ref)
    lg = wide(lg_ref)
    vb = wide(v_ref)
    if valid < CHUNK:
        live = lax.broadcasted_iota(jnp.int32, (CHUNK, 1), 0) < valid
        k = jnp.where(live, k, 0.0)
        lg = jnp.where(live, lg, 0.0)
        vb = jnp.where(live, vb, jnp.zeros_like(vb))

    lg_hi = lg.astype(BF16)
    lg_lo = (lg - lg_hi.astype(F32)).astype(BF16)
    tri = tri_ref[...]
    b = _dot(tri, lg_hi) + _dot(tri, lg_lo)
    b_mid = b[CHUNK // 2 - 1:CHUNK // 2, :]
    qs = q * jnp.exp(b)
    qt = q * jnp.exp(b - b_mid)
    kt = (k * jnp.exp(b_mid - b)).astype(BF16)
    b_t = b.T
    b_last = b_t[:, CHUNK - 1:CHUNK]
    kl_t = (k.T * jnp.exp(b_last - b_t)).astype(BF16)
    decay = jnp.exp(b_last)

    row = lax.broadcasted_iota(jnp.int32, (CHUNK, CHUNK), 0)
    col = lax.broadcasted_iota(jnp.int32, (CHUNK, CHUNK), 1)
    causal = row >= col
    lane = lax.broadcasted_iota(jnp.int32, (1, LANES), 1)
    heads = [(i, h) for i in range(nseq) for h in range(HEADS)]
    cols = lambda i, h: slice(i * GW + (h // 2) * LANES, i * GW + (h // 2 + 1) * LANES)
    mine = lambda h: (lane // DK) == (h % 2)
    v_of = lambda i, h: vb[:, i * VW + h * DV:i * VW + (h + 1) * DV]
    s_old = [s_sc[i] for i in range(nseq)]
    s_bf = [s.astype(BF16) for s in s_old]
    a = [_dot_nt(jnp.where(mine(h), qt[:, cols(i, h)], 0.0).astype(BF16), kt[:, cols(i, h)]) for i, h in heads]
    inter = [_dot(jnp.where(mine(h), qs[:, cols(i, h)], 0.0).astype(BF16),
                  s_bf[i][(h // 2) * LANES:(h // 2 + 1) * LANES, :]) for i, h in heads]
    upd = [_dot(kl_t[i * GW + h * DK:i * GW + (h + 1) * DK, :], v_of(i, h)) for i, h in heads]
    for n, (i, h) in enumerate(heads):
        o_h = inter[n] + _dot(jnp.where(causal, a[n], 0.0).astype(BF16), v_of(i, h))
        o_ref[i, :, h * DV:(h + 1) * DV] = o_h[0:rows_in]
        rows = slice(i * GW + h * DK, i * GW + (h + 1) * DK)
        s_sc[i, h * DK:(h + 1) * DK, :] = decay[rows, :] * s_old[i][h * DK:(h + 1) * DK, :] + upd[n]

    @pl.when(c == pl.num_programs(1) - 1)
    def _():
        sout_ref[...] = s_sc[...]


def _gla(s0, gq, gk, gv, lg, tri, nb, rows_in, valid):
    nc = gq.shape[0] // (nb * rows_in)
    per = min(nb, GLA_SEQS_PER_STEP)
    assert nb % per == 0
    s0_blk = per if s0.shape[0] == nb else 1
    s0_map = (lambda b, c: (b, 0, 0)) if s0.shape[0] == nb else (lambda b, c: (0, 0, 0))
    seqs = lambda a: a.reshape(nb, nc * rows_in, a.shape[-1])
    rowblk = lambda w: pl.BlockSpec((per, rows_in, w), lambda b, c: (b, c, 0))
    o, s_fin = pl.pallas_call(
        functools.partial(_gla_kernel, valid=valid),
        grid=(nb // per, nc),
        in_specs=[pl.BlockSpec((s0_blk, GW, DV), s0_map), rowblk(GW), rowblk(GW), rowblk(VW), rowblk(GW),
                  pl.BlockSpec(tri.shape, lambda b, c: (0, 0))],
        out_specs=[rowblk(VW), pl.BlockSpec((per, GW, DV), lambda b, c: (b, 0, 0))],
        out_shape=[jax.ShapeDtypeStruct((nb, nc * rows_in, VW), F32), jax.ShapeDtypeStruct((nb, GW, DV), F32)],
        scratch_shapes=[pltpu.VMEM((per, GW, DV), F32)],
        compiler_params=pltpu.CompilerParams(dimension_semantics=("arbitrary", "arbitrary"),
                                             vmem_limit_bytes=VMEM_LIMIT),
        name="gla",
    )(s0, seqs(gq), seqs(gk), seqs(gv), seqs(lg), tri)
    return o.reshape(nb * nc * rows_in, VW), s_fin


def _merge_kernel(x_ref, od_ref, og_ref, gate_ref, dnw_ref, gnw_ref, wtop_ref, wbot_ref, y_ref):
    def head_norm(o, w):
        parts = []
        for h in range(HEADS):
            sl = o[:, h * DV:(h + 1) * DV]
            parts.append(sl * lax.rsqrt(jnp.mean(sl * sl, axis=-1, keepdims=True) + RMS_EPS))
        return jnp.concatenate(parts, axis=-1) * w

    gate = gate_ref[...].astype(F32)
    mix_d = (head_norm(od_ref[...], dnw_ref[...]) * gate[:, 0:VW]).astype(BF16)
    mix_g = (head_norm(og_ref[...], gnw_ref[...]) * gate[:, VW:]).astype(BF16)
    y_ref[...] = x_ref[...] + _dot(mix_d, wtop_ref[...]) + _dot(mix_g, wbot_ref[...])


def _merge(x, od, og, gates, dnw, gnw, wtop, wbot, tr):
    rows = x.shape[0]
    full = lambda a: pl.BlockSpec(a.shape, lambda i: (0,) * a.ndim)
    row = lambda w: pl.BlockSpec((tr, w), lambda i: (i, 0))
    return pl.pallas_call(
        _merge_kernel,
        grid=(rows // tr,),
        in_specs=[row(D_MODEL), row(VW), row(VW), row(2 * VW), full(dnw), full(gnw), full(wtop), full(wbot)],
        out_specs=row(D_MODEL),
        out_shape=jax.ShapeDtypeStruct((rows, D_MODEL), F32),
        compiler_params=pltpu.CompilerParams(dimension_semantics=("arbitrary",), vmem_limit_bytes=VMEM_LIMIT),
        name="merge",
    )(x, od, og, gates, dnw, gnw, wtop, wbot)


def kernel(x_prompt, x_sample, cache_k, cache_v, state_gla, page_table, meta_tokens, rel_bias, norm_w, w_in,
           q_norm_w, k_norm_w, lam_q, lam_k, diff_norm_w, gla_wa2, gla_ba, gla_norm_w, w_out):
    batch, seq, _ = x_prompt.shape
    nb, tq, _ = x_sample.shape
    tile = 512
    rpb = 16

    w = w_in[0]
    wm = w[:, :Z_MAIN].astype(BF16)
    wa1 = jnp.pad(w[:, Z_MAIN:], ((0, 0), (0, LANES - GLA_GATE_RANK))).astype(BF16)
    wa2 = jnp.pad(gla_wa2[0], ((0, LANES - GLA_GATE_RANK), (0, 0))).astype(BF16)
    ba = gla_ba[0][None]
    nw = norm_w[0][None]
    qw = jnp.tile(q_norm_w[0].reshape(-1), HEADS)[None] * (DIFF_SCALE * LOG2E)
    kw = jnp.tile(k_norm_w[0].reshape(-1), HEADS)[None]
    grp = np.arange(256) // DK
    gmat = jnp.asarray(grp[:, None] == grp[None, :], BF16)
    consts = (nw, wm, wa1, wa2, ba, qw, kw, gmat)
    dnw = jnp.tile(diff_norm_w[0], HEADS)[None] * (1.0 - LAM_INIT)
    gnw = jnp.tile(gla_norm_w[0], HEADS)[None]
    wtop = w_out[0][:VW].astype(BF16)
    wbot = w_out[0][VW:].astype(BF16)
    t_idx = np.arange(CHUNK)
    tri = jnp.asarray(t_idx[:, None] >= t_idx[None, :], BF16)
    lq, lk = lam_q[0], lam_k[0]

    xp = x_prompt.reshape(batch * seq, D_MODEL)
    p_qn, p_kn, p_knb, p_v, p_vb, p_gate, p_gq, p_gk, p_gv, p_lg = _inproj(xp, 512, consts)

    xs = jnp.pad(x_sample, ((0, 0), (0, rpb - tq), (0, 0))).reshape(nb * rpb, D_MODEL)
    xm = jnp.pad(meta_tokens, ((0, CHUNK - N_META), (0, 0)))
    x_small = jnp.concatenate([xs, xm], axis=0)
    ns = nb * rpb
    small = _inproj(x_small, x_small.shape[0], consts)
    s_qn, s_kn, s_knb, s_v, s_vb, s_gate, s_gq, s_gk, s_gv, s_lg = [a[:ns] for a in small]
    m_qn, m_kn, m_knb, m_v, m_vb, m_gate, m_gq, m_gk, m_gv, m_lg = [a[ns:] for a in small]

    o_d = _attn_prompt(rel_bias, lq, lk, p_qn.reshape(batch, seq, QKW), p_knb.reshape(batch, seq, QKW),
                       p_vb.reshape(batch, seq, VW), m_knb, m_vb, tile)

    zero_state = jnp.zeros((1, GW, DV), F32)
    _, s_meta = _gla(zero_state, m_gq, m_gk, m_gv, m_lg, tri, 1, CHUNK, N_META)
    o_g, s_fin = _gla(s_meta, p_gq, p_gk, p_gv, p_lg, tri, batch, CHUNK, CHUNK)
    os_g, s_new = _gla(state_gla[0].reshape(nb, GW, DV), s_gq, s_gk, s_gv, s_lg, tri, nb, rpb, tq)

    n_pool, page_size = cache_k.shape[1], cache_k.shape[2]
    ck = cache_k.reshape(n_pool, page_size * HEADS, 2 * DK)
    cv = cache_v.reshape(n_pool, page_size * HEADS, DV)
    os_d = _attn_sample(page_table, rel_bias, lq, lk, s_qn, s_kn.reshape(ns * HEADS, 2 * DK),
                        s_v.reshape(ns * HEADS, DV), ck, cv, tq, 16)

    y_prompt = _merge(xp, o_d.reshape(batch * seq, VW), o_g, p_gate, dnw, gnw, wtop, wbot, 512)
    y_small = _merge(xs, os_d, os_g, s_gate, dnw, gnw, wtop, wbot, ns)

    y_prompt = y_prompt.reshape(batch, seq, D_MODEL)
    y_sample = y_small.reshape(nb, rpb, D_MODEL)[:, :tq]
    k_meta = jnp.broadcast_to(m_kn[:N_META][None], (batch, N_META, QKW))
    v_meta = jnp.broadcast_to(m_v[:N_META][None], (batch, N_META, VW))
    k_prompt = jnp.concatenate([k_meta, p_kn.reshape(batch, seq, QKW)], axis=1)
    v_prompt = jnp.concatenate([v_meta, p_v.reshape(batch, seq, VW)], axis=1)
    k_prompt = k_prompt.reshape(1, batch, seq + N_META, HEADS, 2 * DK)
    v_prompt = v_prompt.reshape(1, batch, seq + N_META, HEADS, DV)
    s_prompt = s_fin.reshape(1, batch, HEADS, DK, DV)
    k_sample = s_kn.reshape(nb, rpb, HEADS, 2 * DK)[None, :, :tq]
    v_sample = s_v.reshape(nb, rpb, HEADS, DV)[None, :, :tq]
    s_sample = s_new.reshape(1, nb, HEADS, DK, DV)
    return (y_prompt, y_sample, k_prompt, v_prompt, s_prompt, k_sample, v_sample, s_sample)
```

```python
import functools
import math

import numpy as np
import jax
import jax.numpy as jnp
from jax import lax
from jax.experimental import pallas as pl
from jax.experimental.pallas import tpu as pltpu

D_MODEL = 1024
N_META = 16
HEADS = 4
DK = 64
DV = 128
DIFF_SCALE = DK ** -0.5
GLA_GATE_RANK = 16
GLA_GATE_NORM = 16.0
NUM_BUCKETS = 32
MAX_DISTANCE = 128
RMS_EPS = 1e-6
LAM_INIT = 0.8 - 0.6 * math.exp(-0.3 * 0)
QKW = HEADS * 2 * DK
VW = HEADS * DV
GW = HEADS * DK
Z_MAIN = 3 * QKW + VW + 2 * GW + 2 * VW
LANES = 128
CHUNK = 128
NEG = -1e30
LOG2E = math.log2(math.e)
RING_SLOTS = 3
GLA_SEQS_PER_STEP = 4
VMEM_LIMIT = 56 * 1024 * 1024

F32 = jnp.float32
BF16 = jnp.bfloat16


def _dot(a, b):
    return jnp.dot(a, b, preferred_element_type=F32)


def _dot_nt(a, b):
    return lax.dot_general(a, b, (((1,), (1,)), ((), ())), preferred_element_type=F32)


def _dot_tn(a, b):
    return lax.dot_general(a, b, (((0,), (0,)), ((), ())), preferred_element_type=F32)


def _lam_value(lamq_ref, lamk_ref):
    e = jnp.exp(jnp.sum(lamq_ref[...] * lamk_ref[...], axis=-1, keepdims=True))
    return e[0:1, :] - e[1:2, :] + LAM_INIT


def _inproj_kernel(x_ref, nw_ref, wm_ref, wa1_ref, wa2_ref, ba_ref, qw_ref, kw_ref, g_ref,
                   qn_ref, kn_ref, knb_ref, v_ref, vb_ref, gate_ref, gq_ref, gk_ref, gv_ref, lg_ref):
    x = x_ref[...]
    ms = jnp.mean(x * x, axis=-1, keepdims=True)
    hn = (x * lax.rsqrt(ms + RMS_EPS) * nw_ref[...]).astype(BF16)

    def proj(lo, hi):
        return _dot(hn, wm_ref[:, lo:hi])

    def group_norm(z, w):
        sq = (z * z).astype(BF16)
        ss = jnp.concatenate([_dot(sq[:, c:c + 256], g_ref[...]) for c in range(0, QKW, 256)], axis=-1)
        return z * lax.rsqrt(ss * (1.0 / DK) + RMS_EPS) * w

    qn_ref[...] = group_norm(proj(0, QKW), qw_ref[...]).astype(BF16)
    kn = group_norm(proj(QKW, 2 * QKW), kw_ref[...])
    kn_ref[...] = kn
    knb_ref[...] = kn.astype(BF16)
    v = proj(2 * QKW, 2 * QKW + VW)
    v_ref[...] = v
    vb_ref[...] = v.astype(BF16)
    o = 2 * QKW + VW
    dg = proj(o, o + VW)
    gate_ref[:, 0:VW] = (dg * jax.nn.sigmoid(dg)).astype(BF16)
    o += VW
    gq_ref[...] = proj(o, o + GW) * (DK ** -0.5)
    gk_ref[...] = proj(o + GW, o + 2 * GW)
    o += 2 * GW
    gv_ref[...] = proj(o, o + VW).astype(BF16)
    o += VW
    gg = proj(o, o + VW)
    gate_ref[:, VW:2 * VW] = (gg * jax.nn.sigmoid(gg)).astype(BF16)
    ga = _dot(hn, wa1_ref[...])
    xg = _dot(ga.astype(BF16), wa2_ref[...]) + ba_ref[...]
    lg_ref[...] = (jnp.minimum(xg, 0.0) - jnp.log(1.0 + jnp.exp(-jnp.abs(xg)))) * (1.0 / GLA_GATE_NORM)


def _inproj(x, tr, consts):
    rows = x.shape[0]
    full = lambda a: pl.BlockSpec(a.shape, lambda i: (0,) * a.ndim)
    row = lambda w: pl.BlockSpec((tr, w), lambda i: (i, 0))
    outs = [(QKW, BF16), (QKW, F32), (QKW, BF16), (VW, F32), (VW, BF16), (2 * VW, BF16),
            (GW, F32), (GW, F32), (VW, BF16), (GW, F32)]
    return pl.pallas_call(
        _inproj_kernel,
        grid=(rows // tr,),
        in_specs=[row(D_MODEL)] + [full(a) for a in consts],
        out_specs=[row(w) for w, _ in outs],
        out_shape=[jax.ShapeDtypeStruct((rows, w), dt) for w, dt in outs],
        compiler_params=pltpu.CompilerParams(dimension_semantics=("arbitrary",), vmem_limit_bytes=VMEM_LIMIT),
        name="inproj",
    )(x, *consts)


def _bucket_ranges():
    n = np.arange(MAX_DISTANCE)
    max_exact = NUM_BUCKETS // 2
    nf = np.maximum(n, 1).astype(np.float32)
    large = max_exact + (np.log(nf / np.float32(max_exact)) / np.float32(math.log(MAX_DISTANCE / max_exact))
                         * np.float32(NUM_BUCKETS - max_exact)).astype(np.int32)
    bucket = np.where(n < max_exact, n, np.minimum(large, NUM_BUCKETS - 1))
    return [(int(n[bucket == b].min()), int(n[bucket == b].max())) for b in range(NUM_BUCKETS)]


_BUCKET_RANGES = _bucket_ranges()


def _bias_table(n, rb_ref, h):
    far = rb_ref[NUM_BUCKETS - 1, h]
    t = jnp.zeros(n.shape, F32)
    for b, (lo, hi) in enumerate(_BUCKET_RANGES[:-1]):
        val = (rb_ref[b, h] - far) * LOG2E
        cond = (n == lo) if lo == hi else ((n >= lo) & (n <= hi))
        t = jnp.where(cond, val, t)
    return jnp.where(n < 0, NEG, t)


def _attn_prompt_kernel(rb_ref, lamq_ref, lamk_ref, qa_ref, qb_ref, k_ref, v_ref, km_ref, vm_ref, oa_ref, ob_ref,
                        bias_sc, mt_sc, qs_sc, m_sc, l_sc, acc_sc, *, tile, nq):
    h = pl.program_id(0)
    b = pl.program_id(1)
    ip = pl.program_id(2)
    half = nq // 2

    @pl.when((b == 0) & (ip == 0))
    def _():
        kk = lax.broadcasted_iota(jnp.int32, (tile, tile), 0)
        qq = lax.broadcasted_iota(jnp.int32, (tile, tile), 1)
        bias_sc[0] = jnp.zeros((tile, tile), F32)
        bias_sc[1] = _bias_table(qq - kk + tile, rb_ref, h)
        bias_sc[2] = _bias_table(qq - kk, rb_ref, h)
        km = lax.broadcasted_iota(jnp.int32, (N_META, tile), 0)
        qm = lax.broadcasted_iota(jnp.int32, (N_META, tile), 1)
        mt_sc[...] = _bias_table(N_META + qm - km, rb_ref, h)

    lane = lax.broadcasted_iota(jnp.int32, (1, LANES), 1)
    for t, q_ref in enumerate((qa_ref, qb_ref)):
        q = q_ref[...]
        zero = jnp.zeros_like(q)
        qs_sc[t] = jnp.concatenate([jnp.where(lane < DK, q, zero), jnp.where(lane >= DK, q, zero)], axis=0)
    m_sc[...] = jnp.full(m_sc.shape, NEG, F32)
    l_sc[...] = jnp.zeros(l_sc.shape, F32)
    acc_sc[...] = jnp.zeros(acc_sc.shape, F32)

    def plan(n):
        if n < half:
            slot = jnp.where(n > ip, 1, 0)
            key_tile = jnp.where(n <= ip, n, n - ip - 1)
            return slot, key_tile, jnp.where(n == ip, 2, jnp.where(n == ip - 1, 1, 0))
        return 1, n - ip - 1, {nq - 1: 1, nq: 2}.get(n)

    def rows(key_tile):
        return pl.ds(pl.multiple_of(key_tile * tile, tile), tile)

    def scores(slot, key_tile, kind):
        s = _dot_nt(k_ref[rows(key_tile), :], qs_sc[slot])
        if kind is None:
            return s
        bias = bias_sc[kind]
        return s + jnp.concatenate([bias, bias], axis=1)

    def update(slot, s, vt):
        m_prev = m_sc[slot]
        m_new = jnp.maximum(m_prev, jnp.max(s, axis=0, keepdims=True))
        alpha = jnp.exp2(m_prev - m_new)
        p = jnp.exp2(s - m_new)
        l_sc[slot] = alpha * l_sc[slot] + jnp.sum(p, axis=0, keepdims=True)
        acc_sc[slot] = alpha * acc_sc[slot] + _dot_tn(vt, p.astype(BF16))
        m_sc[slot] = m_new

    meta_bias = jnp.where(ip == 0, mt_sc[...], 0.0)
    update(0, _dot_nt(km_ref[...], qs_sc[0]) + jnp.concatenate([meta_bias, meta_bias], axis=1), vm_ref[...])
    update(1, _dot_nt(km_ref[...], qs_sc[1]), vm_ref[...])

    plans = [plan(n) for n in range(nq + 1)]
    s_next = scores(*plans[0])
    for n in range(nq + 1):
        s_cur = s_next
        if n < nq:
            s_next = scores(*plans[n + 1])
        update(plans[n][0], s_cur, v_ref[rows(plans[n][1]), :])

    lam = _lam_value(lamq_ref, lamk_ref)
    for t, o_ref in enumerate((oa_ref, ob_ref)):
        acc = acc_sc[t]
        inv = 1.0 / l_sc[t]
        o_t = acc[:, 0:tile] * inv[:, 0:tile] - (lam * inv[:, tile:]) * acc[:, tile:]
        o_ref[...] = o_t.T


def _attn_prompt(rel_bias, lam_q, lam_k, qn, knb, vb, km, vm, tile):
    batch, seq, _ = qn.shape
    nq = seq // tile
    half = nq // 2
    assert nq % 2 == 0
    small = lambda a: pl.BlockSpec(a.shape, lambda h, b, ip: (0,) * a.ndim)
    kv = pl.BlockSpec((None, seq, LANES), lambda h, b, ip: (b, 0, h))
    meta = pl.BlockSpec((N_META, LANES), lambda h, b, ip: (0, h))
    short = lambda h, b, ip: (b, ip, h)
    return pl.pallas_call(
        functools.partial(_attn_prompt_kernel, tile=tile, nq=nq),
        grid=(HEADS, batch, half),
        in_specs=[pl.BlockSpec(memory_space=pltpu.SMEM), small(lam_q), small(lam_k),
                  pl.BlockSpec((None, tile, LANES), short),
                  pl.BlockSpec((None, tile, LANES), lambda h, b, ip: (b, nq - 1 - ip, h)),
                  kv, kv, meta, meta],
        out_specs=[pl.BlockSpec((None, tile, LANES), short),
                   pl.BlockSpec((None, tile, LANES), lambda h, b, ip: (b, half - 1 - ip, h))],
        out_shape=[jax.ShapeDtypeStruct((batch, seq // 2, VW), F32)] * 2,
        scratch_shapes=[pltpu.VMEM((3, tile, tile), F32), pltpu.VMEM((N_META, tile), F32),
                        pltpu.VMEM((2, 2 * tile, LANES), BF16),
                        pltpu.VMEM((2, 1, 2 * tile), F32), pltpu.VMEM((2, 1, 2 * tile), F32),
                        pltpu.VMEM((2, LANES, 2 * tile), F32)],
        compiler_params=pltpu.CompilerParams(dimension_semantics=("arbitrary",) * 3, vmem_limit_bytes=VMEM_LIMIT),
        name="attn_prompt",
    )(rel_bias, lam_q, lam_k, qn, qn, knb, vb, km, vm)


def _attn_sample_kernel(pt_ref, rb_ref, lamq_ref, lamk_ref, q_ref, kn_ref, vn_ref, ck_hbm, cv_hbm, o_ref,
                        kbuf, vbuf, sem, w_sc, mask_sc, ptab_sc, ntab_sc, m_sc, l_sc, acc_sc, *, pages, tq):
    b = pl.program_id(0)
    g = pl.program_id(1)
    ng = pl.num_programs(1)
    step = b * ng + g
    total = pl.num_programs(0) * ng
    nrow = 2 * HEADS * tq
    pcols = kbuf.shape[2]
    page_size = pcols // HEADS

    def page_copies(st):
        slot = st % RING_SLOTS
        out = []
        for u in range(pages):
            page = pt_ref[st // ng, (st % ng) * pages + u]
            out.append(pltpu.make_async_copy(ck_hbm.at[page], kbuf.at[slot, u], sem.at[slot, 0, u]))
            out.append(pltpu.make_async_copy(cv_hbm.at[page], vbuf.at[slot, u], sem.at[slot, 1, u]))
        return out

    @pl.when(step == 0)
    def _():
        for ahead in range(RING_SLOTS - 1):
            for c in page_copies(ahead):
                c.start()

    @pl.when(step + RING_SLOTS - 1 < total)
    def _():
        for c in page_copies(step + RING_SLOTS - 1):
            c.start()

    @pl.when((b == 0) & (g == 0))
    def _():
        def tables(cols, offset):
            r = lax.broadcasted_iota(jnp.int32, (nrow, cols), 0)
            c = lax.broadcasted_iota(jnp.int32, (nrow, cols), 1)
            tok = r // (2 * HEADS)
            head = (r % (2 * HEADS)) // 2
            n = offset + tok - c // HEADS
            t = jnp.zeros((nrow, cols), F32)
            for h in range(HEADS):
                t = jnp.where(head == h, _bias_table(n, rb_ref, h), t)
            return jnp.where(head == c % HEADS, t, NEG), c // HEADS

        mask_sc[...] = jnp.where(tables(pcols, MAX_DISTANCE)[0] > 0.5 * NEG, 0.0, NEG)
        ptab_sc[...] = tables(pcols, page_size)[0]
        nt, slot = tables(LANES, 0)
        ntab_sc[...] = jnp.where(slot < tq, nt, NEG)

    def update(s, pv):
        m_prev = m_sc[...]
        m_new = jnp.maximum(m_prev, jnp.max(s, axis=-1, keepdims=True))
        alpha = jnp.exp2(m_prev - m_new)
        p = jnp.exp2(s - m_new)
        l_sc[...] = alpha * l_sc[...] + jnp.sum(p, axis=-1, keepdims=True)
        acc_sc[...] = alpha * acc_sc[...] + pv(p.astype(BF16))
        m_sc[...] = m_new

    @pl.when(g == 0)
    def _():
        q = q_ref[...].astype(F32)
        r8 = lax.broadcasted_iota(jnp.int32, (2 * HEADS, LANES), 0)
        lane = lax.broadcasted_iota(jnp.int32, (2 * HEADS, LANES), 1)
        blocks = []
        for t in range(tq):
            blk = jnp.zeros((2 * HEADS, LANES), F32)
            for h in range(HEADS):
                blk = jnp.where(r8 // 2 == h, jnp.broadcast_to(q[t:t + 1, h * LANES:(h + 1) * LANES], blk.shape), blk)
            blocks.append(jnp.where(lane // DK == r8 % 2, blk, 0.0))
        w_sc[...] = jnp.concatenate(blocks, axis=0).astype(BF16)
        m_sc[...] = jnp.full(m_sc.shape, NEG, F32)
        l_sc[...] = jnp.zeros(l_sc.shape, F32)
        acc_sc[...] = jnp.zeros(acc_sc.shape, F32)
        pad = jnp.zeros((LANES - kn_ref.shape[0], LANES), F32)
        kn = jnp.concatenate([kn_ref[...], pad], axis=0).astype(BF16)
        vn = jnp.concatenate([vn_ref[...], pad], axis=0).astype(BF16)
        s = _dot_nt(w_sc[...], kn) + ntab_sc[...]
        update(s, lambda p: _dot(p, vn))

    for c in page_copies(step):
        c.wait()
    slot = step % RING_SLOTS
    w = w_sc[...]
    last = g == ng - 1
    parts = []
    for u in range(pages):
        s = _dot_nt(w, kbuf[slot, u].astype(BF16))
        if u == pages - 1:
            s = s + jnp.where(last, ptab_sc[...], mask_sc[...])
        else:
            s = s + mask_sc[...]
        parts.append(s)
    s = jnp.concatenate(parts, axis=-1)

    def pv(p):
        acc = _dot(p[:, 0:pcols], vbuf[slot, 0].astype(BF16))
        for u in range(1, pages):
            acc += _dot(p[:, u * pcols:(u + 1) * pcols], vbuf[slot, u].astype(BF16))
        return acc

    update(s, pv)

    @pl.when(last)
    def _():
        lam = _lam_value(lamq_ref, lamk_ref)
        accn = acc_sc[...] * (1.0 / l_sc[...])
        o_ref[...] = jnp.zeros(o_ref.shape, F32)
        for t in range(tq):
            for h in range(HEADS):
                r = t * 2 * HEADS + 2 * h
                o_ref[t:t + 1, h * DV:(h + 1) * DV] = accn[r:r + 1, :] - lam * accn[r + 1:r + 2, :]


def _attn_sample(page_table, rel_bias, lam_q, lam_k, qn, kn, v, cache_k, cache_v, tq, pages):
    nb, n_pages = page_table.shape
    rpb = qn.shape[0] // nb
    pcols = cache_k.shape[1]
    ng = n_pages // pages
    nrow = 2 * HEADS * tq
    assert nb * ng >= RING_SLOTS - 1
    small = lambda a: pl.BlockSpec(a.shape, lambda b, g, pt: (0,) * a.ndim)
    grid_spec = pltpu.PrefetchScalarGridSpec(
        num_scalar_prefetch=1,
        grid=(nb, ng),
        in_specs=[pl.BlockSpec(memory_space=pltpu.SMEM), small(lam_q), small(lam_k),
                  pl.BlockSpec((rpb, QKW), lambda b, g, pt: (b, 0)),
                  pl.BlockSpec((rpb * HEADS, LANES), lambda b, g, pt: (b, 0)),
                  pl.BlockSpec((rpb * HEADS, LANES), lambda b, g, pt: (b, 0)),
                  pl.BlockSpec(memory_space=pl.ANY), pl.BlockSpec(memory_space=pl.ANY)],
        out_specs=pl.BlockSpec((rpb, VW), lambda b, g, pt: (b, 0)),
        scratch_shapes=[pltpu.VMEM((RING_SLOTS, pages, pcols, LANES), F32),
                        pltpu.VMEM((RING_SLOTS, pages, pcols, LANES), F32),
                        pltpu.SemaphoreType.DMA((RING_SLOTS, 2, pages)),
                        pltpu.VMEM((nrow, LANES), BF16), pltpu.VMEM((nrow, pcols), F32), pltpu.VMEM((nrow, pcols), F32),
                        pltpu.VMEM((nrow, LANES), F32), pltpu.VMEM((nrow, 1), F32), pltpu.VMEM((nrow, 1), F32),
                        pltpu.VMEM((nrow, LANES), F32)],
    )
    return pl.pallas_call(
        functools.partial(_attn_sample_kernel, pages=pages, tq=tq),
        grid_spec=grid_spec,
        out_shape=jax.ShapeDtypeStruct((nb * rpb, VW), F32),
        compiler_params=pltpu.CompilerParams(dimension_semantics=("arbitrary", "arbitrary"),
                                             vmem_limit_bytes=VMEM_LIMIT),
        name="attn_sample",
    )(page_table, rel_bias, lam_q, lam_k, qn, kn, v, cache_k, cache_v)


def _gla_kernel(s0_ref, q_ref, k_ref, v_ref, lg_ref, tri_ref, o_ref, sout_ref, s_sc, *, valid):
    c = pl.program_id(1)
    nseq = q_ref.shape[0]

    @pl.when(c == 0)
    def _():
        for i in range(nseq):
            s_sc[i] = s0_ref[0 if s0_ref.shape[0] != nseq else i]

    rows_in = q_ref.shape[1]

    def padded(a):
        if rows_in == CHUNK:
            return a
        return jnp.concatenate([a, jnp.zeros((CHUNK - rows_in, a.shape[1]), a.dtype)], axis=0)

    wide = lambda ref: jnp.concatenate([padded(ref[i]) for i in range(nseq)], axis=1)
    q = wide(q_ref)
    k = wide(k_ref)
    lg = wide(lg_ref)
    vb = wide(v_ref)
    if valid < CHUNK:
        live = lax.broadcasted_iota(jnp.int32, (CHUNK, 1), 0) < valid
        k = jnp.where(live, k, 0.0)
        lg = jnp.where(live, lg, 0.0)
        vb = jnp.where(live, vb, jnp.zeros_like(vb))

    lg_hi = lg.astype(BF16)
    lg_lo = (lg - lg_hi.astype(F32)).astype(BF16)
    tri = tri_ref[...]
    b = _dot(tri, lg_hi) + _dot(tri, lg_lo)
    b_mid = b[CHUNK // 2 - 1:CHUNK // 2, :]
    qs = q * jnp.exp(b)
    qt = q * jnp.exp(b - b_mid)
    kt = (k * jnp.exp(b_mid - b)).astype(BF16)
    b_t = b.T
    b_last = b_t[:, CHUNK - 1:CHUNK]
    kl_t = (k.T * jnp.exp(b_last - b_t)).astype(BF16)
    decay = jnp.exp(b_last)

    row = lax.broadcasted_iota(jnp.int32, (CHUNK, CHUNK), 0)
    col = lax.broadcasted_iota(jnp.int32, (CHUNK, CHUNK), 1)
    causal = row >= col
    lane = lax.broadcasted_iota(jnp.int32, (1, LANES), 1)
    heads = [(i, h) for i in range(nseq) for h in range(HEADS)]
    cols = lambda i, h: slice(i * GW + (h // 2) * LANES, i * GW + (h // 2 + 1) * LANES)
    mine = lambda h: (lane // DK) == (h % 2)
    v_of = lambda i, h: vb[:, i * VW + h * DV:i * VW + (h + 1) * DV]
    s_old = [s_sc[i] for i in range(nseq)]
    s_bf = [s.astype(BF16) for s in s_old]
    a = [_dot_nt(jnp.where(mine(h), qt[:, cols(i, h)], 0.0).astype(BF16), kt[:, cols(i, h)]) for i, h in heads]
    inter = [_dot(jnp.where(mine(h), qs[:, cols(i, h)], 0.0).astype(BF16),
                  s_bf[i][(h // 2) * LANES:(h // 2 + 1) * LANES, :]) for i, h in heads]
    upd = [_dot(kl_t[i * GW + h * DK:i * GW + (h + 1) * DK, :], v_of(i, h)) for i, h in heads]
    for n, (i, h) in enumerate(heads):
        o_h = inter[n] + _dot(jnp.where(causal, a[n], 0.0).astype(BF16), v_of(i, h))
        o_ref[i, :, h * DV:(h + 1) * DV] = o_h[0:rows_in]
        rows = slice(i * GW + h * DK, i * GW + (h + 1) * DK)
        s_sc[i, h * DK:(h + 1) * DK, :] = decay[rows, :] * s_old[i][h * DK:(h + 1) * DK, :] + upd[n]

    @pl.when(c == pl.num_programs(1) - 1)
    def _():
        sout_ref[...] = s_sc[...]


def _gla(s0, gq, gk, gv, lg, tri, nb, rows_in, valid):
    nc = gq.shape[0] // (nb * rows_in)
    per = min(nb, GLA_SEQS_PER_STEP)
    assert nb % per == 0
    s0_blk = per if s0.shape[0] == nb else 1
    s0_map = (lambda b, c: (b, 0, 0)) if s0.shape[0] == nb else (lambda b, c: (0, 0, 0))
    seqs = lambda a: a.reshape(nb, nc * rows_in, a.shape[-1])
    rowblk = lambda w: pl.BlockSpec((per, rows_in, w), lambda b, c: (b, c, 0))
    o, s_fin = pl.pallas_call(
        functools.partial(_gla_kernel, valid=valid),
        grid=(nb // per, nc),
        in_specs=[pl.BlockSpec((s0_blk, GW, DV), s0_map), rowblk(GW), rowblk(GW), rowblk(VW), rowblk(GW),
                  pl.BlockSpec(tri.shape, lambda b, c: (0, 0))],
        out_specs=[rowblk(VW), pl.BlockSpec((per, GW, DV), lambda b, c: (b, 0, 0))],
        out_shape=[jax.ShapeDtypeStruct((nb, nc * rows_in, VW), F32), jax.ShapeDtypeStruct((nb, GW, DV), F32)],
        scratch_shapes=[pltpu.VMEM((per, GW, DV), F32)],
        compiler_params=pltpu.CompilerParams(dimension_semantics=("arbitrary", "arbitrary"),
                                             vmem_limit_bytes=VMEM_LIMIT),
        name="gla",
    )(s0, seqs(gq), seqs(gk), seqs(gv), seqs(lg), tri)
    return o.reshape(nb * nc * rows_in, VW), s_fin


def _merge_kernel(x_ref, od_ref, og_ref, gate_ref, dnw_ref, gnw_ref, wtop_ref, wbot_ref, y_ref):
    def head_norm(o, w):
        parts = []
        for h in range(HEADS):
            sl = o[:, h * DV:(h + 1) * DV]
            parts.append(sl * lax.rsqrt(jnp.mean(sl * sl, axis=-1, keepdims=True) + RMS_EPS))
        return jnp.concatenate(parts, axis=-1) * w

    gate = gate_ref[...].astype(F32)
    mix_d = (head_norm(od_ref[...], dnw_ref[...]) * gate[:, 0:VW]).astype(BF16)
    mix_g = (head_norm(og_ref[...], gnw_ref[...]) * gate[:, VW:]).astype(BF16)
    y_ref[...] = x_ref[...] + _dot(mix_d, wtop_ref[...]) + _dot(mix_g, wbot_ref[...])


def _merge(x, od, og, gates, dnw, gnw, wtop, wbot, tr):
    rows = x.shape[0]
    full = lambda a: pl.BlockSpec(a.shape, lambda i: (0,) * a.ndim)
    row = lambda w: pl.BlockSpec((tr, w), lambda i: (i, 0))
    return pl.pallas_call(
        _merge_kernel,
        grid=(rows // tr,),
        in_specs=[row(D_MODEL), row(VW), row(VW), row(2 * VW), full(dnw), full(gnw), full(wtop), full(wbot)],
        out_specs=row(D_MODEL),
        out_shape=jax.ShapeDtypeStruct((rows, D_MODEL), F32),
        compiler_params=pltpu.CompilerParams(dimension_semantics=("arbitrary",), vmem_limit_bytes=VMEM_LIMIT),
        name="merge",
    )(x, od, og, gates, dnw, gnw, wtop, wbot)


def kernel(x_prompt, x_sample, cache_k, cache_v, state_gla, page_table, meta_tokens, rel_bias, norm_w, w_in,
           q_norm_w, k_norm_w, lam_q, lam_k, diff_norm_w, gla_wa2, gla_ba, gla_norm_w, w_out):
    batch, seq, _ = x_prompt.shape
    nb, tq, _ = x_sample.shape
    tile = 512
    rpb = 16

    w = w_in[0]
    wm = w[:, :Z_MAIN].astype(BF16)
    wa1 = jnp.pad(w[:, Z_MAIN:], ((0, 0), (0, LANES - GLA_GATE_RANK))).astype(BF16)
    wa2 = jnp.pad(gla_wa2[0], ((0, LANES - GLA_GATE_RANK), (0, 0))).astype(BF16)
    ba = gla_ba[0][None]
    nw = norm_w[0][None]
    qw = jnp.tile(q_norm_w[0].reshape(-1), HEADS)[None] * (DIFF_SCALE * LOG2E)
    kw = jnp.tile(k_norm_w[0].reshape(-1), HEADS)[None]
    grp = np.arange(256) // DK
    gmat = jnp.asarray(grp[:, None] == grp[None, :], BF16)
    consts = (nw, wm, wa1, wa2, ba, qw, kw, gmat)
    dnw = jnp.tile(diff_norm_w[0], HEADS)[None] * (1.0 - LAM_INIT)
    gnw = jnp.tile(gla_norm_w[0], HEADS)[None]
    wtop = w_out[0][:VW].astype(BF16)
    wbot = w_out[0][VW:].astype(BF16)
    t_idx = np.arange(CHUNK)
    tri = jnp.asarray(t_idx[:, None] >= t_idx[None, :], BF16)
    lq, lk = lam_q[0], lam_k[0]

    xp = x_prompt.reshape(batch * seq, D_MODEL)
    p_qn, p_kn, p_knb, p_v, p_vb, p_gate, p_gq, p_gk, p_gv, p_lg = _inproj(xp, 512, consts)

    xs = jnp.pad(x_sample, ((0, 0), (0, rpb - tq), (0, 0))).reshape(nb * rpb, D_MODEL)
    xm = jnp.pad(meta_tokens, ((0, CHUNK - N_META), (0, 0)))
    x_small = jnp.concatenate([xs, xm], axis=0)
    ns = nb * rpb
    small = _inproj(x_small, x_small.shape[0], consts)
    s_qn, s_kn, s_knb, s_v, s_vb, s_gate, s_gq, s_gk, s_gv, s_lg = [a[:ns] for a in small]
    m_qn, m_kn, m_knb, m_v, m_vb, m_gate, m_gq, m_gk, m_gv, m_lg = [a[ns:] for a in small]

    o_d = jnp.concatenate(_attn_prompt(rel_bias, lq, lk, p_qn.reshape(batch, seq, QKW), p_knb.reshape(batch, seq, QKW),
                                       p_vb.reshape(batch, seq, VW), m_knb, m_vb, tile), axis=1)

    zero_state = jnp.zeros((1, GW, DV), F32)
    _, s_meta = _gla(zero_state, m_gq, m_gk, m_gv, m_lg, tri, 1, CHUNK, N_META)
    o_g, s_fin = _gla(s_meta, p_gq, p_gk, p_gv, p_lg, tri, batch, CHUNK, CHUNK)
    os_g, s_new = _gla(state_gla[0].reshape(nb, GW, DV), s_gq, s_gk, s_gv, s_lg, tri, nb, rpb, tq)

    n_pool, page_size = cache_k.shape[1], cache_k.shape[2]
    ck = cache_k.reshape(n_pool, page_size * HEADS, 2 * DK)
    cv = cache_v.reshape(n_pool, page_size * HEADS, DV)
    os_d = _attn_sample(page_table, rel_bias, lq, lk, s_qn, s_kn.reshape(ns * HEADS, 2 * DK),
                        s_v.reshape(ns * HEADS, DV), ck, cv, tq, 16)

    y_prompt = _merge(xp, o_d.reshape(batch * seq, VW), o_g, p_gate, dnw, gnw, wtop, wbot, 512)
    y_small = _merge(xs, os_d, os_g, s_gate, dnw, gnw, wtop, wbot, ns)

    y_prompt = y_prompt.reshape(batch, seq, D_MODEL)
    y_sample = y_small.reshape(nb, rpb, D_MODEL)[:, :tq]
    k_meta = jnp.broadcast_to(m_kn[:N_META][None], (batch, N_META, QKW))
    v_meta = jnp.broadcast_to(m_v[:N_META][None], (batch, N_META, VW))
    k_prompt = jnp.concatenate([k_meta, p_kn.reshape(batch, seq, QKW)], axis=1)
    v_prompt = jnp.concatenate([v_meta, p_v.reshape(batch, seq, VW)], axis=1)
    k_prompt = k_prompt.reshape(1, batch, seq + N_META, HEADS, 2 * DK)
    v_prompt = v_prompt.reshape(1, batch, seq + N_META, HEADS, DV)
    s_prompt = s_fin.reshape(1, batch, HEADS, DK, DV)
    k_sample = s_kn.reshape(nb, rpb, HEADS, 2 * DK)[None, :, :tq]
    v_sample = s_v.reshape(nb, rpb, HEADS, DV)[None, :, :tq]
    s_sample = s_new.reshape(1, nb, HEADS, DK, DV)
    return (y_prompt, y_sample, k_prompt, v_prompt, s_prompt, k_sample, v_sample, s_sample)
```

```python
import functools
import math

import numpy as np
import jax
import jax.numpy as jnp
from jax import lax
from jax.experimental import pallas as pl
from jax.experimental.pallas import tpu as pltpu

D_MODEL = 1024
N_META = 16
HEADS = 4
DK = 64
DV = 128
DIFF_SCALE = DK ** -0.5
GLA_GATE_RANK = 16
GLA_GATE_NORM = 16.0
NUM_BUCKETS = 32
MAX_DISTANCE = 128
RMS_EPS = 1e-6
LAM_INIT = 0.8 - 0.6 * math.exp(-0.3 * 0)
QKW = HEADS * 2 * DK
VW = HEADS * DV
GW = HEADS * DK
Z_MAIN = 3 * QKW + VW + 2 * GW + 2 * VW
LANES = 128
CHUNK = 128
NEG = -1e30
LOG2E = math.log2(math.e)
RING_SLOTS = 3
GLA_SEQS_PER_STEP = 4
VMEM_LIMIT = 56 * 1024 * 1024

F32 = jnp.float32
BF16 = jnp.bfloat16


def _dot(a, b):
    return jnp.dot(a, b, preferred_element_type=F32)


def _dot_nt(a, b):
    return lax.dot_general(a, b, (((1,), (1,)), ((), ())), preferred_element_type=F32)


def _dot_tn(a, b):
    return lax.dot_general(a, b, (((0,), (0,)), ((), ())), preferred_element_type=F32)


def _lam_value(lamq_ref, lamk_ref):
    e = jnp.exp(jnp.sum(lamq_ref[...] * lamk_ref[...], axis=-1, keepdims=True))
    return e[0:1, :] - e[1:2, :] + LAM_INIT


def _inproj_kernel(x_ref, nw_ref, wm_ref, wa1_ref, wa2_ref, ba_ref, qw_ref, kw_ref, g_ref, *rest, seq_tiles):
    if seq_tiles is None:
        qn_ref, kn_ref, knb_ref, v_ref, vb_ref, gate_ref, gq_ref, gk_ref, gv_ref, lg_ref = rest
    else:
        (mk_ref, mv_ref, qn_ref, kn_hbm, knb_ref, v_hbm, vb_ref, gate_ref, gq_ref, gk_ref, gv_ref, lg_ref,
         stage, sem, meta_sem) = rest
    x = x_ref[...]
    ms = jnp.mean(x * x, axis=-1, keepdims=True)
    hn = (x * lax.rsqrt(ms + RMS_EPS) * nw_ref[...]).astype(BF16)

    def proj(lo, hi):
        return _dot(hn, wm_ref[:, lo:hi])

    def group_norm(z, w):
        sq = (z * z).astype(BF16)
        ss = jnp.concatenate([_dot(sq[:, c:c + 256], g_ref[...]) for c in range(0, QKW, 256)], axis=-1)
        return z * lax.rsqrt(ss * (1.0 / DK) + RMS_EPS) * w

    qn_ref[...] = group_norm(proj(0, QKW), qw_ref[...]).astype(BF16)
    kn = group_norm(proj(QKW, 2 * QKW), kw_ref[...])
    knb_ref[...] = kn.astype(BF16)
    v = proj(2 * QKW, 2 * QKW + VW)
    vb_ref[...] = v.astype(BF16)
    if seq_tiles is None:
        kn_ref[...] = kn
        v_ref[...] = v
    else:
        _write_cache_rows(kn, v, mk_ref, mv_ref, kn_hbm, v_hbm, stage, sem, meta_sem, seq_tiles)
    o = 2 * QKW + VW
    dg = proj(o, o + VW)
    gate_ref[:, 0:VW] = (dg * jax.nn.sigmoid(dg)).astype(BF16)
    o += VW
    gq_ref[...] = proj(o, o + GW) * (DK ** -0.5)
    gk_ref[...] = proj(o + GW, o + 2 * GW)
    o += 2 * GW
    gv_ref[...] = proj(o, o + VW).astype(BF16)
    o += VW
    gg = proj(o, o + VW)
    gate_ref[:, VW:2 * VW] = (gg * jax.nn.sigmoid(gg)).astype(BF16)
    ga = _dot(hn, wa1_ref[...])
    xg = _dot(ga.astype(BF16), wa2_ref[...]) + ba_ref[...]
    lg_ref[...] = (jnp.minimum(xg, 0.0) - jnp.log(1.0 + jnp.exp(-jnp.abs(xg)))) * (1.0 / GLA_GATE_NORM)


def _write_cache_rows(kn, v, mk_ref, mv_ref, k_hbm, v_hbm, stage, sem, meta_sem, seq_tiles):
    i = pl.program_id(0)
    n = pl.num_programs(0)
    tr = kn.shape[0]
    seq_rows = (seq_tiles * tr + N_META) * HEADS

    def tile_copies(step):
        row0 = (step // seq_tiles) * seq_rows + (N_META + (step % seq_tiles) * tr) * HEADS
        dst = pl.ds(pl.multiple_of(row0, 8), tr * HEADS)
        return [pltpu.make_async_copy(stage.at[step % 2, 0], k_hbm.at[dst], sem.at[step % 2, 0]),
                pltpu.make_async_copy(stage.at[step % 2, 1], v_hbm.at[dst], sem.at[step % 2, 1])]

    @pl.when(i >= 2)
    def _():
        for c in tile_copies(i - 2):
            c.wait()

    slot = i % 2
    for h in range(HEADS):
        stage[slot, 0, pl.ds(h, tr, stride=HEADS), :] = kn[:, h * LANES:(h + 1) * LANES]
        stage[slot, 1, pl.ds(h, tr, stride=HEADS), :] = v[:, h * LANES:(h + 1) * LANES]
    for c in tile_copies(i):
        c.start()

    @pl.when(i % seq_tiles == 0)
    def _():
        dst = pl.ds(pl.multiple_of((i // seq_tiles) * seq_rows, 8), N_META * HEADS)
        meta = [pltpu.make_async_copy(mk_ref, k_hbm.at[dst], meta_sem.at[0]),
                pltpu.make_async_copy(mv_ref, v_hbm.at[dst], meta_sem.at[1])]
        for c in meta:
            c.start()
        for c in meta:
            c.wait()

    @pl.when(i == n - 1)
    def _():
        for c in tile_copies(i - 1) + tile_copies(i):
            c.wait()


def _inproj(x, tr, consts, seq_tiles=None, meta_rows=None):
    rows = x.shape[0]
    steps = rows // tr
    full = lambda a: pl.BlockSpec(a.shape, lambda i: (0,) * a.ndim)
    row = lambda w: pl.BlockSpec((tr, w), lambda i: (i, 0))
    outs = [(QKW, BF16), (QKW, F32), (QKW, BF16), (VW, F32), (VW, BF16), (2 * VW, BF16),
            (GW, F32), (GW, F32), (VW, BF16), (GW, F32)]
    out_specs = [row(w) for w, _ in outs]
    out_shape = [jax.ShapeDtypeStruct((rows, w), dt) for w, dt in outs]
    extra_in, extra_specs, scratch = [], [], []
    if seq_tiles is not None:
        assert steps % seq_tiles == 0 and steps >= 2
        cache_rows = (steps // seq_tiles) * (seq_tiles * tr + N_META) * HEADS
        for o in (1, 3):
            out_specs[o] = pl.BlockSpec(memory_space=pl.ANY)
            out_shape[o] = jax.ShapeDtypeStruct((cache_rows, LANES), F32)
        extra_in = list(meta_rows)
        extra_specs = [full(a) for a in meta_rows]
        scratch = [pltpu.VMEM((2, 2, tr * HEADS, LANES), F32), pltpu.SemaphoreType.DMA((2, 2)),
                   pltpu.SemaphoreType.DMA((2,))]
    return pl.pallas_call(
        functools.partial(_inproj_kernel, seq_tiles=seq_tiles),
        grid=(steps,),
        in_specs=[row(D_MODEL)] + [full(a) for a in consts] + extra_specs,
        out_specs=out_specs,
        out_shape=out_shape,
        scratch_shapes=scratch,
        compiler_params=pltpu.CompilerParams(dimension_semantics=("arbitrary",), vmem_limit_bytes=VMEM_LIMIT),
        name="inproj",
    )(x, *consts, *extra_in)


def _bucket_ranges():
    n = np.arange(MAX_DISTANCE)
    max_exact = NUM_BUCKETS // 2
    nf = np.maximum(n, 1).astype(np.float32)
    large = max_exact + (np.log(nf / np.float32(max_exact)) / np.float32(math.log(MAX_DISTANCE / max_exact))
                         * np.float32(NUM_BUCKETS - max_exact)).astype(np.int32)
    bucket = np.where(n < max_exact, n, np.minimum(large, NUM_BUCKETS - 1))
    return [(int(n[bucket == b].min()), int(n[bucket == b].max())) for b in range(NUM_BUCKETS)]


_BUCKET_RANGES = _bucket_ranges()


def _bias_table(n, rb_ref, h):
    far = rb_ref[NUM_BUCKETS - 1, h]
    t = jnp.zeros(n.shape, F32)
    for b, (lo, hi) in enumerate(_BUCKET_RANGES[:-1]):
        val = (rb_ref[b, h] - far) * LOG2E
        cond = (n == lo) if lo == hi else ((n >= lo) & (n <= hi))
        t = jnp.where(cond, val, t)
    return jnp.where(n < 0, NEG, t)


def _attn_prompt_kernel(rb_ref, lamq_ref, lamk_ref, qa_ref, qb_ref, k_ref, v_ref, km_ref, vm_ref, oa_ref, ob_ref,
                        bias_sc, mt_sc, qs_sc, m_sc, l_sc, acc_sc, *, tile, nq):
    h = pl.program_id(0)
    b = pl.program_id(1)
    ip = pl.program_id(2)
    half = nq // 2

    @pl.when((b == 0) & (ip == 0))
    def _():
        kk = lax.broadcasted_iota(jnp.int32, (tile, tile), 0)
        qq = lax.broadcasted_iota(jnp.int32, (tile, tile), 1)
        bias_sc[0] = jnp.zeros((tile, tile), F32)
        bias_sc[1] = _bias_table(qq - kk + tile, rb_ref, h)
        bias_sc[2] = _bias_table(qq - kk, rb_ref, h)
        km = lax.broadcasted_iota(jnp.int32, (N_META, tile), 0)
        qm = lax.broadcasted_iota(jnp.int32, (N_META, tile), 1)
        mt_sc[...] = _bias_table(N_META + qm - km, rb_ref, h)

    lane = lax.broadcasted_iota(jnp.int32, (1, LANES), 1)
    for t, q_ref in enumerate((qa_ref, qb_ref)):
        q = q_ref[...]
        zero = jnp.zeros_like(q)
        qs_sc[t] = jnp.concatenate([jnp.where(lane < DK, q, zero), jnp.where(lane >= DK, q, zero)], axis=0)
    m_sc[...] = jnp.full(m_sc.shape, NEG, F32)
    l_sc[...] = jnp.zeros(l_sc.shape, F32)
    acc_sc[...] = jnp.zeros(acc_sc.shape, F32)

    def plan(n):
        if n < half:
            slot = jnp.where(n > ip, 1, 0)
            key_tile = jnp.where(n <= ip, n, n - ip - 1)
            return slot, key_tile, jnp.where(n == ip, 2, jnp.where(n == ip - 1, 1, 0))
        return 1, n - ip - 1, {nq - 1: 1, nq: 2}.get(n)

    def rows(key_tile):
        return pl.ds(pl.multiple_of(key_tile * tile, tile), tile)

    def scores(slot, key_tile, kind):
        s = _dot_nt(k_ref[rows(key_tile), :], qs_sc[slot])
        if kind is None:
            return s
        bias = bias_sc[kind]
        return s + jnp.concatenate([bias, bias], axis=1)

    def update(slot, s, vt):
        m_prev = m_sc[slot]
        m_new = jnp.maximum(m_prev, jnp.max(s, axis=0, keepdims=True))
        alpha = jnp.exp2(m_prev - m_new)
        p = jnp.exp2(s - m_new)
        l_sc[slot] = alpha * l_sc[slot] + jnp.sum(p, axis=0, keepdims=True)
        acc_sc[slot] = alpha * acc_sc[slot] + _dot_tn(vt, p.astype(BF16))
        m_sc[slot] = m_new

    meta_bias = jnp.where(ip == 0, mt_sc[...], 0.0)
    update(0, _dot_nt(km_ref[...], qs_sc[0]) + jnp.concatenate([meta_bias, meta_bias], axis=1), vm_ref[...])
    update(1, _dot_nt(km_ref[...], qs_sc[1]), vm_ref[...])

    plans = [plan(n) for n in range(nq + 1)]
    s_next = scores(*plans[0])
    for n in range(nq + 1):
        s_cur = s_next
        if n < nq:
            s_next = scores(*plans[n + 1])
        update(plans[n][0], s_cur, v_ref[rows(plans[n][1]), :])

    lam = _lam_value(lamq_ref, lamk_ref)
    for t, o_ref in enumerate((oa_ref, ob_ref)):
        acc = acc_sc[t]
        inv = 1.0 / l_sc[t]
        o_t = acc[:, 0:tile] * inv[:, 0:tile] - (lam * inv[:, tile:]) * acc[:, tile:]
        o_ref[...] = o_t.T


def _attn_prompt(rel_bias, lam_q, lam_k, qn, knb, vb, km, vm, tile):
    batch, seq, _ = qn.shape
    nq = seq // tile
    half = nq // 2
    assert nq % 2 == 0
    small = lambda a: pl.BlockSpec(a.shape, lambda h, b, ip: (0,) * a.ndim)
    kv = pl.BlockSpec((None, seq, LANES), lambda h, b, ip: (b, 0, h))
    meta = pl.BlockSpec((N_META, LANES), lambda h, b, ip: (0, h))
    short = lambda h, b, ip: (b, ip, h)
    return pl.pallas_call(
        functools.partial(_attn_prompt_kernel, tile=tile, nq=nq),
        grid=(HEADS, batch, half),
        in_specs=[pl.BlockSpec(memory_space=pltpu.SMEM), small(lam_q), small(lam_k),
                  pl.BlockSpec((None, tile, LANES), short),
                  pl.BlockSpec((None, tile, LANES), lambda h, b, ip: (b, nq - 1 - ip, h)),
                  kv, kv, meta, meta],
        out_specs=[pl.BlockSpec((None, tile, LANES), short),
                   pl.BlockSpec((None, tile, LANES), lambda h, b, ip: (b, half - 1 - ip, h))],
        out_shape=[jax.ShapeDtypeStruct((batch, seq // 2, VW), F32)] * 2,
        scratch_shapes=[pltpu.VMEM((3, tile, tile), F32), pltpu.VMEM((N_META, tile), F32),
                        pltpu.VMEM((2, 2 * tile, LANES), BF16),
                        pltpu.VMEM((2, 1, 2 * tile), F32), pltpu.VMEM((2, 1, 2 * tile), F32),
                        pltpu.VMEM((2, LANES, 2 * tile), F32)],
        compiler_params=pltpu.CompilerParams(dimension_semantics=("arbitrary",) * 3, vmem_limit_bytes=VMEM_LIMIT),
        name="attn_prompt",
    )(rel_bias, lam_q, lam_k, qn, qn, knb, vb, km, vm)


def _attn_sample_kernel(pt_ref, rb_ref, lamq_ref, lamk_ref, q_ref, kn_ref, vn_ref, ck_hbm, cv_hbm, o_ref,
                        kbuf, vbuf, sem, w_sc, mask_sc, ptab_sc, ntab_sc, m_sc, l_sc, acc_sc, *, pages, tq):
    b = pl.program_id(0)
    g = pl.program_id(1)
    ng = pl.num_programs(1)
    step = b * ng + g
    total = pl.num_programs(0) * ng
    nrow = 2 * HEADS * tq
    pcols = kbuf.shape[2]
    page_size = pcols // HEADS

    def page_copies(st):
        slot = st % RING_SLOTS
        out = []
        for u in range(pages):
            page = pt_ref[st // ng, (st % ng) * pages + u]
            out.append(pltpu.make_async_copy(ck_hbm.at[page], kbuf.at[slot, u], sem.at[slot, 0, u]))
            out.append(pltpu.make_async_copy(cv_hbm.at[page], vbuf.at[slot, u], sem.at[slot, 1, u]))
        return out

    @pl.when(step == 0)
    def _():
        for ahead in range(RING_SLOTS - 1):
            for c in page_copies(ahead):
                c.start()

    @pl.when(step + RING_SLOTS - 1 < total)
    def _():
        for c in page_copies(step + RING_SLOTS - 1):
            c.start()

    @pl.when((b == 0) & (g == 0))
    def _():
        def tables(cols, offset):
            r = lax.broadcasted_iota(jnp.int32, (nrow, cols), 0)
            c = lax.broadcasted_iota(jnp.int32, (nrow, cols), 1)
            tok = r // (2 * HEADS)
            head = (r % (2 * HEADS)) // 2
            n = offset + tok - c // HEADS
            t = jnp.zeros((nrow, cols), F32)
            for h in range(HEADS):
                t = jnp.where(head == h, _bias_table(n, rb_ref, h), t)
            return jnp.where(head == c % HEADS, t, NEG), c // HEADS

        mask_sc[...] = jnp.where(tables(pcols, MAX_DISTANCE)[0] > 0.5 * NEG, 0.0, NEG)
        ptab_sc[...] = tables(pcols, page_size)[0]
        nt, slot = tables(LANES, 0)
        ntab_sc[...] = jnp.where(slot < tq, nt, NEG)

    def update(s, pv):
        m_prev = m_sc[...]
        m_new = jnp.maximum(m_prev, jnp.max(s, axis=-1, keepdims=True))
        alpha = jnp.exp2(m_prev - m_new)
        p = jnp.exp2(s - m_new)
        l_sc[...] = alpha * l_sc[...] + jnp.sum(p, axis=-1, keepdims=True)
        acc_sc[...] = alpha * acc_sc[...] + pv(p.astype(BF16))
        m_sc[...] = m_new

    @pl.when(g == 0)
    def _():
        q = q_ref[...].astype(F32)
        r8 = lax.broadcasted_iota(jnp.int32, (2 * HEADS, LANES), 0)
        lane = lax.broadcasted_iota(jnp.int32, (2 * HEADS, LANES), 1)
        blocks = []
        for t in range(tq):
            blk = jnp.zeros((2 * HEADS, LANES), F32)
            for h in range(HEADS):
                blk = jnp.where(r8 // 2 == h, jnp.broadcast_to(q[t:t + 1, h * LANES:(h + 1) * LANES], blk.shape), blk)
            blocks.append(jnp.where(lane // DK == r8 % 2, blk, 0.0))
        w_sc[...] = jnp.concatenate(blocks, axis=0).astype(BF16)
        m_sc[...] = jnp.full(m_sc.shape, NEG, F32)
        l_sc[...] = jnp.zeros(l_sc.shape, F32)
        acc_sc[...] = jnp.zeros(acc_sc.shape, F32)
        pad = jnp.zeros((LANES - kn_ref.shape[0], LANES), F32)
        kn = jnp.concatenate([kn_ref[...], pad], axis=0).astype(BF16)
        vn = jnp.concatenate([vn_ref[...], pad], axis=0).astype(BF16)
        s = _dot_nt(w_sc[...], kn) + ntab_sc[...]
        update(s, lambda p: _dot(p, vn))

    for c in page_copies(step):
        c.wait()
    slot = step % RING_SLOTS
    w = w_sc[...]
    last = g == ng - 1
    parts = []
    for u in range(pages):
        s = _dot_nt(w, kbuf[slot, u].astype(BF16))
        if u == pages - 1:
            s = s + jnp.where(last, ptab_sc[...], mask_sc[...])
        else:
            s = s + mask_sc[...]
        parts.append(s)
    s = jnp.concatenate(parts, axis=-1)

    def pv(p):
        acc = _dot(p[:, 0:pcols], vbuf[slot, 0].astype(BF16))
        for u in range(1, pages):
            acc += _dot(p[:, u * pcols:(u + 1) * pcols], vbuf[slot, u].astype(BF16))
        return acc

    update(s, pv)

    @pl.when(last)
    def _():
        lam = _lam_value(lamq_ref, lamk_ref)
        accn = acc_sc[...] * (1.0 / l_sc[...])
        o_ref[...] = jnp.zeros(o_ref.shape, F32)
        for t in range(tq):
            for h in range(HEADS):
                r = t * 2 * HEADS + 2 * h
                o_ref[t:t + 1, h * DV:(h + 1) * DV] = accn[r:r + 1, :] - lam * accn[r + 1:r + 2, :]


def _attn_sample(page_table, rel_bias, lam_q, lam_k, qn, kn, v, cache_k, cache_v, tq, pages):
    nb, n_pages = page_table.shape
    rpb = qn.shape[0] // nb
    pcols = cache_k.shape[1]
    ng = n_pages // pages
    nrow = 2 * HEADS * tq
    assert nb * ng >= RING_SLOTS - 1
    small = lambda a: pl.BlockSpec(a.shape, lambda b, g, pt: (0,) * a.ndim)
    grid_spec = pltpu.PrefetchScalarGridSpec(
        num_scalar_prefetch=1,
        grid=(nb, ng),
        in_specs=[pl.BlockSpec(memory_space=pltpu.SMEM), small(lam_q), small(lam_k),
                  pl.BlockSpec((rpb, QKW), lambda b, g, pt: (b, 0)),
                  pl.BlockSpec((rpb * HEADS, LANES), lambda b, g, pt: (b, 0)),
                  pl.BlockSpec((rpb * HEADS, LANES), lambda b, g, pt: (b, 0)),
                  pl.BlockSpec(memory_space=pl.ANY), pl.BlockSpec(memory_space=pl.ANY)],
        out_specs=pl.BlockSpec((rpb, VW), lambda b, g, pt: (b, 0)),
        scratch_shapes=[pltpu.VMEM((RING_SLOTS, pages, pcols, LANES), F32),
                        pltpu.VMEM((RING_SLOTS, pages, pcols, LANES), F32),
                        pltpu.SemaphoreType.DMA((RING_SLOTS, 2, pages)),
                        pltpu.VMEM((nrow, LANES), BF16), pltpu.VMEM((nrow, pcols), F32), pltpu.VMEM((nrow, pcols), F32),
                        pltpu.VMEM((nrow, LANES), F32), pltpu.VMEM((nrow, 1), F32), pltpu.VMEM((nrow, 1), F32),
                        pltpu.VMEM((nrow, LANES), F32)],
    )
    return pl.pallas_call(
        functools.partial(_attn_sample_kernel, pages=pages, tq=tq),
        grid_spec=grid_spec,
        out_shape=jax.ShapeDtypeStruct((nb * rpb, VW), F32),
        compiler_params=pltpu.CompilerParams(dimension_semantics=("arbitrary", "arbitrary"),
                                             vmem_limit_bytes=VMEM_LIMIT),
        name="attn_sample",
    )(page_table, rel_bias, lam_q, lam_k, qn, kn, v, cache_k, cache_v)


def _gla_kernel(s0_ref, q_ref, k_ref, v_ref, lg_ref, tri_ref, o_ref, sout_ref, s_sc, *, valid):
    c = pl.program_id(1)
    nseq = q_ref.shape[0]

    @pl.when(c == 0)
    def _():
        for i in range(nseq):
            s_sc[i] = s0_ref[0 if s0_ref.shape[0] != nseq else i]

    rows_in = q_ref.shape[1]

    def padded(a):
        if rows_in == CHUNK:
            return a
        return jnp.concatenate([a, jnp.zeros((CHUNK - rows_in, a.shape[1]), a.dtype)], axis=0)

    wide = lambda ref: jnp.concatenate([padded(ref[i]) for i in range(nseq)], axis=1)
    q = wide(q_ref)
    k = wide(k_ref)
    lg = wide(lg_ref)
    vb = wide(v_ref)
    if valid < CHUNK:
        live = lax.broadcasted_iota(jnp.int32, (CHUNK, 1), 0) < valid
        k = jnp.where(live, k, 0.0)
        lg = jnp.where(live, lg, 0.0)
        vb = jnp.where(live, vb, jnp.zeros_like(vb))

    lg_hi = lg.astype(BF16)
    lg_lo = (lg - lg_hi.astype(F32)).astype(BF16)
    tri = tri_ref[...]
    b = _dot(tri, lg_hi) + _dot(tri, lg_lo)
    b_mid = b[CHUNK // 2 - 1:CHUNK // 2, :]
    qs = q * jnp.exp(b)
    qt = q * jnp.exp(b - b_mid)
    kt = (k * jnp.exp(b_mid - b)).astype(BF16)
    b_t = b.T
    b_last = b_t[:, CHUNK - 1:CHUNK]
    kl_t = (k.T * jnp.exp(b_last - b_t)).astype(BF16)
    decay = jnp.exp(b_last)

    row = lax.broadcasted_iota(jnp.int32, (CHUNK, CHUNK), 0)
    col = lax.broadcasted_iota(jnp.int32, (CHUNK, CHUNK), 1)
    causal = row >= col
    lane = lax.broadcasted_iota(jnp.int32, (1, LANES), 1)
    heads = [(i, h) for i in range(nseq) for h in range(HEADS)]
    cols = lambda i, h: slice(i * GW + (h // 2) * LANES, i * GW + (h // 2 + 1) * LANES)
    mine = lambda h: (lane // DK) == (h % 2)
    v_of = lambda i, h: vb[:, i * VW + h * DV:i * VW + (h + 1) * DV]
    s_old = [s_sc[i] for i in range(nseq)]
    s_bf = [s.astype(BF16) for s in s_old]
    a = [_dot_nt(jnp.where(mine(h), qt[:, cols(i, h)], 0.0).astype(BF16), kt[:, cols(i, h)]) for i, h in heads]
    inter = [_dot(jnp.where(mine(h), qs[:, cols(i, h)], 0.0).astype(BF16),
                  s_bf[i][(h // 2) * LANES:(h // 2 + 1) * LANES, :]) for i, h in heads]
    upd = [_dot(kl_t[i * GW + h * DK:i * GW + (h + 1) * DK, :], v_of(i, h)) for i, h in heads]
    for n, (i, h) in enumerate(heads):
        o_h = inter[n] + _dot(jnp.where(causal, a[n], 0.0).astype(BF16), v_of(i, h))
        o_ref[i, :, h * DV:(h + 1) * DV] = o_h[0:rows_in]
        rows = slice(i * GW + h * DK, i * GW + (h + 1) * DK)
        s_sc[i, h * DK:(h + 1) * DK, :] = decay[rows, :] * s_old[i][h * DK:(h + 1) * DK, :] + upd[n]

    @pl.when(c == pl.num_programs(1) - 1)
    def _():
        sout_ref[...] = s_sc[...]


def _gla(s0, gq, gk, gv, lg, tri, nb, rows_in, valid):
    nc = gq.shape[0] // (nb * rows_in)
    per = min(nb, GLA_SEQS_PER_STEP)
    assert nb % per == 0
    s0_blk = per if s0.shape[0] == nb else 1
    s0_map = (lambda b, c: (b, 0, 0)) if s0.shape[0] == nb else (lambda b, c: (0, 0, 0))
    seqs = lambda a: a.reshape(nb, nc * rows_in, a.shape[-1])
    rowblk = lambda w: pl.BlockSpec((per, rows_in, w), lambda b, c: (b, c, 0))
    o, s_fin = pl.pallas_call(
        functools.partial(_gla_kernel, valid=valid),
        grid=(nb // per, nc),
        in_specs=[pl.BlockSpec((s0_blk, GW, DV), s0_map), rowblk(GW), rowblk(GW), rowblk(VW), rowblk(GW),
                  pl.BlockSpec(tri.shape, lambda b, c: (0, 0))],
        out_specs=[rowblk(VW), pl.BlockSpec((per, GW, DV), lambda b, c: (b, 0, 0))],
        out_shape=[jax.ShapeDtypeStruct((nb, nc * rows_in, VW), F32), jax.ShapeDtypeStruct((nb, GW, DV), F32)],
        scratch_shapes=[pltpu.VMEM((per, GW, DV), F32)],
        compiler_params=pltpu.CompilerParams(dimension_semantics=("arbitrary", "arbitrary"),
                                             vmem_limit_bytes=VMEM_LIMIT),
        name="gla",
    )(s0, seqs(gq), seqs(gk), seqs(gv), seqs(lg), tri)
    return o.reshape(nb * nc * rows_in, VW), s_fin


def _merge_kernel(x_ref, *rest, seq_tiles):
    if seq_tiles is None:
        od_ref, og_ref, gate_ref, dnw_ref, gnw_ref, wtop_ref, wbot_ref, y_ref = rest
        od = od_ref[...]
    else:
        oda_ref, odb_ref, og_ref, gate_ref, dnw_ref, gnw_ref, wtop_ref, wbot_ref, y_ref = rest
        od = jnp.where(pl.program_id(0) % seq_tiles < seq_tiles // 2, oda_ref[...], odb_ref[...])

    def head_norm(o, w):
        parts = []
        for h in range(HEADS):
            sl = o[:, h * DV:(h + 1) * DV]
            parts.append(sl * lax.rsqrt(jnp.mean(sl * sl, axis=-1, keepdims=True) + RMS_EPS))
        return jnp.concatenate(parts, axis=-1) * w

    gate = gate_ref[...].astype(F32)
    mix_d = (head_norm(od, dnw_ref[...]) * gate[:, 0:VW]).astype(BF16)
    mix_g = (head_norm(og_ref[...], gnw_ref[...]) * gate[:, VW:]).astype(BF16)
    y_ref[...] = x_ref[...] + _dot(mix_d, wtop_ref[...]) + _dot(mix_g, wbot_ref[...])


def _merge(x, od_parts, og, gates, dnw, gnw, wtop, wbot, tr, seq_tiles=None):
    rows = x.shape[0]
    full = lambda a: pl.BlockSpec(a.shape, lambda i: (0,) * a.ndim)
    row = lambda w: pl.BlockSpec((tr, w), lambda i: (i, 0))
    if seq_tiles is None:
        od_specs = [row(VW)]
    else:
        half = seq_tiles // 2
        od_specs = [pl.BlockSpec((tr, VW), lambda i: ((i // seq_tiles) * half + jnp.minimum(i % seq_tiles, half - 1), 0)),
                    pl.BlockSpec((tr, VW), lambda i: ((i // seq_tiles) * half + jnp.maximum(i % seq_tiles - half, 0), 0))]
    return pl.pallas_call(
        functools.partial(_merge_kernel, seq_tiles=seq_tiles),
        grid=(rows // tr,),
        in_specs=[row(D_MODEL)] + od_specs + [row(VW), row(2 * VW), full(dnw), full(gnw), full(wtop), full(wbot)],
        out_specs=row(D_MODEL),
        out_shape=jax.ShapeDtypeStruct((rows, D_MODEL), F32),
        compiler_params=pltpu.CompilerParams(dimension_semantics=("arbitrary",), vmem_limit_bytes=VMEM_LIMIT),
        name="merge",
    )(x, *od_parts, og, gates, dnw, gnw, wtop, wbot)


def kernel(x_prompt, x_sample, cache_k, cache_v, state_gla, page_table, meta_tokens, rel_bias, norm_w, w_in,
           q_norm_w, k_norm_w, lam_q, lam_k, diff_norm_w, gla_wa2, gla_ba, gla_norm_w, w_out):
    batch, seq, _ = x_prompt.shape
    nb, tq, _ = x_sample.shape
    tile = 512
    rpb = 16

    w = w_in[0]
    wm = w[:, :Z_MAIN].astype(BF16)
    wa1 = jnp.pad(w[:, Z_MAIN:], ((0, 0), (0, LANES - GLA_GATE_RANK))).astype(BF16)
    wa2 = jnp.pad(gla_wa2[0], ((0, LANES - GLA_GATE_RANK), (0, 0))).astype(BF16)
    ba = gla_ba[0][None]
    nw = norm_w[0][None]
    qw = jnp.tile(q_norm_w[0].reshape(-1), HEADS)[None] * (DIFF_SCALE * LOG2E)
    kw = jnp.tile(k_norm_w[0].reshape(-1), HEADS)[None]
    grp = np.arange(256) // DK
    gmat = jnp.asarray(grp[:, None] == grp[None, :], BF16)
    consts = (nw, wm, wa1, wa2, ba, qw, kw, gmat)
    dnw = jnp.tile(diff_norm_w[0], HEADS)[None] * (1.0 - LAM_INIT)
    gnw = jnp.tile(gla_norm_w[0], HEADS)[None]
    wtop = w_out[0][:VW].astype(BF16)
    wbot = w_out[0][VW:].astype(BF16)
    t_idx = np.arange(CHUNK)
    tri = jnp.asarray(t_idx[:, None] >= t_idx[None, :], BF16)
    lq, lk = lam_q[0], lam_k[0]

    xs = jnp.pad(x_sample, ((0, 0), (0, rpb - tq), (0, 0))).reshape(nb * rpb, D_MODEL)
    xm = jnp.pad(meta_tokens, ((0, CHUNK - N_META), (0, 0)))
    x_small = jnp.concatenate([xs, xm], axis=0)
    ns = nb * rpb
    small = _inproj(x_small, x_small.shape[0], consts)
    s_qn, s_kn, s_knb, s_v, s_vb, s_gate, s_gq, s_gk, s_gv, s_lg = [a[:ns] for a in small]
    m_qn, m_kn, m_knb, m_v, m_vb, m_gate, m_gq, m_gk, m_gv, m_lg = [a[ns:] for a in small]

    xp = x_prompt.reshape(batch * seq, D_MODEL)
    meta_rows = (m_kn[:N_META].reshape(N_META * HEADS, 2 * DK), m_v[:N_META].reshape(N_META * HEADS, DV))
    p_qn, k_rows, p_knb, v_rows, p_vb, p_gate, p_gq, p_gk, p_gv, p_lg = _inproj(
        xp, tile, consts, seq_tiles=seq // tile, meta_rows=meta_rows)

    o_d = _attn_prompt(rel_bias, lq, lk, p_qn.reshape(batch, seq, QKW), p_knb.reshape(batch, seq, QKW),
                       p_vb.reshape(batch, seq, VW), m_knb, m_vb, tile)

    zero_state = jnp.zeros((1, GW, DV), F32)
    _, s_meta = _gla(zero_state, m_gq, m_gk, m_gv, m_lg, tri, 1, CHUNK, N_META)
    o_g, s_fin = _gla(s_meta, p_gq, p_gk, p_gv, p_lg, tri, batch, CHUNK, CHUNK)
    os_g, s_new = _gla(state_gla[0].reshape(nb, GW, DV), s_gq, s_gk, s_gv, s_lg, tri, nb, rpb, tq)

    n_pool, page_size = cache_k.shape[1], cache_k.shape[2]
    ck = cache_k.reshape(n_pool, page_size * HEADS, 2 * DK)
    cv = cache_v.reshape(n_pool, page_size * HEADS, DV)
    os_d = _attn_sample(page_table, rel_bias, lq, lk, s_qn, s_kn.reshape(ns * HEADS, 2 * DK),
                        s_v.reshape(ns * HEADS, DV), ck, cv, tq, 16)

    half_rows = batch * seq // 2
    y_prompt = _merge(xp, [o.reshape(half_rows, VW) for o in o_d], o_g, p_gate, dnw, gnw, wtop, wbot, tile,
                      seq_tiles=seq // tile)
    y_small = _merge(xs, [os_d], os_g, s_gate, dnw, gnw, wtop, wbot, ns)

    y_prompt = y_prompt.reshape(batch, seq, D_MODEL)
    y_sample = y_small.reshape(nb, rpb, D_MODEL)[:, :tq]
    k_prompt = k_rows.reshape(1, batch, seq + N_META, HEADS, 2 * DK)
    v_prompt = v_rows.reshape(1, batch, seq + N_META, HEADS, DV)
    s_prompt = s_fin.reshape(1, batch, HEADS, DK, DV)
    k_sample = s_kn.reshape(nb, rpb, HEADS, 2 * DK)[None, :, :tq]
    v_sample = s_v.reshape(nb, rpb, HEADS, DV)[None, :, :tq]
    s_sample = s_new.reshape(1, nb, HEADS, DK, DV)
    return (y_prompt, y_sample, k_prompt, v_prompt, s_prompt, k_sample, v_sample, s_sample)
```

```python
import functools
import math

import numpy as np
import jax
import jax.numpy as jnp
from jax import lax
from jax.experimental import pallas as pl
from jax.experimental.pallas import tpu as pltpu

D_MODEL = 1024
N_META = 16
HEADS = 4
DK = 64
DV = 128
DIFF_SCALE = DK ** -0.5
GLA_GATE_RANK = 16
GLA_GATE_NORM = 16.0
NUM_BUCKETS = 32
MAX_DISTANCE = 128
RMS_EPS = 1e-6
LAM_INIT = 0.8 - 0.6 * math.exp(-0.3 * 0)
QKW = HEADS * 2 * DK
VW = HEADS * DV
GW = HEADS * DK
Z_MAIN = 3 * QKW + VW + 2 * GW + 2 * VW
LANES = 128
CHUNK = 128
NEG = -1e30
LOG2E = math.log2(math.e)
RING_SLOTS = 3
ATTN_PAGES = 8
GLA_SEQS_PER_STEP = 4
VMEM_LIMIT = 56 * 1024 * 1024

F32 = jnp.float32
BF16 = jnp.bfloat16


def _dot(a, b):
    return jnp.dot(a, b, preferred_element_type=F32)


def _dot_nt(a, b):
    return lax.dot_general(a, b, (((1,), (1,)), ((), ())), preferred_element_type=F32)


def _dot_tn(a, b):
    return lax.dot_general(a, b, (((0,), (0,)), ((), ())), preferred_element_type=F32)


def _lam_value(lamq_ref, lamk_ref):
    e = jnp.exp(jnp.sum(lamq_ref[...] * lamk_ref[...], axis=-1, keepdims=True))
    return e[0:1, :] - e[1:2, :] + LAM_INIT


def _inproj_kernel(x_ref, nw_ref, wm_ref, wa1_ref, wa2_ref, ba_ref, qw_ref, kw_ref, g_ref, *rest, seq_tiles):
    if seq_tiles is None:
        qn_ref, kn_ref, knb_ref, v_ref, vb_ref, gate_ref, gq_ref, gk_ref, gv_ref, lg_ref = rest
    else:
        (mk_ref, mv_ref, qn_ref, kn_hbm, knb_ref, v_hbm, vb_ref, gate_ref, gq_ref, gk_ref, gv_ref, lg_ref,
         stage, sem, meta_sem) = rest
    x = x_ref[...]
    ms = jnp.mean(x * x, axis=-1, keepdims=True)
    hn = (x * lax.rsqrt(ms + RMS_EPS) * nw_ref[...]).astype(BF16)

    def proj(lo, hi):
        return _dot(hn, wm_ref[:, lo:hi])

    def group_norm(z, w):
        sq = (z * z).astype(BF16)
        ss = jnp.concatenate([_dot(sq[:, c:c + 256], g_ref[...]) for c in range(0, QKW, 256)], axis=-1)
        return z * lax.rsqrt(ss * (1.0 / DK) + RMS_EPS) * w

    qn_ref[...] = group_norm(proj(0, QKW), qw_ref[...]).astype(BF16)
    kn = group_norm(proj(QKW, 2 * QKW), kw_ref[...])
    knb_ref[...] = kn.astype(BF16)
    v = proj(2 * QKW, 2 * QKW + VW)
    vb_ref[...] = v.astype(BF16)
    if seq_tiles is None:
        kn_ref[...] = kn
        v_ref[...] = v
    else:
        _write_cache_rows(kn, v, mk_ref, mv_ref, kn_hbm, v_hbm, stage, sem, meta_sem, seq_tiles)
    o = 2 * QKW + VW
    dg = proj(o, o + VW)
    gate_ref[:, 0:VW] = (dg * jax.nn.sigmoid(dg)).astype(BF16)
    o += VW
    gq_ref[...] = proj(o, o + GW) * (DK ** -0.5)
    gk_ref[...] = proj(o + GW, o + 2 * GW)
    o += 2 * GW
    gv_ref[...] = proj(o, o + VW).astype(BF16)
    o += VW
    gg = proj(o, o + VW)
    gate_ref[:, VW:2 * VW] = (gg * jax.nn.sigmoid(gg)).astype(BF16)
    ga = _dot(hn, wa1_ref[...])
    xg = _dot(ga.astype(BF16), wa2_ref[...]) + ba_ref[...]
    lg_ref[...] = (jnp.minimum(xg, 0.0) - jnp.log(1.0 + jnp.exp(-jnp.abs(xg)))) * (1.0 / GLA_GATE_NORM)


def _write_cache_rows(kn, v, mk_ref, mv_ref, k_hbm, v_hbm, stage, sem, meta_sem, seq_tiles):
    i = pl.program_id(0)
    n = pl.num_programs(0)
    tr = kn.shape[0]
    seq_rows = (seq_tiles * tr + N_META) * HEADS

    def tile_copies(step):
        row0 = (step // seq_tiles) * seq_rows + (N_META + (step % seq_tiles) * tr) * HEADS
        dst = pl.ds(pl.multiple_of(row0, 8), tr * HEADS)
        return [pltpu.make_async_copy(stage.at[step % 2, 0], k_hbm.at[dst], sem.at[step % 2, 0]),
                pltpu.make_async_copy(stage.at[step % 2, 1], v_hbm.at[dst], sem.at[step % 2, 1])]

    @pl.when(i >= 2)
    def _():
        for c in tile_copies(i - 2):
            c.wait()

    slot = i % 2
    for h in range(HEADS):
        stage[slot, 0, pl.ds(h, tr, stride=HEADS), :] = kn[:, h * LANES:(h + 1) * LANES]
        stage[slot, 1, pl.ds(h, tr, stride=HEADS), :] = v[:, h * LANES:(h + 1) * LANES]
    for c in tile_copies(i):
        c.start()

    @pl.when(i % seq_tiles == 0)
    def _():
        dst = pl.ds(pl.multiple_of((i // seq_tiles) * seq_rows, 8), N_META * HEADS)
        meta = [pltpu.make_async_copy(mk_ref, k_hbm.at[dst], meta_sem.at[0]),
                pltpu.make_async_copy(mv_ref, v_hbm.at[dst], meta_sem.at[1])]
        for c in meta:
            c.start()
        for c in meta:
            c.wait()

    @pl.when(i == n - 1)
    def _():
        for c in tile_copies(i - 1) + tile_copies(i):
            c.wait()


def _inproj(x, tr, consts, seq_tiles=None, meta_rows=None):
    rows = x.shape[0]
    steps = rows // tr
    full = lambda a: pl.BlockSpec(a.shape, lambda i: (0,) * a.ndim)
    row = lambda w: pl.BlockSpec((tr, w), lambda i: (i, 0))
    outs = [(QKW, BF16), (QKW, F32), (QKW, BF16), (VW, F32), (VW, BF16), (2 * VW, BF16),
            (GW, F32), (GW, F32), (VW, BF16), (GW, F32)]
    out_specs = [row(w) for w, _ in outs]
    out_shape = [jax.ShapeDtypeStruct((rows, w), dt) for w, dt in outs]
    extra_in, extra_specs, scratch = [], [], []
    if seq_tiles is not None:
        assert steps % seq_tiles == 0 and steps >= 2
        cache_rows = (steps // seq_tiles) * (seq_tiles * tr + N_META) * HEADS
        for o in (1, 3):
            out_specs[o] = pl.BlockSpec(memory_space=pl.ANY)
            out_shape[o] = jax.ShapeDtypeStruct((cache_rows, LANES), F32)
        extra_in = list(meta_rows)
        extra_specs = [full(a) for a in meta_rows]
        scratch = [pltpu.VMEM((2, 2, tr * HEADS, LANES), F32), pltpu.SemaphoreType.DMA((2, 2)),
                   pltpu.SemaphoreType.DMA((2,))]
    return pl.pallas_call(
        functools.partial(_inproj_kernel, seq_tiles=seq_tiles),
        grid=(steps,),
        in_specs=[row(D_MODEL)] + [full(a) for a in consts] + extra_specs,
        out_specs=out_specs,
        out_shape=out_shape,
        scratch_shapes=scratch,
        compiler_params=pltpu.CompilerParams(dimension_semantics=("arbitrary",), vmem_limit_bytes=VMEM_LIMIT),
        name="inproj",
    )(x, *consts, *extra_in)


def _bucket_ranges():
    n = np.arange(MAX_DISTANCE)
    max_exact = NUM_BUCKETS // 2
    nf = np.maximum(n, 1).astype(np.float32)
    large = max_exact + (np.log(nf / np.float32(max_exact)) / np.float32(math.log(MAX_DISTANCE / max_exact))
                         * np.float32(NUM_BUCKETS - max_exact)).astype(np.int32)
    bucket = np.where(n < max_exact, n, np.minimum(large, NUM_BUCKETS - 1))
    return [(int(n[bucket == b].min()), int(n[bucket == b].max())) for b in range(NUM_BUCKETS)]


_BUCKET_RANGES = _bucket_ranges()


def _bias_table(n, rb_ref, h):
    far = rb_ref[NUM_BUCKETS - 1, h]
    t = jnp.zeros(n.shape, F32)
    for b, (lo, hi) in enumerate(_BUCKET_RANGES[:-1]):
        val = (rb_ref[b, h] - far) * LOG2E
        cond = (n == lo) if lo == hi else ((n >= lo) & (n <= hi))
        t = jnp.where(cond, val, t)
    return jnp.where(n < 0, NEG, t)


def _attn_kernel(pt_ref, rb_ref, lamq_ref, lamk_ref, qa_ref, qb_ref, k_ref, v_ref, km_ref, vm_ref,
                 sq_ref, skn_ref, svn_ref, ck_hbm, cv_hbm, oa_ref, ob_ref, so_ref,
                 bias_sc, mt_sc, qs_sc, m_sc, l_sc, acc_sc, *sample_scratch, tile, nq, pages, tq, groups):
    h = pl.program_id(0)
    b = pl.program_id(1)
    ip = pl.program_id(2)
    half = nq // 2
    st = (h * pl.num_programs(1) + b) * half + ip
    total_steps = pl.num_programs(0) * pl.num_programs(1) * half
    sample_group = _sample_stream(st, total_steps, pt_ref, rb_ref, lamq_ref, lamk_ref, sq_ref, skn_ref, svn_ref,
                                  ck_hbm, cv_hbm, so_ref, *sample_scratch, pages=pages, tq=tq, groups=groups)

    @pl.when((b == 0) & (ip == 0))
    def _():
        kk = lax.broadcasted_iota(jnp.int32, (tile, tile), 0)
        qq = lax.broadcasted_iota(jnp.int32, (tile, tile), 1)
        bias_sc[0] = jnp.zeros((tile, tile), F32)
        bias_sc[1] = _bias_table(qq - kk + tile, rb_ref, h)
        bias_sc[2] = _bias_table(qq - kk, rb_ref, h)
        km = lax.broadcasted_iota(jnp.int32, (N_META, tile), 0)
        qm = lax.broadcasted_iota(jnp.int32, (N_META, tile), 1)
        mt_sc[...] = _bias_table(N_META + qm - km, rb_ref, h)

    lane = lax.broadcasted_iota(jnp.int32, (1, LANES), 1)
    for t, q_ref in enumerate((qa_ref, qb_ref)):
        q = q_ref[...]
        zero = jnp.zeros_like(q)
        qs_sc[t] = jnp.concatenate([jnp.where(lane < DK, q, zero), jnp.where(lane >= DK, q, zero)], axis=0)
    m_sc[...] = jnp.full(m_sc.shape, NEG, F32)
    l_sc[...] = jnp.zeros(l_sc.shape, F32)
    acc_sc[...] = jnp.zeros(acc_sc.shape, F32)

    def plan(n):
        if n < half:
            slot = jnp.where(n > ip, 1, 0)
            key_tile = jnp.where(n <= ip, n, n - ip - 1)
            return slot, key_tile, jnp.where(n == ip, 2, jnp.where(n == ip - 1, 1, 0))
        return 1, n - ip - 1, {nq - 1: 1, nq: 2}.get(n)

    def rows(key_tile):
        return pl.ds(pl.multiple_of(key_tile * tile, tile), tile)

    def scores(slot, key_tile, kind):
        s = _dot_nt(k_ref[rows(key_tile), :], qs_sc[slot])
        if kind is None:
            return s
        bias = bias_sc[kind]
        return s + jnp.concatenate([bias, bias], axis=1)

    def update(slot, s, vt):
        m_prev = m_sc[slot]
        m_new = jnp.maximum(m_prev, jnp.max(s, axis=0, keepdims=True))
        alpha = jnp.exp2(m_prev - m_new)
        p = jnp.exp2(s - m_new)
        l_sc[slot] = alpha * l_sc[slot] + jnp.sum(p, axis=0, keepdims=True)
        acc_sc[slot] = alpha * acc_sc[slot] + _dot_tn(vt, p.astype(BF16))
        m_sc[slot] = m_new

    meta_bias = jnp.where(ip == 0, mt_sc[...], 0.0)
    update(0, _dot_nt(km_ref[...], qs_sc[0]) + jnp.concatenate([meta_bias, meta_bias], axis=1), vm_ref[...])
    update(1, _dot_nt(km_ref[...], qs_sc[1]), vm_ref[...])

    plans = [plan(n) for n in range(nq + 1)]
    s_next = scores(*plans[0])
    for n in range(nq + 1):
        s_cur = s_next
        if n < nq:
            s_next = scores(*plans[n + 1])
        update(plans[n][0], s_cur, v_ref[rows(plans[n][1]), :])
        if n < groups:
            sample_group(n)

    lam = _lam_value(lamq_ref, lamk_ref)
    for t, o_ref in enumerate((oa_ref, ob_ref)):
        acc = acc_sc[t]
        inv = 1.0 / l_sc[t]
        o_t = acc[:, 0:tile] * inv[:, 0:tile] - (lam * inv[:, tile:]) * acc[:, tile:]
        o_ref[...] = o_t.T


def _attn(page_table, rel_bias, lam_q, lam_k, qn, knb, vb, km, vm, s_qn, s_kn, s_v, cache_k, cache_v, tile, tq, pages):
    batch, seq, _ = qn.shape
    nq = seq // tile
    half = nq // 2
    assert nq % 2 == 0
    steps = HEADS * batch * half
    nb, n_pages = page_table.shape
    rpb = s_qn.shape[0] // nb
    pcols = cache_k.shape[1]
    nrow = 2 * HEADS * tq
    total_groups = nb * (n_pages // pages)
    groups = total_groups // steps
    assert groups * steps == total_groups and (n_pages // pages) % groups == 0
    assert groups <= nq + 1 and total_groups >= RING_SLOTS - 1
    steps_per_elem = (n_pages // pages) // groups
    small = lambda a: pl.BlockSpec(a.shape, lambda h, b, ip, pt: (0,) * a.ndim)
    kv = pl.BlockSpec((None, seq, LANES), lambda h, b, ip, pt: (b, 0, h))
    meta = pl.BlockSpec((N_META, LANES), lambda h, b, ip, pt: (0, h))
    short = lambda h, b, ip, pt: (b, ip, h)
    elem = lambda h, b, ip, pt: (((h * batch + b) * half + ip) // steps_per_elem, 0)
    grid_spec = pltpu.PrefetchScalarGridSpec(
        num_scalar_prefetch=1,
        grid=(HEADS, batch, half),
        in_specs=[pl.BlockSpec(memory_space=pltpu.SMEM), small(lam_q), small(lam_k),
                  pl.BlockSpec((None, tile, LANES), short),
                  pl.BlockSpec((None, tile, LANES), lambda h, b, ip, pt: (b, nq - 1 - ip, h)),
                  kv, kv, meta, meta,
                  pl.BlockSpec((rpb, QKW), elem), pl.BlockSpec((rpb * HEADS, LANES), elem),
                  pl.BlockSpec((rpb * HEADS, LANES), elem),
                  pl.BlockSpec(memory_space=pl.ANY), pl.BlockSpec(memory_space=pl.ANY)],
        out_specs=[pl.BlockSpec((None, tile, LANES), short),
                   pl.BlockSpec((None, tile, LANES), lambda h, b, ip, pt: (b, half - 1 - ip, h)),
                   pl.BlockSpec((rpb, VW), elem)],
        scratch_shapes=[pltpu.VMEM((3, tile, tile), F32), pltpu.VMEM((N_META, tile), F32),
                        pltpu.VMEM((2, 2 * tile, LANES), BF16),
                        pltpu.VMEM((2, 1, 2 * tile), F32), pltpu.VMEM((2, 1, 2 * tile), F32),
                        pltpu.VMEM((2, LANES, 2 * tile), F32),
                        pltpu.VMEM((RING_SLOTS, pages, pcols, LANES), F32),
                        pltpu.VMEM((RING_SLOTS, pages, pcols, LANES), F32),
                        pltpu.SemaphoreType.DMA((RING_SLOTS, 2, pages)),
                        pltpu.VMEM((nrow, LANES), BF16), pltpu.VMEM((nrow, pcols), F32), pltpu.VMEM((nrow, pcols), F32),
                        pltpu.VMEM((nrow, LANES), F32), pltpu.VMEM((nrow, 1), F32), pltpu.VMEM((nrow, 1), F32),
                        pltpu.VMEM((nrow, LANES), F32)],
    )
    return pl.pallas_call(
        functools.partial(_attn_kernel, tile=tile, nq=nq, pages=pages, tq=tq, groups=groups),
        grid_spec=grid_spec,
        out_shape=[jax.ShapeDtypeStruct((batch, seq // 2, VW), F32)] * 2
                  + [jax.ShapeDtypeStruct((nb * rpb, VW), F32)],
        compiler_params=pltpu.CompilerParams(dimension_semantics=("arbitrary",) * 3, vmem_limit_bytes=VMEM_LIMIT),
        name="attn",
    )(page_table, rel_bias, lam_q, lam_k, qn, qn, knb, vb, km, vm, s_qn, s_kn, s_v, cache_k, cache_v)


def _sample_stream(st, total_steps, pt_ref, rb_ref, lamq_ref, lamk_ref, q_ref, kn_ref, vn_ref, ck_hbm, cv_hbm, o_ref,
                   kbuf, vbuf, sem, w_sc, mask_sc, ptab_sc, ntab_sc, m_sc, l_sc, acc_sc, *, pages, tq, groups):
    nrow = 2 * HEADS * tq
    pcols = kbuf.shape[2]
    page_size = pcols // HEADS
    ng = pt_ref.shape[1] // pages
    steps_per_elem = ng // groups
    first = st % steps_per_elem == 0
    final = st % steps_per_elem == steps_per_elem - 1
    total = total_steps * groups

    def page_copies(gi):
        slot = gi % RING_SLOTS
        out = []
        for u in range(pages):
            page = pt_ref[gi // ng, (gi % ng) * pages + u]
            out.append(pltpu.make_async_copy(ck_hbm.at[page], kbuf.at[slot, u], sem.at[slot, 0, u]))
            out.append(pltpu.make_async_copy(cv_hbm.at[page], vbuf.at[slot, u], sem.at[slot, 1, u]))
        return out

    def build_tables():
        def tables(cols, offset):
            r = lax.broadcasted_iota(jnp.int32, (nrow, cols), 0)
            c = lax.broadcasted_iota(jnp.int32, (nrow, cols), 1)
            tok = r // (2 * HEADS)
            head = (r % (2 * HEADS)) // 2
            n = offset + tok - c // HEADS
            t = jnp.zeros((nrow, cols), F32)
            for h in range(HEADS):
                t = jnp.where(head == h, _bias_table(n, rb_ref, h), t)
            return jnp.where(head == c % HEADS, t, NEG), c // HEADS

        mask_sc[...] = jnp.where(tables(pcols, MAX_DISTANCE)[0] > 0.5 * NEG, 0.0, NEG)
        ptab_sc[...] = tables(pcols, page_size)[0]
        nt, slot = tables(LANES, 0)
        ntab_sc[...] = jnp.where(slot < tq, nt, NEG)

    def update(s, pv):
        m_prev = m_sc[...]
        m_new = jnp.maximum(m_prev, jnp.max(s, axis=-1, keepdims=True))
        alpha = jnp.exp2(m_prev - m_new)
        p = jnp.exp2(s - m_new)
        l_sc[...] = alpha * l_sc[...] + jnp.sum(p, axis=-1, keepdims=True)
        acc_sc[...] = alpha * acc_sc[...] + pv(p.astype(BF16))
        m_sc[...] = m_new

    def start_element():
        q = q_ref[...].astype(F32)
        r8 = lax.broadcasted_iota(jnp.int32, (2 * HEADS, LANES), 0)
        lane = lax.broadcasted_iota(jnp.int32, (2 * HEADS, LANES), 1)
        blocks = []
        for t in range(tq):
            blk = jnp.zeros((2 * HEADS, LANES), F32)
            for h in range(HEADS):
                blk = jnp.where(r8 // 2 == h, jnp.broadcast_to(q[t:t + 1, h * LANES:(h + 1) * LANES], blk.shape), blk)
            blocks.append(jnp.where(lane // DK == r8 % 2, blk, 0.0))
        w_sc[...] = jnp.concatenate(blocks, axis=0).astype(BF16)
        m_sc[...] = jnp.full(m_sc.shape, NEG, F32)
        l_sc[...] = jnp.zeros(l_sc.shape, F32)
        acc_sc[...] = jnp.zeros(acc_sc.shape, F32)
        pad = jnp.zeros((LANES - kn_ref.shape[0], LANES), F32)
        kn = jnp.concatenate([kn_ref[...], pad], axis=0).astype(BF16)
        vn = jnp.concatenate([vn_ref[...], pad], axis=0).astype(BF16)
        s = _dot_nt(w_sc[...], kn) + ntab_sc[...]
        update(s, lambda p: _dot(p, vn))

    def group(n):
        gi = st * groups + n
        if n == 0:
            @pl.when(st == 0)
            def _():
                for ahead in range(RING_SLOTS - 1):
                    for c in page_copies(ahead):
                        c.start()
                build_tables()

        @pl.when(gi + RING_SLOTS - 1 < total)
        def _():
            for c in page_copies(gi + RING_SLOTS - 1):
                c.start()

        if n == 0:
            pl.when(first)(start_element)
        for c in page_copies(gi):
            c.wait()
        slot = gi % RING_SLOTS
        w = w_sc[...]
        parts = []
        for u in range(pages):
            s = _dot_nt(w, kbuf[slot, u].astype(BF16))
            if u == pages - 1 and n == groups - 1:
                s = s + jnp.where(final, ptab_sc[...], mask_sc[...])
            else:
                s = s + mask_sc[...]
            parts.append(s)
        s = jnp.concatenate(parts, axis=-1)

        def pv(p):
            acc = _dot(p[:, 0:pcols], vbuf[slot, 0].astype(BF16))
            for u in range(1, pages):
                acc += _dot(p[:, u * pcols:(u + 1) * pcols], vbuf[slot, u].astype(BF16))
            return acc

        update(s, pv)
        if n == groups - 1:
            pl.when(final)(finish_element)

    def finish_element():
        lam = _lam_value(lamq_ref, lamk_ref)
        accn = acc_sc[...] * (1.0 / l_sc[...])
        o_ref[...] = jnp.zeros(o_ref.shape, F32)
        for t in range(tq):
            for h in range(HEADS):
                r = t * 2 * HEADS + 2 * h
                o_ref[t:t + 1, h * DV:(h + 1) * DV] = accn[r:r + 1, :] - lam * accn[r + 1:r + 2, :]

    return group


def _gla_kernel(s0_ref, q_ref, k_ref, v_ref, lg_ref, tri_ref, o_ref, sout_ref, s_sc, *, valid):
    c = pl.program_id(1)
    nseq = q_ref.shape[0]

    @pl.when(c == 0)
    def _():
        for i in range(nseq):
            s_sc[i] = s0_ref[0 if s0_ref.shape[0] != nseq else i]

    rows_in = q_ref.shape[1]

    def padded(a):
        if rows_in == CHUNK:
            return a
        return jnp.concatenate([a, jnp.zeros((CHUNK - rows_in, a.shape[1]), a.dtype)], axis=0)

    wide = lambda ref: jnp.concatenate([padded(ref[i]) for i in range(nseq)], axis=1)
    q = wide(q_ref)
    k = wide(k_ref)
    lg = wide(lg_ref)
    vb = wide(v_ref)
    if valid < CHUNK:
        live = lax.broadcasted_iota(jnp.int32, (CHUNK, 1), 0) < valid
        k = jnp.where(live, k, 0.0)
        lg = jnp.where(live, lg, 0.0)
        vb = jnp.where(live, vb, jnp.zeros_like(vb))

    lg_hi = lg.astype(BF16)
    lg_lo = (lg - lg_hi.astype(F32)).astype(BF16)
    tri = tri_ref[...]
    b = _dot(tri, lg_hi) + _dot(tri, lg_lo)
    b_mid = b[CHUNK // 2 - 1:CHUNK // 2, :]
    qs = q * jnp.exp(b)
    qt = q * jnp.exp(b - b_mid)
    kt = (k * jnp.exp(b_mid - b)).astype(BF16)
    b_t = b.T
    b_last = b_t[:, CHUNK - 1:CHUNK]
    kl_t = (k.T * jnp.exp(b_last - b_t)).astype(BF16)
    decay = jnp.exp(b_last)

    row = lax.broadcasted_iota(jnp.int32, (CHUNK, CHUNK), 0)
    col = lax.broadcasted_iota(jnp.int32, (CHUNK, CHUNK), 1)
    causal = row >= col
    lane = lax.broadcasted_iota(jnp.int32, (1, LANES), 1)
    heads = [(i, h) for i in range(nseq) for h in range(HEADS)]
    cols = lambda i, h: slice(i * GW + (h // 2) * LANES, i * GW + (h // 2 + 1) * LANES)
    mine = lambda h: (lane // DK) == (h % 2)
    v_of = lambda i, h: vb[:, i * VW + h * DV:i * VW + (h + 1) * DV]
    s_old = [s_sc[i] for i in range(nseq)]
    s_bf = [s.astype(BF16) for s in s_old]
    a = [_dot_nt(jnp.where(mine(h), qt[:, cols(i, h)], 0.0).astype(BF16), kt[:, cols(i, h)]) for i, h in heads]
    inter = [_dot(jnp.where(mine(h), qs[:, cols(i, h)], 0.0).astype(BF16),
                  s_bf[i][(h // 2) * LANES:(h // 2 + 1) * LANES, :]) for i, h in heads]
    upd = [_dot(kl_t[i * GW + h * DK:i * GW + (h + 1) * DK, :], v_of(i, h)) for i, h in heads]
    for n, (i, h) in enumerate(heads):
        o_h = inter[n] + _dot(jnp.where(causal, a[n], 0.0).astype(BF16), v_of(i, h))
        o_ref[i, :, h * DV:(h + 1) * DV] = o_h[0:rows_in]
        rows = slice(i * GW + h * DK, i * GW + (h + 1) * DK)
        s_sc[i, h * DK:(h + 1) * DK, :] = decay[rows, :] * s_old[i][h * DK:(h + 1) * DK, :] + upd[n]

    @pl.when(c == pl.num_programs(1) - 1)
    def _():
        sout_ref[...] = s_sc[...]


def _gla(s0, gq, gk, gv, lg, tri, nb, rows_in, valid):
    nc = gq.shape[0] // (nb * rows_in)
    per = min(nb, GLA_SEQS_PER_STEP)
    assert nb % per == 0
    s0_blk = per if s0.shape[0] == nb else 1
    s0_map = (lambda b, c: (b, 0, 0)) if s0.shape[0] == nb else (lambda b, c: (0, 0, 0))
    seqs = lambda a: a.reshape(nb, nc * rows_in, a.shape[-1])
    rowblk = lambda w: pl.BlockSpec((per, rows_in, w), lambda b, c: (b, c, 0))
    o, s_fin = pl.pallas_call(
        functools.partial(_gla_kernel, valid=valid),
        grid=(nb // per, nc),
        in_specs=[pl.BlockSpec((s0_blk, GW, DV), s0_map), rowblk(GW), rowblk(GW), rowblk(VW), rowblk(GW),
                  pl.BlockSpec(tri.shape, lambda b, c: (0, 0))],
        out_specs=[rowblk(VW), pl.BlockSpec((per, GW, DV), lambda b, c: (b, 0, 0))],
        out_shape=[jax.ShapeDtypeStruct((nb, nc * rows_in, VW), F32), jax.ShapeDtypeStruct((nb, GW, DV), F32)],
        scratch_shapes=[pltpu.VMEM((per, GW, DV), F32)],
        compiler_params=pltpu.CompilerParams(dimension_semantics=("arbitrary", "arbitrary"),
                                             vmem_limit_bytes=VMEM_LIMIT),
        name="gla",
    )(s0, seqs(gq), seqs(gk), seqs(gv), seqs(lg), tri)
    return o.reshape(nb * nc * rows_in, VW), s_fin


def _merge_kernel(x_ref, *rest, seq_tiles):
    if seq_tiles is None:
        od_ref, og_ref, gate_ref, dnw_ref, gnw_ref, wtop_ref, wbot_ref, y_ref = rest
        od = od_ref[...]
    else:
        oda_ref, odb_ref, og_ref, gate_ref, dnw_ref, gnw_ref, wtop_ref, wbot_ref, y_ref = rest
        od = jnp.where(pl.program_id(0) % seq_tiles < seq_tiles // 2, oda_ref[...], odb_ref[...])

    def head_norm(o, w):
        parts = []
        for h in range(HEADS):
            sl = o[:, h * DV:(h + 1) * DV]
            parts.append(sl * lax.rsqrt(jnp.mean(sl * sl, axis=-1, keepdims=True) + RMS_EPS))
        return jnp.concatenate(parts, axis=-1) * w

    gate = gate_ref[...].astype(F32)
    mix_d = (head_norm(od, dnw_ref[...]) * gate[:, 0:VW]).astype(BF16)
    mix_g = (head_norm(og_ref[...], gnw_ref[...]) * gate[:, VW:]).astype(BF16)
    y_ref[...] = x_ref[...] + _dot(mix_d, wtop_ref[...]) + _dot(mix_g, wbot_ref[...])


def _merge(x, od_parts, og, gates, dnw, gnw, wtop, wbot, tr, seq_tiles=None):
    rows = x.shape[0]
    full = lambda a: pl.BlockSpec(a.shape, lambda i: (0,) * a.ndim)
    row = lambda w: pl.BlockSpec((tr, w), lambda i: (i, 0))
    if seq_tiles is None:
        od_specs = [row(VW)]
    else:
        half = seq_tiles // 2
        od_specs = [pl.BlockSpec((tr, VW), lambda i: ((i // seq_tiles) * half + jnp.minimum(i % seq_tiles, half - 1), 0)),
                    pl.BlockSpec((tr, VW), lambda i: ((i // seq_tiles) * half + jnp.maximum(i % seq_tiles - half, 0), 0))]
    return pl.pallas_call(
        functools.partial(_merge_kernel, seq_tiles=seq_tiles),
        grid=(rows // tr,),
        in_specs=[row(D_MODEL)] + od_specs + [row(VW), row(2 * VW), full(dnw), full(gnw), full(wtop), full(wbot)],
        out_specs=row(D_MODEL),
        out_shape=jax.ShapeDtypeStruct((rows, D_MODEL), F32),
        compiler_params=pltpu.CompilerParams(dimension_semantics=("arbitrary",), vmem_limit_bytes=VMEM_LIMIT),
        name="merge",
    )(x, *od_parts, og, gates, dnw, gnw, wtop, wbot)


def kernel(x_prompt, x_sample, cache_k, cache_v, state_gla, page_table, meta_tokens, rel_bias, norm_w, w_in,
           q_norm_w, k_norm_w, lam_q, lam_k, diff_norm_w, gla_wa2, gla_ba, gla_norm_w, w_out):
    batch, seq, _ = x_prompt.shape
    nb, tq, _ = x_sample.shape
    tile = 512
    rpb = 16

    w = w_in[0]
    wm = w[:, :Z_MAIN].astype(BF16)
    wa1 = jnp.pad(w[:, Z_MAIN:], ((0, 0), (0, LANES - GLA_GATE_RANK))).astype(BF16)
    wa2 = jnp.pad(gla_wa2[0], ((0, LANES - GLA_GATE_RANK), (0, 0))).astype(BF16)
    ba = gla_ba[0][None]
    nw = norm_w[0][None]
    qw = jnp.tile(q_norm_w[0].reshape(-1), HEADS)[None] * (DIFF_SCALE * LOG2E)
    kw = jnp.tile(k_norm_w[0].reshape(-1), HEADS)[None]
    grp = np.arange(256) // DK
    gmat = jnp.asarray(grp[:, None] == grp[None, :], BF16)
    consts = (nw, wm, wa1, wa2, ba, qw, kw, gmat)
    dnw = jnp.tile(diff_norm_w[0], HEADS)[None] * (1.0 - LAM_INIT)
    gnw = jnp.tile(gla_norm_w[0], HEADS)[None]
    wtop = w_out[0][:VW].astype(BF16)
    wbot = w_out[0][VW:].astype(BF16)
    t_idx = np.arange(CHUNK)
    tri = jnp.asarray(t_idx[:, None] >= t_idx[None, :], BF16)
    lq, lk = lam_q[0], lam_k[0]

    xs = jnp.pad(x_sample, ((0, 0), (0, rpb - tq), (0, 0))).reshape(nb * rpb, D_MODEL)
    xm = jnp.pad(meta_tokens, ((0, CHUNK - N_META), (0, 0)))
    x_small = jnp.concatenate([xs, xm], axis=0)
    ns = nb * rpb
    small = _inproj(x_small, x_small.shape[0], consts)
    s_qn, s_kn, s_knb, s_v, s_vb, s_gate, s_gq, s_gk, s_gv, s_lg = [a[:ns] for a in small]
    m_qn, m_kn, m_knb, m_v, m_vb, m_gate, m_gq, m_gk, m_gv, m_lg = [a[ns:] for a in small]

    xp = x_prompt.reshape(batch * seq, D_MODEL)
    meta_rows = (m_kn[:N_META].reshape(N_META * HEADS, 2 * DK), m_v[:N_META].reshape(N_META * HEADS, DV))
    p_qn, k_rows, p_knb, v_rows, p_vb, p_gate, p_gq, p_gk, p_gv, p_lg = _inproj(
        xp, tile, consts, seq_tiles=seq // tile, meta_rows=meta_rows)

    n_pool, page_size = cache_k.shape[1], cache_k.shape[2]
    ck = cache_k.reshape(n_pool, page_size * HEADS, 2 * DK)
    cv = cache_v.reshape(n_pool, page_size * HEADS, DV)
    o_first, o_second, os_d = _attn(
        page_table, rel_bias, lq, lk, p_qn.reshape(batch, seq, QKW), p_knb.reshape(batch, seq, QKW),
        p_vb.reshape(batch, seq, VW), m_knb, m_vb, s_qn, s_kn.reshape(ns * HEADS, 2 * DK),
        s_v.reshape(ns * HEADS, DV), ck, cv, tile, tq, ATTN_PAGES)
    o_d = (o_first, o_second)

    zero_state = jnp.zeros((1, GW, DV), F32)
    _, s_meta = _gla(zero_state, m_gq, m_gk, m_gv, m_lg, tri, 1, CHUNK, N_META)
    o_g, s_fin = _gla(s_meta, p_gq, p_gk, p_gv, p_lg, tri, batch, CHUNK, CHUNK)
    os_g, s_new = _gla(state_gla[0].reshape(nb, GW, DV), s_gq, s_gk, s_gv, s_lg, tri, nb, rpb, tq)

    half_rows = batch * seq // 2
    y_prompt = _merge(xp, [o.reshape(half_rows, VW) for o in o_d], o_g, p_gate, dnw, gnw, wtop, wbot, tile,
                      seq_tiles=seq // tile)
    y_small = _merge(xs, [os_d], os_g, s_gate, dnw, gnw, wtop, wbot, ns)

    y_prompt = y_prompt.reshape(batch, seq, D_MODEL)
    y_sample = y_small.reshape(nb, rpb, D_MODEL)[:, :tq]
    k_prompt = k_rows.reshape(1, batch, seq + N_META, HEADS, 2 * DK)
    v_prompt = v_rows.reshape(1, batch, seq + N_META, HEADS, DV)
    s_prompt = s_fin.reshape(1, batch, HEADS, DK, DV)
    k_sample = s_kn.reshape(nb, rpb, HEADS, 2 * DK)[None, :, :tq]
    v_sample = s_v.reshape(nb, rpb, HEADS, DV)[None, :, :tq]
    s_sample = s_new.reshape(1, nb, HEADS, DK, DV)
    return (y_prompt, y_sample, k_prompt, v_prompt, s_prompt, k_sample, v_sample, s_sample)
```

```python
import functools
import math

import numpy as np
import jax
import jax.numpy as jnp
from jax import lax
from jax.experimental import pallas as pl
from jax.experimental.pallas import tpu as pltpu

D_MODEL = 1024
N_META = 16
HEADS = 4
DK = 64
DV = 128
DIFF_SCALE = DK ** -0.5
GLA_GATE_RANK = 16
GLA_GATE_NORM = 16.0
NUM_BUCKETS = 32
MAX_DISTANCE = 128
RMS_EPS = 1e-6
LAM_INIT = 0.8 - 0.6 * math.exp(-0.3 * 0)
QKW = HEADS * 2 * DK
VW = HEADS * DV
GW = HEADS * DK
Z_MAIN = 3 * QKW + VW + 2 * GW + 2 * VW
LANES = 128
CHUNK = 128
NEG = -1e30
LOG2E = math.log2(math.e)
RING_SLOTS = 3
ATTN_PAGES = 16
GLA_SEQS_PER_STEP = 4
VMEM_LIMIT = 56 * 1024 * 1024

F32 = jnp.float32
BF16 = jnp.bfloat16


def _dot(a, b):
    return jnp.dot(a, b, preferred_element_type=F32)


def _dot_nt(a, b):
    return lax.dot_general(a, b, (((1,), (1,)), ((), ())), preferred_element_type=F32)


def _dot_tn(a, b):
    return lax.dot_general(a, b, (((0,), (0,)), ((), ())), preferred_element_type=F32)


def _lam_value(lamq_ref, lamk_ref):
    e = jnp.exp(jnp.sum(lamq_ref[...] * lamk_ref[...], axis=-1, keepdims=True))
    return e[0:1, :] - e[1:2, :] + LAM_INIT


def _inproj_kernel(x_ref, nw_ref, wm_ref, wa1_ref, wa2_ref, ba_ref, qw_ref, kw_ref, g_ref, *rest, seq_tiles):
    if seq_tiles is None:
        qn_ref, kn_ref, knb_ref, v_ref, vb_ref, gate_ref, gq_ref, gk_ref, gv_ref, lg_ref = rest
    else:
        (mk_ref, mv_ref, qn_ref, kn_hbm, knb_ref, v_hbm, vb_ref, gate_ref, gq_ref, gk_ref, gv_ref, lg_ref,
         stage, sem, meta_sem) = rest
    x = x_ref[...]
    ms = jnp.mean(x * x, axis=-1, keepdims=True)
    hn = (x * lax.rsqrt(ms + RMS_EPS) * nw_ref[...]).astype(BF16)

    def proj(lo, hi):
        return _dot(hn, wm_ref[:, lo:hi])

    def group_norm(z, w):
        sq = (z * z).astype(BF16)
        ss = jnp.concatenate([_dot(sq[:, c:c + 256], g_ref[...]) for c in range(0, QKW, 256)], axis=-1)
        return z * lax.rsqrt(ss * (1.0 / DK) + RMS_EPS) * w

    qn_ref[...] = group_norm(proj(0, QKW), qw_ref[...]).astype(BF16)
    kn = group_norm(proj(QKW, 2 * QKW), kw_ref[...])
    knb_ref[...] = kn.astype(BF16)
    v = proj(2 * QKW, 2 * QKW + VW)
    vb_ref[...] = v.astype(BF16)
    if seq_tiles is None:
        kn_ref[...] = kn
        v_ref[...] = v
    else:
        _write_cache_rows(kn, v, mk_ref, mv_ref, kn_hbm, v_hbm, stage, sem, meta_sem, seq_tiles)
    o = 2 * QKW + VW
    dg = proj(o, o + VW)
    gate_ref[:, 0:VW] = (dg * jax.nn.sigmoid(dg)).astype(BF16)
    o += VW
    gq_ref[...] = proj(o, o + GW) * (DK ** -0.5)
    gk_ref[...] = proj(o + GW, o + 2 * GW)
    o += 2 * GW
    gv_ref[...] = proj(o, o + VW).astype(BF16)
    o += VW
    gg = proj(o, o + VW)
    gate_ref[:, VW:2 * VW] = (gg * jax.nn.sigmoid(gg)).astype(BF16)
    ga = _dot(hn, wa1_ref[...])
    xg = _dot(ga.astype(BF16), wa2_ref[...]) + ba_ref[...]
    lg_ref[...] = (jnp.minimum(xg, 0.0) - jnp.log(1.0 + jnp.exp(-jnp.abs(xg)))) * (1.0 / GLA_GATE_NORM)


def _write_cache_rows(kn, v, mk_ref, mv_ref, k_hbm, v_hbm, stage, sem, meta_sem, seq_tiles):
    i = pl.program_id(0)
    n = pl.num_programs(0)
    tr = kn.shape[0]
    seq_rows = (seq_tiles * tr + N_META) * HEADS

    def tile_copies(step):
        row0 = (step // seq_tiles) * seq_rows + (N_META + (step % seq_tiles) * tr) * HEADS
        dst = pl.ds(pl.multiple_of(row0, 8), tr * HEADS)
        return [pltpu.make_async_copy(stage.at[step % 2, 0], k_hbm.at[dst], sem.at[step % 2, 0]),
                pltpu.make_async_copy(stage.at[step % 2, 1], v_hbm.at[dst], sem.at[step % 2, 1])]

    @pl.when(i >= 2)
    def _():
        for c in tile_copies(i - 2):
            c.wait()

    slot = i % 2
    for h in range(HEADS):
        stage[slot, 0, pl.ds(h, tr, stride=HEADS), :] = kn[:, h * LANES:(h + 1) * LANES]
        stage[slot, 1, pl.ds(h, tr, stride=HEADS), :] = v[:, h * LANES:(h + 1) * LANES]
    for c in tile_copies(i):
        c.start()

    @pl.when(i % seq_tiles == 0)
    def _():
        dst = pl.ds(pl.multiple_of((i // seq_tiles) * seq_rows, 8), N_META * HEADS)
        meta = [pltpu.make_async_copy(mk_ref, k_hbm.at[dst], meta_sem.at[0]),
                pltpu.make_async_copy(mv_ref, v_hbm.at[dst], meta_sem.at[1])]
        for c in meta:
            c.start()
        for c in meta:
            c.wait()

    @pl.when(i == n - 1)
    def _():
        for c in tile_copies(i - 1) + tile_copies(i):
            c.wait()


def _inproj(x, tr, consts, seq_tiles=None, meta_rows=None):
    rows = x.shape[0]
    steps = rows // tr
    full = lambda a: pl.BlockSpec(a.shape, lambda i: (0,) * a.ndim)
    row = lambda w: pl.BlockSpec((tr, w), lambda i: (i, 0))
    outs = [(QKW, BF16), (QKW, F32), (QKW, BF16), (VW, F32), (VW, BF16), (2 * VW, BF16),
            (GW, F32), (GW, F32), (VW, BF16), (GW, F32)]
    out_specs = [row(w) for w, _ in outs]
    out_shape = [jax.ShapeDtypeStruct((rows, w), dt) for w, dt in outs]
    extra_in, extra_specs, scratch = [], [], []
    if seq_tiles is not None:
        assert steps % seq_tiles == 0 and steps >= 2
        cache_rows = (steps // seq_tiles) * (seq_tiles * tr + N_META) * HEADS
        for o in (1, 3):
            out_specs[o] = pl.BlockSpec(memory_space=pl.ANY)
            out_shape[o] = jax.ShapeDtypeStruct((cache_rows, LANES), F32)
        extra_in = list(meta_rows)
        extra_specs = [full(a) for a in meta_rows]
        scratch = [pltpu.VMEM((2, 2, tr * HEADS, LANES), F32), pltpu.SemaphoreType.DMA((2, 2)),
                   pltpu.SemaphoreType.DMA((2,))]
    return pl.pallas_call(
        functools.partial(_inproj_kernel, seq_tiles=seq_tiles),
        grid=(steps,),
        in_specs=[row(D_MODEL)] + [full(a) for a in consts] + extra_specs,
        out_specs=out_specs,
        out_shape=out_shape,
        scratch_shapes=scratch,
        compiler_params=pltpu.CompilerParams(dimension_semantics=("arbitrary",), vmem_limit_bytes=VMEM_LIMIT),
        name="inproj",
    )(x, *consts, *extra_in)


def _bucket_ranges():
    n = np.arange(MAX_DISTANCE)
    max_exact = NUM_BUCKETS // 2
    nf = np.maximum(n, 1).astype(np.float32)
    large = max_exact + (np.log(nf / np.float32(max_exact)) / np.float32(math.log(MAX_DISTANCE / max_exact))
                         * np.float32(NUM_BUCKETS - max_exact)).astype(np.int32)
    bucket = np.where(n < max_exact, n, np.minimum(large, NUM_BUCKETS - 1))
    return [(int(n[bucket == b].min()), int(n[bucket == b].max())) for b in range(NUM_BUCKETS)]


_BUCKET_RANGES = _bucket_ranges()


def _bias_table(n, rb_ref, h):
    far = rb_ref[NUM_BUCKETS - 1, h]
    t = jnp.zeros(n.shape, F32)
    for b, (lo, hi) in enumerate(_BUCKET_RANGES[:-1]):
        val = (rb_ref[b, h] - far) * LOG2E
        cond = (n == lo) if lo == hi else ((n >= lo) & (n <= hi))
        t = jnp.where(cond, val, t)
    return jnp.where(n < 0, NEG, t)


def _attn_kernel(pt_ref, rb_ref, lamq_ref, lamk_ref, qa_ref, qb_ref, k_ref, v_ref, km_ref, vm_ref,
                 sq_ref, skn_ref, svn_ref, ck_hbm, cv_hbm, oa_ref, ob_ref, so_ref,
                 bias_sc, mt_sc, qs_sc, m_sc, l_sc, acc_sc, *sample_scratch, tile, nq, pages, tq, groups):
    h = pl.program_id(0)
    b = pl.program_id(1)
    ip = pl.program_id(2)
    half = nq // 2
    st = (h * pl.num_programs(1) + b) * half + ip
    total_steps = pl.num_programs(0) * pl.num_programs(1) * half
    sample_begin, sample_scores, sample_softmax, sample_values, sample_end = _sample_stream(
        st, total_steps, pt_ref, rb_ref, lamq_ref, lamk_ref, sq_ref, skn_ref, svn_ref, ck_hbm, cv_hbm, so_ref,
        *sample_scratch, pages=pages, tq=tq, groups=groups)
    sample_begin()

    @pl.when((b == 0) & (ip == 0))
    def _():
        kk = lax.broadcasted_iota(jnp.int32, (tile, tile), 0)
        qq = lax.broadcasted_iota(jnp.int32, (tile, tile), 1)
        bias_sc[0] = jnp.zeros((tile, tile), F32)
        bias_sc[1] = _bias_table(qq - kk + tile, rb_ref, h)
        bias_sc[2] = _bias_table(qq - kk, rb_ref, h)
        km = lax.broadcasted_iota(jnp.int32, (N_META, tile), 0)
        qm = lax.broadcasted_iota(jnp.int32, (N_META, tile), 1)
        mt_sc[...] = _bias_table(N_META + qm - km, rb_ref, h)

    lane = lax.broadcasted_iota(jnp.int32, (1, LANES), 1)
    for t, q_ref in enumerate((qa_ref, qb_ref)):
        q = q_ref[...]
        zero = jnp.zeros_like(q)
        qs_sc[t] = jnp.concatenate([jnp.where(lane < DK, q, zero), jnp.where(lane >= DK, q, zero)], axis=0)
    m_sc[...] = jnp.full(m_sc.shape, NEG, F32)
    l_sc[...] = jnp.zeros(l_sc.shape, F32)
    acc_sc[...] = jnp.zeros(acc_sc.shape, F32)

    def plan(n):
        if n < half:
            slot = jnp.where(n > ip, 1, 0)
            key_tile = jnp.where(n <= ip, n, n - ip - 1)
            return slot, key_tile, jnp.where(n == ip, 2, jnp.where(n == ip - 1, 1, 0))
        return 1, n - ip - 1, {nq - 1: 1, nq: 2}.get(n)

    def rows(key_tile):
        return pl.ds(pl.multiple_of(key_tile * tile, tile), tile)

    def scores(slot, key_tile, kind):
        s = _dot_nt(k_ref[rows(key_tile), :], qs_sc[slot])
        if kind is None:
            return s
        bias = bias_sc[kind]
        return s + jnp.concatenate([bias, bias], axis=1)

    def update(slot, s, vt):
        m_prev = m_sc[slot]
        m_new = jnp.maximum(m_prev, jnp.max(s, axis=0, keepdims=True))
        alpha = jnp.exp2(m_prev - m_new)
        p = jnp.exp2(s - m_new)
        l_sc[slot] = alpha * l_sc[slot] + jnp.sum(p, axis=0, keepdims=True)
        acc_sc[slot] = alpha * acc_sc[slot] + _dot_tn(vt, p.astype(BF16))
        m_sc[slot] = m_new

    meta_bias = jnp.where(ip == 0, mt_sc[...], 0.0)
    update(0, _dot_nt(km_ref[...], qs_sc[0]) + jnp.concatenate([meta_bias, meta_bias], axis=1), vm_ref[...])
    update(1, _dot_nt(km_ref[...], qs_sc[1]), vm_ref[...])

    plans = [plan(n) for n in range(nq + 1)]
    spread = (nq + 1) // groups
    s_next = scores(*plans[0])
    for n in range(nq + 1):
        s_cur = s_next
        group = n // spread if (n % spread == 0 and n // spread < groups) else None
        if group is not None:
            sample_s = sample_scores(group)
        if n < nq:
            s_next = scores(*plans[n + 1])
        if group is not None:
            sample_alpha, sample_p = sample_softmax(sample_s)
        update(plans[n][0], s_cur, v_ref[rows(plans[n][1]), :])
        if group is not None:
            sample_values(group, sample_alpha, sample_p)

    lam = _lam_value(lamq_ref, lamk_ref)
    for t, o_ref in enumerate((oa_ref, ob_ref)):
        acc = acc_sc[t]
        inv = 1.0 / l_sc[t]
        o_t = acc[:, 0:tile] * inv[:, 0:tile] - (lam * inv[:, tile:]) * acc[:, tile:]
        o_ref[...] = o_t.T
    sample_end()


def _attn(page_table, rel_bias, lam_q, lam_k, qn, knb, vb, km, vm, s_qn, s_kn, s_v, cache_k, cache_v, tile, tq, pages):
    batch, seq, _ = qn.shape
    nq = seq // tile
    half = nq // 2
    assert nq % 2 == 0
    steps = HEADS * batch * half
    nb, n_pages = page_table.shape
    rpb = s_qn.shape[0] // nb
    pcols = cache_k.shape[1]
    nrow = 2 * HEADS * tq
    total_groups = nb * (n_pages // pages)
    groups = total_groups // steps
    assert groups * steps == total_groups and (n_pages // pages) % groups == 0
    assert groups <= nq + 1 and total_groups >= RING_SLOTS - 1
    steps_per_elem = (n_pages // pages) // groups
    small = lambda a: pl.BlockSpec(a.shape, lambda h, b, ip, pt: (0,) * a.ndim)
    kv = pl.BlockSpec((None, seq, LANES), lambda h, b, ip, pt: (b, 0, h))
    meta = pl.BlockSpec((N_META, LANES), lambda h, b, ip, pt: (0, h))
    short = lambda h, b, ip, pt: (b, ip, h)
    elem = lambda h, b, ip, pt: (((h * batch + b) * half + ip) // steps_per_elem, 0)
    grid_spec = pltpu.PrefetchScalarGridSpec(
        num_scalar_prefetch=1,
        grid=(HEADS, batch, half),
        in_specs=[pl.BlockSpec(memory_space=pltpu.SMEM), small(lam_q), small(lam_k),
                  pl.BlockSpec((None, tile, LANES), short),
                  pl.BlockSpec((None, tile, LANES), lambda h, b, ip, pt: (b, nq - 1 - ip, h)),
                  kv, kv, meta, meta,
                  pl.BlockSpec((rpb, QKW), elem), pl.BlockSpec((rpb * HEADS, LANES), elem),
                  pl.BlockSpec((rpb * HEADS, LANES), elem),
                  pl.BlockSpec(memory_space=pl.ANY), pl.BlockSpec(memory_space=pl.ANY)],
        out_specs=[pl.BlockSpec((None, tile, LANES), short),
                   pl.BlockSpec((None, tile, LANES), lambda h, b, ip, pt: (b, half - 1 - ip, h)),
                   pl.BlockSpec((rpb, VW), elem)],
        scratch_shapes=[pltpu.VMEM((3, tile, tile), F32), pltpu.VMEM((N_META, tile), F32),
                        pltpu.VMEM((2, 2 * tile, LANES), BF16),
                        pltpu.VMEM((2, 1, 2 * tile), F32), pltpu.VMEM((2, 1, 2 * tile), F32),
                        pltpu.VMEM((2, LANES, 2 * tile), F32),
                        pltpu.VMEM((RING_SLOTS, pages, pcols, LANES), F32),
                        pltpu.VMEM((RING_SLOTS, pages, pcols, LANES), F32),
                        pltpu.SemaphoreType.DMA((RING_SLOTS, 2, pages)),
                        pltpu.VMEM((nrow, LANES), BF16), pltpu.VMEM((nrow, pcols), F32), pltpu.VMEM((nrow, pcols), F32),
                        pltpu.VMEM((nrow, LANES), F32), pltpu.VMEM((nrow, 1), F32), pltpu.VMEM((nrow, 1), F32),
                        pltpu.VMEM((nrow, LANES), F32)],
    )
    return pl.pallas_call(
        functools.partial(_attn_kernel, tile=tile, nq=nq, pages=pages, tq=tq, groups=groups),
        grid_spec=grid_spec,
        out_shape=[jax.ShapeDtypeStruct((batch, seq // 2, VW), F32)] * 2
                  + [jax.ShapeDtypeStruct((nb * rpb, VW), F32)],
        compiler_params=pltpu.CompilerParams(dimension_semantics=("arbitrary",) * 3, vmem_limit_bytes=VMEM_LIMIT),
        name="attn",
    )(page_table, rel_bias, lam_q, lam_k, qn, qn, knb, vb, km, vm, s_qn, s_kn, s_v, cache_k, cache_v)


def _sample_stream(st, total_steps, pt_ref, rb_ref, lamq_ref, lamk_ref, q_ref, kn_ref, vn_ref, ck_hbm, cv_hbm, o_ref,
                   kbuf, vbuf, sem, w_sc, mask_sc, ptab_sc, ntab_sc, m_sc, l_sc, acc_sc, *, pages, tq, groups):
    nrow = 2 * HEADS * tq
    pcols = kbuf.shape[2]
    page_size = pcols // HEADS
    ng = pt_ref.shape[1] // pages
    steps_per_elem = ng // groups
    first = st % steps_per_elem == 0
    final = st % steps_per_elem == steps_per_elem - 1
    total = total_steps * groups

    def page_copies(gi):
        slot = gi % RING_SLOTS
        src = jnp.minimum(gi, total - 1)
        out = []
        for u in range(pages):
            page = pt_ref[src // ng, (src % ng) * pages + u]
            out.append(pltpu.make_async_copy(ck_hbm.at[page], kbuf.at[slot, u], sem.at[slot, 0, u]))
            out.append(pltpu.make_async_copy(cv_hbm.at[page], vbuf.at[slot, u], sem.at[slot, 1, u]))
        return out

    def build_tables():
        def tables(cols, offset):
            r = lax.broadcasted_iota(jnp.int32, (nrow, cols), 0)
            c = lax.broadcasted_iota(jnp.int32, (nrow, cols), 1)
            tok = r // (2 * HEADS)
            head = (r % (2 * HEADS)) // 2
            n = offset + tok - c // HEADS
            t = jnp.zeros((nrow, cols), F32)
            for h in range(HEADS):
                t = jnp.where(head == h, _bias_table(n, rb_ref, h), t)
            return jnp.where(head == c % HEADS, t, NEG), c // HEADS

        mask_sc[...] = jnp.where(tables(pcols, MAX_DISTANCE)[0] > 0.5 * NEG, 0.0, NEG)
        ptab_sc[...] = tables(pcols, page_size)[0]
        nt, slot = tables(LANES, 0)
        ntab_sc[...] = jnp.where(slot < tq, nt, NEG)

    def update(s, pv):
        m_prev = m_sc[...]
        m_new = jnp.maximum(m_prev, jnp.max(s, axis=-1, keepdims=True))
        alpha = jnp.exp2(m_prev - m_new)
        p = jnp.exp2(s - m_new)
        l_sc[...] = alpha * l_sc[...] + jnp.sum(p, axis=-1, keepdims=True)
        acc_sc[...] = alpha * acc_sc[...] + pv(p.astype(BF16))
        m_sc[...] = m_new

    def start_element():
        q = q_ref[...].astype(F32)
        r8 = lax.broadcasted_iota(jnp.int32, (2 * HEADS, LANES), 0)
        lane = lax.broadcasted_iota(jnp.int32, (2 * HEADS, LANES), 1)
        blocks = []
        for t in range(tq):
            blk = jnp.zeros((2 * HEADS, LANES), F32)
            for h in range(HEADS):
                blk = jnp.where(r8 // 2 == h, jnp.broadcast_to(q[t:t + 1, h * LANES:(h + 1) * LANES], blk.shape), blk)
            blocks.append(jnp.where(lane // DK == r8 % 2, blk, 0.0))
        w_sc[...] = jnp.concatenate(blocks, axis=0).astype(BF16)
        m_sc[...] = jnp.full(m_sc.shape, NEG, F32)
        l_sc[...] = jnp.zeros(l_sc.shape, F32)
        acc_sc[...] = jnp.zeros(acc_sc.shape, F32)
        pad = jnp.zeros((LANES - kn_ref.shape[0], LANES), F32)
        kn = jnp.concatenate([kn_ref[...], pad], axis=0).astype(BF16)
        vn = jnp.concatenate([vn_ref[...], pad], axis=0).astype(BF16)
        s = _dot_nt(w_sc[...], kn) + ntab_sc[...]
        update(s, lambda p: _dot(p, vn))

    def begin():
        @pl.when(st == 0)
        def _():
            for ahead in range(RING_SLOTS - 1):
                for c in page_copies(ahead):
                    c.start()
            build_tables()

        pl.when(first)(start_element)

    def scores(n):
        gi = st * groups + n
        for c in page_copies(gi + RING_SLOTS - 1):
            c.start()
        for c in page_copies(gi):
            c.wait()
        slot = gi % RING_SLOTS
        w = w_sc[...]
        parts = []
        for u in range(pages):
            s = _dot_nt(w, kbuf[slot, u].astype(BF16))
            if u == pages - 1 and n == groups - 1:
                s = s + jnp.where(final, ptab_sc[...], mask_sc[...])
            else:
                s = s + mask_sc[...]
            parts.append(s)
        return jnp.concatenate(parts, axis=-1)

    def softmax(s):
        m_prev = m_sc[...]
        m_new = jnp.maximum(m_prev, jnp.max(s, axis=-1, keepdims=True))
        alpha = jnp.exp2(m_prev - m_new)
        p = jnp.exp2(s - m_new)
        l_sc[...] = alpha * l_sc[...] + jnp.sum(p, axis=-1, keepdims=True)
        m_sc[...] = m_new
        return alpha, p.astype(BF16)

    def values(n, alpha, p):
        slot = (st * groups + n) % RING_SLOTS
        acc = _dot(p[:, 0:pcols], vbuf[slot, 0].astype(BF16))
        for u in range(1, pages):
            acc += _dot(p[:, u * pcols:(u + 1) * pcols], vbuf[slot, u].astype(BF16))
        acc_sc[...] = alpha * acc_sc[...] + acc

    def end():
        pl.when(final)(finish_element)

        @pl.when(st == total_steps - 1)
        def _():
            for extra in range(RING_SLOTS - 1):
                for c in page_copies(total + extra):
                    c.wait()

    def finish_element():
        lam = _lam_value(lamq_ref, lamk_ref)
        accn = acc_sc[...] * (1.0 / l_sc[...])
        o_ref[...] = jnp.zeros(o_ref.shape, F32)
        for t in range(tq):
            for h in range(HEADS):
                r = t * 2 * HEADS + 2 * h
                o_ref[t:t + 1, h * DV:(h + 1) * DV] = accn[r:r + 1, :] - lam * accn[r + 1:r + 2, :]

    return begin, scores, softmax, values, end


def _gla_kernel(s0_ref, q_ref, k_ref, v_ref, lg_ref, tri_ref, o_ref, sout_ref, s_sc, *, valid):
    c = pl.program_id(1)
    nseq = q_ref.shape[0]

    @pl.when(c == 0)
    def _():
        for i in range(nseq):
            s_sc[i] = s0_ref[0 if s0_ref.shape[0] != nseq else i]

    rows_in = q_ref.shape[1]

    def padded(a):
        if rows_in == CHUNK:
            return a
        return jnp.concatenate([a, jnp.zeros((CHUNK - rows_in, a.shape[1]), a.dtype)], axis=0)

    wide = lambda ref: jnp.concatenate([padded(ref[i]) for i in range(nseq)], axis=1)
    q = wide(q_ref)
    k = wide(k_ref)
    lg = wide(lg_ref)
    vb = wide(v_ref)
    if valid < CHUNK:
        live = lax.broadcasted_iota(jnp.int32, (CHUNK, 1), 0) < valid
        k = jnp.where(live, k, 0.0)
        lg = jnp.where(live, lg, 0.0)
        vb = jnp.where(live, vb, jnp.zeros_like(vb))

    lg_hi = lg.astype(BF16)
    lg_lo = (lg - lg_hi.astype(F32)).astype(BF16)
    tri = tri_ref[...]
    b = _dot(tri, lg_hi) + _dot(tri, lg_lo)
    b_mid = b[CHUNK // 2 - 1:CHUNK // 2, :]
    qs = q * jnp.exp(b)
    qt = q * jnp.exp(b - b_mid)
    kt = (k * jnp.exp(b_mid - b)).astype(BF16)
    b_t = b.T
    b_last = b_t[:, CHUNK - 1:CHUNK]
    kl_t = (k.T * jnp.exp(b_last - b_t)).astype(BF16)
    decay = jnp.exp(b_last)

    row = lax.broadcasted_iota(jnp.int32, (CHUNK, CHUNK), 0)
    col = lax.broadcasted_iota(jnp.int32, (CHUNK, CHUNK), 1)
    causal = row >= col
    lane = lax.broadcasted_iota(jnp.int32, (1, LANES), 1)
    heads = [(i, h) for i in range(nseq) for h in range(HEADS)]
    cols = lambda i, h: slice(i * GW + (h // 2) * LANES, i * GW + (h // 2 + 1) * LANES)
    mine = lambda h: (lane // DK) == (h % 2)
    v_of = lambda i, h: vb[:, i * VW + h * DV:i * VW + (h + 1) * DV]
    s_old = [s_sc[i] for i in range(nseq)]
    s_bf = [s.astype(BF16) for s in s_old]
    a = [_dot_nt(jnp.where(mine(h), qt[:, cols(i, h)], 0.0).astype(BF16), kt[:, cols(i, h)]) for i, h in heads]
    inter = [_dot(jnp.where(mine(h), qs[:, cols(i, h)], 0.0).astype(BF16),
                  s_bf[i][(h // 2) * LANES:(h // 2 + 1) * LANES, :]) for i, h in heads]
    upd = [_dot(kl_t[i * GW + h * DK:i * GW + (h + 1) * DK, :], v_of(i, h)) for i, h in heads]
    for n, (i, h) in enumerate(heads):
        o_h = inter[n] + _dot(jnp.where(causal, a[n], 0.0).astype(BF16), v_of(i, h))
        o_ref[i, :, h * DV:(h + 1) * DV] = o_h[0:rows_in]
        rows = slice(i * GW + h * DK, i * GW + (h + 1) * DK)
        s_sc[i, h * DK:(h + 1) * DK, :] = decay[rows, :] * s_old[i][h * DK:(h + 1) * DK, :] + upd[n]

    @pl.when(c == pl.num_programs(1) - 1)
    def _():
        sout_ref[...] = s_sc[...]


def _gla(s0, gq, gk, gv, lg, tri, nb, rows_in, valid):
    nc = gq.shape[0] // (nb * rows_in)
    per = min(nb, GLA_SEQS_PER_STEP)
    assert nb % per == 0
    s0_blk = per if s0.shape[0] == nb else 1
    s0_map = (lambda b, c: (b, 0, 0)) if s0.shape[0] == nb else (lambda b, c: (0, 0, 0))
    seqs = lambda a: a.reshape(nb, nc * rows_in, a.shape[-1])
    rowblk = lambda w: pl.BlockSpec((per, rows_in, w), lambda b, c: (b, c, 0))
    o, s_fin = pl.pallas_call(
        functools.partial(_gla_kernel, valid=valid),
        grid=(nb // per, nc),
        in_specs=[pl.BlockSpec((s0_blk, GW, DV), s0_map), rowblk(GW), rowblk(GW), rowblk(VW), rowblk(GW),
                  pl.BlockSpec(tri.shape, lambda b, c: (0, 0))],
        out_specs=[rowblk(VW), pl.BlockSpec((per, GW, DV), lambda b, c: (b, 0, 0))],
        out_shape=[jax.ShapeDtypeStruct((nb, nc * rows_in, VW), F32), jax.ShapeDtypeStruct((nb, GW, DV), F32)],
        scratch_shapes=[pltpu.VMEM((per, GW, DV), F32)],
        compiler_params=pltpu.CompilerParams(dimension_semantics=("arbitrary", "arbitrary"),
                                             vmem_limit_bytes=VMEM_LIMIT),
        name="gla",
    )(s0, seqs(gq), seqs(gk), seqs(gv), seqs(lg), tri)
    return o.reshape(nb * nc * rows_in, VW), s_fin


def _merge_kernel(x_ref, *rest, seq_tiles):
    if seq_tiles is None:
        od_ref, og_ref, gate_ref, dnw_ref, gnw_ref, wtop_ref, wbot_ref, y_ref = rest
        od = od_ref[...]
    else:
        oda_ref, odb_ref, og_ref, gate_ref, dnw_ref, gnw_ref, wtop_ref, wbot_ref, y_ref = rest
        od = jnp.where(pl.program_id(0) % seq_tiles < seq_tiles // 2, oda_ref[...], odb_ref[...])

    def head_norm(o, w):
        parts = []
        for h in range(HEADS):
            sl = o[:, h * DV:(h + 1) * DV]
            parts.append(sl * lax.rsqrt(jnp.mean(sl * sl, axis=-1, keepdims=True) + RMS_EPS))
        return jnp.concatenate(parts, axis=-1) * w

    gate = gate_ref[...].astype(F32)
    mix_d = (head_norm(od, dnw_ref[...]) * gate[:, 0:VW]).astype(BF16)
    mix_g = (head_norm(og_ref[...], gnw_ref[...]) * gate[:, VW:]).astype(BF16)
    y_ref[...] = x_ref[...] + _dot(mix_d, wtop_ref[...]) + _dot(mix_g, wbot_ref[...])


def _merge(x, od_parts, og, gates, dnw, gnw, wtop, wbot, tr, seq_tiles=None):
    rows = x.shape[0]
    full = lambda a: pl.BlockSpec(a.shape, lambda i: (0,) * a.ndim)
    row = lambda w: pl.BlockSpec((tr, w), lambda i: (i, 0))
    if seq_tiles is None:
        od_specs = [row(VW)]
    else:
        half = seq_tiles // 2
        od_specs = [pl.BlockSpec((tr, VW), lambda i: ((i // seq_tiles) * half + jnp.minimum(i % seq_tiles, half - 1), 0)),
                    pl.BlockSpec((tr, VW), lambda i: ((i // seq_tiles) * half + jnp.maximum(i % seq_tiles - half, 0), 0))]
    return pl.pallas_call(
        functools.partial(_merge_kernel, seq_tiles=seq_tiles),
        grid=(rows // tr,),
        in_specs=[row(D_MODEL)] + od_specs + [row(VW), row(2 * VW), full(dnw), full(gnw), full(wtop), full(wbot)],
        out_specs=row(D_MODEL),
        out_shape=jax.ShapeDtypeStruct((rows, D_MODEL), F32),
        compiler_params=pltpu.CompilerParams(dimension_semantics=("arbitrary",), vmem_limit_bytes=VMEM_LIMIT),
        name="merge",
    )(x, *od_parts, og, gates, dnw, gnw, wtop, wbot)


def kernel(x_prompt, x_sample, cache_k, cache_v, state_gla, page_table, meta_tokens, rel_bias, norm_w, w_in,
           q_norm_w, k_norm_w, lam_q, lam_k, diff_norm_w, gla_wa2, gla_ba, gla_norm_w, w_out):
    batch, seq, _ = x_prompt.shape
    nb, tq, _ = x_sample.shape
    tile = 512
    rpb = 16

    w = w_in[0]
    wm = w[:, :Z_MAIN].astype(BF16)
    wa1 = jnp.pad(w[:, Z_MAIN:], ((0, 0), (0, LANES - GLA_GATE_RANK))).astype(BF16)
    wa2 = jnp.pad(gla_wa2[0], ((0, LANES - GLA_GATE_RANK), (0, 0))).astype(BF16)
    ba = gla_ba[0][None]
    nw = norm_w[0][None]
    qw = jnp.tile(q_norm_w[0].reshape(-1), HEADS)[None] * (DIFF_SCALE * LOG2E)
    kw = jnp.tile(k_norm_w[0].reshape(-1), HEADS)[None]
    grp = np.arange(256) // DK
    gmat = jnp.asarray(grp[:, None] == grp[None, :], BF16)
    consts = (nw, wm, wa1, wa2, ba, qw, kw, gmat)
    dnw = jnp.tile(diff_norm_w[0], HEADS)[None] * (1.0 - LAM_INIT)
    gnw = jnp.tile(gla_norm_w[0], HEADS)[None]
    wtop = w_out[0][:VW].astype(BF16)
    wbot = w_out[0][VW:].astype(BF16)
    t_idx = np.arange(CHUNK)
    tri = jnp.asarray(t_idx[:, None] >= t_idx[None, :], BF16)
    lq, lk = lam_q[0], lam_k[0]

    xs = jnp.pad(x_sample, ((0, 0), (0, rpb - tq), (0, 0))).reshape(nb * rpb, D_MODEL)
    xm = jnp.pad(meta_tokens, ((0, CHUNK - N_META), (0, 0)))
    x_small = jnp.concatenate([xs, xm], axis=0)
    ns = nb * rpb
    small = _inproj(x_small, x_small.shape[0], consts)
    s_qn, s_kn, s_knb, s_v, s_vb, s_gate, s_gq, s_gk, s_gv, s_lg = [a[:ns] for a in small]
    m_qn, m_kn, m_knb, m_v, m_vb, m_gate, m_gq, m_gk, m_gv, m_lg = [a[ns:] for a in small]

    xp = x_prompt.reshape(batch * seq, D_MODEL)
    meta_rows = (m_kn[:N_META].reshape(N_META * HEADS, 2 * DK), m_v[:N_META].reshape(N_META * HEADS, DV))
    p_qn, k_rows, p_knb, v_rows, p_vb, p_gate, p_gq, p_gk, p_gv, p_lg = _inproj(
        xp, tile, consts, seq_tiles=seq // tile, meta_rows=meta_rows)

    n_pool, page_size = cache_k.shape[1], cache_k.shape[2]
    ck = cache_k.reshape(n_pool, page_size * HEADS, 2 * DK)
    cv = cache_v.reshape(n_pool, page_size * HEADS, DV)
    o_first, o_second, os_d = _attn(
        page_table, rel_bias, lq, lk, p_qn.reshape(batch, seq, QKW), p_knb.reshape(batch, seq, QKW),
        p_vb.reshape(batch, seq, VW), m_knb, m_vb, s_qn, s_kn.reshape(ns * HEADS, 2 * DK),
        s_v.reshape(ns * HEADS, DV), ck, cv, tile, tq, ATTN_PAGES)
    o_d = (o_first, o_second)

    zero_state = jnp.zeros((1, GW, DV), F32)
    _, s_meta = _gla(zero_state, m_gq, m_gk, m_gv, m_lg, tri, 1, CHUNK, N_META)
    o_g, s_fin = _gla(s_meta, p_gq, p_gk, p_gv, p_lg, tri, batch, CHUNK, CHUNK)
    os_g, s_new = _gla(state_gla[0].reshape(nb, GW, DV), s_gq, s_gk, s_gv, s_lg, tri, nb, rpb, tq)

    half_rows = batch * seq // 2
    y_prompt = _merge(xp, [o.reshape(half_rows, VW) for o in o_d], o_g, p_gate, dnw, gnw, wtop, wbot, tile,
                      seq_tiles=seq // tile)
    y_small = _merge(xs, [os_d], os_g, s_gate, dnw, gnw, wtop, wbot, ns)

    y_prompt = y_prompt.reshape(batch, seq, D_MODEL)
    y_sample = y_small.reshape(nb, rpb, D_MODEL)[:, :tq]
    k_prompt = k_rows.reshape(1, batch, seq + N_META, HEADS, 2 * DK)
    v_prompt = v_rows.reshape(1, batch, seq + N_META, HEADS, DV)
    s_prompt = s_fin.reshape(1, batch, HEADS, DK, DV)
    k_sample = s_kn.reshape(nb, rpb, HEADS, 2 * DK)[None, :, :tq]
    v_sample = s_v.reshape(nb, rpb, HEADS, DV)[None, :, :tq]
    s_sample = s_new.reshape(1, nb, HEADS, DK, DV)
    return (y_prompt, y_sample, k_prompt, v_prompt, s_prompt, k_sample, v_sample, s_sample)
```

```python
import functools
import math

import numpy as np
import jax
import jax.numpy as jnp
from jax import lax
from jax.experimental import pallas as pl
from jax.experimental.pallas import tpu as pltpu

D_MODEL = 1024
N_META = 16
HEADS = 4
DK = 64
DV = 128
DIFF_SCALE = DK ** -0.5
GLA_GATE_RANK = 16
GLA_GATE_NORM = 16.0
NUM_BUCKETS = 32
MAX_DISTANCE = 128
RMS_EPS = 1e-6
LAM_INIT = 0.8 - 0.6 * math.exp(-0.3 * 0)
QKW = HEADS * 2 * DK
VW = HEADS * DV
GW = HEADS * DK
Z_MAIN = 3 * QKW + VW + 2 * GW + 2 * VW
LANES = 128
CHUNK = 128
NEG = -1e30
LOG2E = math.log2(math.e)
RING_SLOTS = 3
ATTN_PAGES = 16
GLA_SEQS_PER_STEP = 4
VMEM_LIMIT = 56 * 1024 * 1024

F32 = jnp.float32
BF16 = jnp.bfloat16


def _dot(a, b):
    return jnp.dot(a, b, preferred_element_type=F32)


def _dot_nt(a, b):
    return lax.dot_general(a, b, (((1,), (1,)), ((), ())), preferred_element_type=F32)


def _dot_tn(a, b):
    return lax.dot_general(a, b, (((0,), (0,)), ((), ())), preferred_element_type=F32)


def _lam_value(lamq_ref, lamk_ref):
    e = jnp.exp(jnp.sum(lamq_ref[...] * lamk_ref[...], axis=-1, keepdims=True))
    return e[0:1, :] - e[1:2, :] + LAM_INIT


def _inproj_kernel(x_ref, nw_ref, wm_ref, wa1_ref, wa2_ref, ba_ref, qw_ref, kw_ref, g_ref, *rest, seq_tiles):
    if seq_tiles is None:
        qn_ref, kn_ref, knb_ref, v_ref, vb_ref, gate_ref, gq_ref, gk_ref, gv_ref, lg_ref = rest
    else:
        (mk_ref, mv_ref, qn_ref, kn_hbm, knb_ref, v_hbm, vb_ref, gate_ref, gq_ref, gk_ref, gv_ref, lg_ref,
         stage, sem, meta_sem) = rest
    x = x_ref[...]
    ms = jnp.mean(x * x, axis=-1, keepdims=True)
    hn = (x * lax.rsqrt(ms + RMS_EPS) * nw_ref[...]).astype(BF16)

    def proj(lo, hi):
        return _dot(hn, wm_ref[:, lo:hi])

    def group_norm(z, w):
        sq = (z * z).astype(BF16)
        ss = jnp.concatenate([_dot(sq[:, c:c + 256], g_ref[...]) for c in range(0, QKW, 256)], axis=-1)
        return z * lax.rsqrt(ss * (1.0 / DK) + RMS_EPS) * w

    qn_ref[...] = group_norm(proj(0, QKW), qw_ref[...]).astype(BF16)
    kn = group_norm(proj(QKW, 2 * QKW), kw_ref[...])
    knb_ref[...] = kn.astype(BF16)
    v = proj(2 * QKW, 2 * QKW + VW)
    vb_ref[...] = v.astype(BF16)
    if seq_tiles is None:
        kn_ref[...] = kn
        v_ref[...] = v
    o = 2 * QKW + VW
    dg = proj(o, o + VW)
    gate_ref[:, 0:VW] = (dg * jax.nn.sigmoid(dg)).astype(BF16)
    o += VW
    gq_ref[...] = proj(o, o + GW) * (DK ** -0.5)
    gk_ref[...] = proj(o + GW, o + 2 * GW)
    o += 2 * GW
    gv_ref[...] = proj(o, o + VW).astype(BF16)
    o += VW
    gg = proj(o, o + VW)
    gate_ref[:, VW:2 * VW] = (gg * jax.nn.sigmoid(gg)).astype(BF16)
    ga = _dot(hn, wa1_ref[...])
    xg = _dot(ga.astype(BF16), wa2_ref[...]) + ba_ref[...]
    lg_ref[...] = (jnp.minimum(xg, 0.0) - jnp.log(1.0 + jnp.exp(-jnp.abs(xg)))) * (1.0 / GLA_GATE_NORM)
    if seq_tiles is not None:
        _write_cache_rows(kn, v, mk_ref, mv_ref, kn_hbm, v_hbm, stage, sem, meta_sem, seq_tiles)


def _write_cache_rows(kn, v, mk_ref, mv_ref, k_hbm, v_hbm, stage, sem, meta_sem, seq_tiles):
    i = pl.program_id(0)
    n = pl.num_programs(0)
    tr = kn.shape[0]
    seq_rows = (seq_tiles * tr + N_META) * HEADS

    def tile_copies(step):
        row0 = (step // seq_tiles) * seq_rows + (N_META + (step % seq_tiles) * tr) * HEADS
        dst = pl.ds(pl.multiple_of(row0, 8), tr * HEADS)
        return [pltpu.make_async_copy(stage.at[step % 2, 0], k_hbm.at[dst], sem.at[step % 2, 0]),
                pltpu.make_async_copy(stage.at[step % 2, 1], v_hbm.at[dst], sem.at[step % 2, 1])]

    @pl.when(i >= 2)
    def _():
        for c in tile_copies(i - 2):
            c.wait()

    slot = i % 2
    for h in range(HEADS):
        stage[slot, 0, pl.ds(h, tr, stride=HEADS), :] = kn[:, h * LANES:(h + 1) * LANES]
        stage[slot, 1, pl.ds(h, tr, stride=HEADS), :] = v[:, h * LANES:(h + 1) * LANES]
    for c in tile_copies(i):
        c.start()

    @pl.when(i % seq_tiles == 0)
    def _():
        dst = pl.ds(pl.multiple_of((i // seq_tiles) * seq_rows, 8), N_META * HEADS)
        meta = [pltpu.make_async_copy(mk_ref, k_hbm.at[dst], meta_sem.at[0]),
                pltpu.make_async_copy(mv_ref, v_hbm.at[dst], meta_sem.at[1])]
        for c in meta:
            c.start()
        for c in meta:
            c.wait()

    @pl.when(i == n - 1)
    def _():
        for c in tile_copies(i - 1) + tile_copies(i):
            c.wait()


def _inproj(x, tr, consts, seq_tiles=None, meta_rows=None):
    rows = x.shape[0]
    steps = rows // tr
    full = lambda a: pl.BlockSpec(a.shape, lambda i: (0,) * a.ndim)
    row = lambda w: pl.BlockSpec((tr, w), lambda i: (i, 0))
    outs = [(QKW, BF16), (QKW, F32), (QKW, BF16), (VW, F32), (VW, BF16), (2 * VW, BF16),
            (GW, F32), (GW, F32), (VW, BF16), (GW, F32)]
    out_specs = [row(w) for w, _ in outs]
    out_shape = [jax.ShapeDtypeStruct((rows, w), dt) for w, dt in outs]
    extra_in, extra_specs, scratch = [], [], []
    if seq_tiles is not None:
        assert steps % seq_tiles == 0 and steps >= 2
        cache_rows = (steps // seq_tiles) * (seq_tiles * tr + N_META) * HEADS
        for o in (1, 3):
            out_specs[o] = pl.BlockSpec(memory_space=pl.ANY)
            out_shape[o] = jax.ShapeDtypeStruct((cache_rows, LANES), F32)
        extra_in = list(meta_rows)
        extra_specs = [full(a) for a in meta_rows]
        scratch = [pltpu.VMEM((2, 2, tr * HEADS, LANES), F32), pltpu.SemaphoreType.DMA((2, 2)),
                   pltpu.SemaphoreType.DMA((2,))]
    return pl.pallas_call(
        functools.partial(_inproj_kernel, seq_tiles=seq_tiles),
        grid=(steps,),
        in_specs=[row(D_MODEL)] + [full(a) for a in consts] + extra_specs,
        out_specs=out_specs,
        out_shape=out_shape,
        scratch_shapes=scratch,
        compiler_params=pltpu.CompilerParams(dimension_semantics=("arbitrary",), vmem_limit_bytes=VMEM_LIMIT),
        name="inproj",
    )(x, *consts, *extra_in)


def _bucket_ranges():
    n = np.arange(MAX_DISTANCE)
    max_exact = NUM_BUCKETS // 2
    nf = np.maximum(n, 1).astype(np.float32)
    large = max_exact + (np.log(nf / np.float32(max_exact)) / np.float32(math.log(MAX_DISTANCE / max_exact))
                         * np.float32(NUM_BUCKETS - max_exact)).astype(np.int32)
    bucket = np.where(n < max_exact, n, np.minimum(large, NUM_BUCKETS - 1))
    return [(int(n[bucket == b].min()), int(n[bucket == b].max())) for b in range(NUM_BUCKETS)]


_BUCKET_RANGES = _bucket_ranges()


def _bias_table(n, rb_ref, h):
    far = rb_ref[NUM_BUCKETS - 1, h]
    t = jnp.zeros(n.shape, F32)
    for b, (lo, hi) in enumerate(_BUCKET_RANGES[:-1]):
        val = (rb_ref[b, h] - far) * LOG2E
        cond = (n == lo) if lo == hi else ((n >= lo) & (n <= hi))
        t = jnp.where(cond, val, t)
    return jnp.where(n < 0, NEG, t)


def _attn_kernel(pt_ref, rb_ref, lamq_ref, lamk_ref, qa_ref, qb_ref, k_ref, v_ref, km_ref, vm_ref,
                 sq_ref, skn_ref, svn_ref, ck_hbm, cv_hbm, oa_ref, ob_ref, so_ref,
                 bias_sc, mt_sc, qs_sc, m_sc, l_sc, acc_sc, *sample_scratch, tile, nq, pages, tq, groups):
    h = pl.program_id(0)
    b = pl.program_id(1)
    ip = pl.program_id(2)
    half = nq // 2
    st = (h * pl.num_programs(1) + b) * half + ip
    total_steps = pl.num_programs(0) * pl.num_programs(1) * half
    sample_begin, sample_scores, sample_softmax, sample_values, sample_end = _sample_stream(
        st, total_steps, pt_ref, rb_ref, lamq_ref, lamk_ref, sq_ref, skn_ref, svn_ref, ck_hbm, cv_hbm, so_ref,
        *sample_scratch, pages=pages, tq=tq, groups=groups)
    sample_begin()

    @pl.when((b == 0) & (ip == 0))
    def _():
        kk = lax.broadcasted_iota(jnp.int32, (tile, tile), 0)
        qq = lax.broadcasted_iota(jnp.int32, (tile, tile), 1)
        bias_sc[0] = jnp.zeros((tile, tile), F32)
        bias_sc[1] = _bias_table(qq - kk + tile, rb_ref, h)
        bias_sc[2] = _bias_table(qq - kk, rb_ref, h)
        km = lax.broadcasted_iota(jnp.int32, (N_META, tile), 0)
        qm = lax.broadcasted_iota(jnp.int32, (N_META, tile), 1)
        mt_sc[...] = _bias_table(N_META + qm - km, rb_ref, h)

    lane = lax.broadcasted_iota(jnp.int32, (1, LANES), 1)
    for t, q_ref in enumerate((qa_ref, qb_ref)):
        q = q_ref[...]
        zero = jnp.zeros_like(q)
        qs_sc[t] = jnp.concatenate([jnp.where(lane < DK, q, zero), jnp.where(lane >= DK, q, zero)], axis=0)
    m_sc[...] = jnp.full(m_sc.shape, NEG, F32)
    l_sc[...] = jnp.zeros(l_sc.shape, F32)
    acc_sc[...] = jnp.zeros(acc_sc.shape, F32)

    def plan(n):
        if n == 0:
            return 0, ip, 2
        if n == 1:
            return 1, nq - 2 - ip, 1
        if n == 2:
            return 1, nq - 1 - ip, 2
        has_sub = ip >= 1
        if n == 3:
            return jnp.where(has_sub, 0, 1), jnp.where(has_sub, ip - 1, 0), jnp.where(has_sub, 1, 0)
        j = n - 4
        short_far = jnp.maximum(ip - 1, 0)
        long_first = jnp.where(has_sub, 0, 1)
        return jnp.where(j < short_far, 0, 1), jnp.where(j < short_far, j, j - short_far + long_first), None

    def rows(key_tile):
        return pl.ds(pl.multiple_of(key_tile * tile, tile), tile)

    def scores(slot, key_tile, kind):
        s = _dot_nt(k_ref[rows(key_tile), :], qs_sc[slot])
        if kind is None:
            return s
        bias = bias_sc[kind]
        return s + jnp.concatenate([bias, bias], axis=1)

    def update(slot, s, vt):
        m_prev = m_sc[slot]
        m_new = jnp.maximum(m_prev, jnp.max(s, axis=0, keepdims=True))
        alpha = jnp.exp2(m_prev - m_new)
        p = jnp.exp2(s - m_new)
        l_sc[slot] = alpha * l_sc[slot] + jnp.sum(p, axis=0, keepdims=True)
        acc_sc[slot] = alpha * acc_sc[slot] + _dot_tn(vt, p.astype(BF16))
        m_sc[slot] = m_new

    meta_bias = jnp.where(ip == 0, mt_sc[...], 0.0)
    update(0, _dot_nt(km_ref[...], qs_sc[0]) + jnp.concatenate([meta_bias, meta_bias], axis=1), vm_ref[...])
    update(1, _dot_nt(km_ref[...], qs_sc[1]), vm_ref[...])

    plans = [plan(n) for n in range(nq + 1)]
    spread = (nq + 1) // groups
    s_next = scores(*plans[0])
    for n in range(nq + 1):
        s_cur = s_next
        group = n // spread if (n % spread == 0 and n // spread < groups) else None
        if group is not None:
            sample_s = sample_scores(group)
        if n < nq:
            s_next = scores(*plans[n + 1])
        if group is not None:
            sample_alpha, sample_p = sample_softmax(sample_s)
        update(plans[n][0], s_cur, v_ref[rows(plans[n][1]), :])
        if group is not None:
            sample_values(group, sample_alpha, sample_p)

    lam = _lam_value(lamq_ref, lamk_ref)
    for t, o_ref in enumerate((oa_ref, ob_ref)):
        acc = acc_sc[t]
        inv = 1.0 / l_sc[t]
        o_t = acc[:, 0:tile] * inv[:, 0:tile] - (lam * inv[:, tile:]) * acc[:, tile:]
        o_ref[...] = o_t.T.astype(o_ref.dtype)
    sample_end()


def _attn(page_table, rel_bias, lam_q, lam_k, qn, knb, vb, km, vm, s_qn, s_kn, s_v, cache_k, cache_v, tile, tq, pages):
    batch, seq, _ = qn.shape
    nq = seq // tile
    half = nq // 2
    assert nq % 2 == 0
    steps = HEADS * batch * half
    nb, n_pages = page_table.shape
    rpb = s_qn.shape[0] // nb
    pcols = cache_k.shape[1]
    nrow = 2 * HEADS * tq
    total_groups = nb * (n_pages // pages)
    groups = total_groups // steps
    assert groups * steps == total_groups and (n_pages // pages) % groups == 0
    assert groups <= nq + 1 and total_groups >= RING_SLOTS - 1
    steps_per_elem = (n_pages // pages) // groups
    small = lambda a: pl.BlockSpec(a.shape, lambda h, b, ip, pt: (0,) * a.ndim)
    kv = pl.BlockSpec((None, seq, LANES), lambda h, b, ip, pt: (b, 0, h))
    meta = pl.BlockSpec((N_META, LANES), lambda h, b, ip, pt: (0, h))
    short = lambda h, b, ip, pt: (b, ip, h)
    elem = lambda h, b, ip, pt: (((h * batch + b) * half + ip) // steps_per_elem, 0)
    grid_spec = pltpu.PrefetchScalarGridSpec(
        num_scalar_prefetch=1,
        grid=(HEADS, batch, half),
        in_specs=[pl.BlockSpec(memory_space=pltpu.SMEM), small(lam_q), small(lam_k),
                  pl.BlockSpec((None, tile, LANES), short),
                  pl.BlockSpec((None, tile, LANES), lambda h, b, ip, pt: (b, nq - 1 - ip, h)),
                  kv, kv, meta, meta,
                  pl.BlockSpec((rpb, QKW), elem), pl.BlockSpec((rpb * HEADS, LANES), elem),
                  pl.BlockSpec((rpb * HEADS, LANES), elem),
                  pl.BlockSpec(memory_space=pl.ANY), pl.BlockSpec(memory_space=pl.ANY)],
        out_specs=[pl.BlockSpec((None, tile, LANES), short),
                   pl.BlockSpec((None, tile, LANES), lambda h, b, ip, pt: (b, half - 1 - ip, h)),
                   pl.BlockSpec((rpb, VW), elem)],
        scratch_shapes=[pltpu.VMEM((3, tile, tile), F32), pltpu.VMEM((N_META, tile), F32),
                        pltpu.VMEM((2, 2 * tile, LANES), BF16),
                        pltpu.VMEM((2, 1, 2 * tile), F32), pltpu.VMEM((2, 1, 2 * tile), F32),
                        pltpu.VMEM((2, LANES, 2 * tile), F32),
                        pltpu.VMEM((RING_SLOTS, pages, pcols, LANES), F32),
                        pltpu.VMEM((RING_SLOTS, pages, pcols, LANES), F32),
                        pltpu.SemaphoreType.DMA((RING_SLOTS, 2, pages)),
                        pltpu.VMEM((nrow, LANES), BF16), pltpu.VMEM((nrow, pcols), F32), pltpu.VMEM((nrow, pcols), F32),
                        pltpu.VMEM((nrow, LANES), F32), pltpu.VMEM((nrow, 1), F32), pltpu.VMEM((nrow, 1), F32),
                        pltpu.VMEM((nrow, LANES), F32)],
    )
    return pl.pallas_call(
        functools.partial(_attn_kernel, tile=tile, nq=nq, pages=pages, tq=tq, groups=groups),
        grid_spec=grid_spec,
        out_shape=[jax.ShapeDtypeStruct((batch, seq // 2, VW), BF16)] * 2
                  + [jax.ShapeDtypeStruct((nb * rpb, VW), F32)],
        compiler_params=pltpu.CompilerParams(dimension_semantics=("arbitrary",) * 3, vmem_limit_bytes=VMEM_LIMIT),
        name="attn",
    )(page_table, rel_bias, lam_q, lam_k, qn, qn, knb, vb, km, vm, s_qn, s_kn, s_v, cache_k, cache_v)


def _sample_stream(st, total_steps, pt_ref, rb_ref, lamq_ref, lamk_ref, q_ref, kn_ref, vn_ref, ck_hbm, cv_hbm, o_ref,
                   kbuf, vbuf, sem, w_sc, mask_sc, ptab_sc, ntab_sc, m_sc, l_sc, acc_sc, *, pages, tq, groups):
    nrow = 2 * HEADS * tq
    pcols = kbuf.shape[2]
    page_size = pcols // HEADS
    ng = pt_ref.shape[1] // pages
    steps_per_elem = ng // groups
    first = st % steps_per_elem == 0
    final = st % steps_per_elem == steps_per_elem - 1
    total = total_steps * groups

    def page_copies(gi):
        slot = gi % RING_SLOTS
        src = jnp.minimum(gi, total - 1)
        out = []
        for u in range(pages):
            page = pt_ref[src // ng, (src % ng) * pages + u]
            out.append(pltpu.make_async_copy(ck_hbm.at[page], kbuf.at[slot, u], sem.at[slot, 0, u]))
            out.append(pltpu.make_async_copy(cv_hbm.at[page], vbuf.at[slot, u], sem.at[slot, 1, u]))
        return out

    def build_tables():
        def tables(cols, offset):
            r = lax.broadcasted_iota(jnp.int32, (nrow, cols), 0)
            c = lax.broadcasted_iota(jnp.int32, (nrow, cols), 1)
            tok = r // (2 * HEADS)
            head = (r % (2 * HEADS)) // 2
            n = offset + tok - c // HEADS
            t = jnp.zeros((nrow, cols), F32)
            for h in range(HEADS):
                t = jnp.where(head == h, _bias_table(n, rb_ref, h), t)
            return jnp.where(head == c % HEADS, t, NEG), c // HEADS

        mask_sc[...] = jnp.where(tables(pcols, MAX_DISTANCE)[0] > 0.5 * NEG, 0.0, NEG)
        ptab_sc[...] = tables(pcols, page_size)[0]
        nt, slot = tables(LANES, 0)
        ntab_sc[...] = jnp.where(slot < tq, nt, NEG)

    def update(s, pv):
        m_prev = m_sc[...]
        m_new = jnp.maximum(m_prev, jnp.max(s, axis=-1, keepdims=True))
        alpha = jnp.exp2(m_prev - m_new)
        p = jnp.exp2(s - m_new)
        l_sc[...] = alpha * l_sc[...] + jnp.sum(p, axis=-1, keepdims=True)
        acc_sc[...] = alpha * acc_sc[...] + pv(p.astype(BF16))
        m_sc[...] = m_new

    def start_element():
        q = q_ref[...].astype(F32)
        r8 = lax.broadcasted_iota(jnp.int32, (2 * HEADS, LANES), 0)
        lane = lax.broadcasted_iota(jnp.int32, (2 * HEADS, LANES), 1)
        blocks = []
        for t in range(tq):
            blk = jnp.zeros((2 * HEADS, LANES), F32)
            for h in range(HEADS):
                blk = jnp.where(r8 // 2 == h, jnp.broadcast_to(q[t:t + 1, h * LANES:(h + 1) * LANES], blk.shape), blk)
            blocks.append(jnp.where(lane // DK == r8 % 2, blk, 0.0))
        w_sc[...] = jnp.concatenate(blocks, axis=0).astype(BF16)
        m_sc[...] = jnp.full(m_sc.shape, NEG, F32)
        l_sc[...] = jnp.zeros(l_sc.shape, F32)
        acc_sc[...] = jnp.zeros(acc_sc.shape, F32)
        pad = jnp.zeros((LANES - kn_ref.shape[0], LANES), F32)
        kn = jnp.concatenate([kn_ref[...], pad], axis=0).astype(BF16)
        vn = jnp.concatenate([vn_ref[...], pad], axis=0).astype(BF16)
        s = _dot_nt(w_sc[...], kn) + ntab_sc[...]
        update(s, lambda p: _dot(p, vn))

    def begin():
        @pl.when(st == 0)
        def _():
            for ahead in range(RING_SLOTS - 1):
                for c in page_copies(ahead):
                    c.start()
            build_tables()

        pl.when(first)(start_element)

    def scores(n):
        gi = st * groups + n
        for c in page_copies(gi + RING_SLOTS - 1):
            c.start()
        for c in page_copies(gi):
            c.wait()
        slot = gi % RING_SLOTS
        w = w_sc[...]
        parts = []
        for u in range(pages):
            s = _dot_nt(w, kbuf[slot, u].astype(BF16))
            if u == pages - 1 and n == groups - 1:
                s = s + jnp.where(final, ptab_sc[...], mask_sc[...])
            else:
                s = s + mask_sc[...]
            parts.append(s)
        return jnp.concatenate(parts, axis=-1)

    def softmax(s):
        m_prev = m_sc[...]
        m_new = jnp.maximum(m_prev, jnp.max(s, axis=-1, keepdims=True))
        alpha = jnp.exp2(m_prev - m_new)
        p = jnp.exp2(s - m_new)
        l_sc[...] = alpha * l_sc[...] + jnp.sum(p, axis=-1, keepdims=True)
        m_sc[...] = m_new
        return alpha, p.astype(BF16)

    def values(n, alpha, p):
        slot = (st * groups + n) % RING_SLOTS
        acc = _dot(p[:, 0:pcols], vbuf[slot, 0].astype(BF16))
        for u in range(1, pages):
            acc += _dot(p[:, u * pcols:(u + 1) * pcols], vbuf[slot, u].astype(BF16))
        acc_sc[...] = alpha * acc_sc[...] + acc

    def end():
        pl.when(final)(finish_element)

        @pl.when(st == total_steps - 1)
        def _():
            for extra in range(RING_SLOTS - 1):
                for c in page_copies(total + extra):
                    c.wait()

    def finish_element():
        lam = _lam_value(lamq_ref, lamk_ref)
        accn = acc_sc[...] * (1.0 / l_sc[...])
        o_ref[...] = jnp.zeros(o_ref.shape, F32)
        for t in range(tq):
            for h in range(HEADS):
                r = t * 2 * HEADS + 2 * h
                o_ref[t:t + 1, h * DV:(h + 1) * DV] = accn[r:r + 1, :] - lam * accn[r + 1:r + 2, :]

    return begin, scores, softmax, values, end


def _gla_kernel(s0_ref, q_ref, k_ref, v_ref, lg_ref, tri_ref, o_ref, sout_ref, s_sc, *, valid):
    c = pl.program_id(1)
    nseq = q_ref.shape[0]

    @pl.when(c == 0)
    def _():
        for i in range(nseq):
            s_sc[i] = s0_ref[0 if s0_ref.shape[0] != nseq else i]

    rows_in = q_ref.shape[1]

    def padded(a):
        if rows_in == CHUNK:
            return a
        return jnp.concatenate([a, jnp.zeros((CHUNK - rows_in, a.shape[1]), a.dtype)], axis=0)

    wide = lambda ref: jnp.concatenate([padded(ref[i]) for i in range(nseq)], axis=1)
    q = wide(q_ref)
    k = wide(k_ref)
    lg = wide(lg_ref)
    vb = wide(v_ref)
    if valid < CHUNK:
        live = lax.broadcasted_iota(jnp.int32, (CHUNK, 1), 0) < valid
        k = jnp.where(live, k, 0.0)
        lg = jnp.where(live, lg, 0.0)
        vb = jnp.where(live, vb, jnp.zeros_like(vb))

    lg_hi = lg.astype(BF16)
    lg_lo = (lg - lg_hi.astype(F32)).astype(BF16)
    tri = tri_ref[...]
    b = _dot(tri, lg_hi) + _dot(tri, lg_lo)
    b_mid = b[CHUNK // 2 - 1:CHUNK // 2, :]
    qs = q * jnp.exp(b)
    qt = q * jnp.exp(b - b_mid)
    kt = (k * jnp.exp(b_mid - b)).astype(BF16)
    b_t = b.T
    b_last = b_t[:, CHUNK - 1:CHUNK]
    kl_t = (k.T * jnp.exp(b_last - b_t)).astype(BF16)
    decay = jnp.exp(b_last)

    row = lax.broadcasted_iota(jnp.int32, (CHUNK, CHUNK), 0)
    col = lax.broadcasted_iota(jnp.int32, (CHUNK, CHUNK), 1)
    causal = row >= col
    lane = lax.broadcasted_iota(jnp.int32, (1, LANES), 1)
    heads = [(i, h) for i in range(nseq) for h in range(HEADS)]
    cols = lambda i, h: slice(i * GW + (h // 2) * LANES, i * GW + (h // 2 + 1) * LANES)
    mine = lambda h: (lane // DK) == (h % 2)
    v_of = lambda i, h: vb[:, i * VW + h * DV:i * VW + (h + 1) * DV]
    s_old = [s_sc[i] for i in range(nseq)]
    s_bf = [s.astype(BF16) for s in s_old]
    a = [_dot_nt(jnp.where(mine(h), qt[:, cols(i, h)], 0.0).astype(BF16), kt[:, cols(i, h)]) for i, h in heads]
    inter = [_dot(jnp.where(mine(h), qs[:, cols(i, h)], 0.0).astype(BF16),
                  s_bf[i][(h // 2) * LANES:(h // 2 + 1) * LANES, :]) for i, h in heads]
    upd = [_dot(kl_t[i * GW + h * DK:i * GW + (h + 1) * DK, :], v_of(i, h)) for i, h in heads]
    for n, (i, h) in enumerate(heads):
        o_h = inter[n] + _dot(jnp.where(causal, a[n], 0.0).astype(BF16), v_of(i, h))
        o_ref[i, :, h * DV:(h + 1) * DV] = o_h[0:rows_in].astype(o_ref.dtype)
        rows = slice(i * GW + h * DK, i * GW + (h + 1) * DK)
        s_sc[i, h * DK:(h + 1) * DK, :] = decay[rows, :] * s_old[i][h * DK:(h + 1) * DK, :] + upd[n]

    @pl.when(c == pl.num_programs(1) - 1)
    def _():
        sout_ref[...] = s_sc[...]


def _gla(s0, gq, gk, gv, lg, tri, nb, rows_in, valid):
    nc = gq.shape[0] // (nb * rows_in)
    per = min(nb, GLA_SEQS_PER_STEP)
    assert nb % per == 0
    s0_blk = per if s0.shape[0] == nb else 1
    s0_map = (lambda b, c: (b, 0, 0)) if s0.shape[0] == nb else (lambda b, c: (0, 0, 0))
    seqs = lambda a: a.reshape(nb, nc * rows_in, a.shape[-1])
    rowblk = lambda w: pl.BlockSpec((per, rows_in, w), lambda b, c: (b, c, 0))
    o, s_fin = pl.pallas_call(
        functools.partial(_gla_kernel, valid=valid),
        grid=(nb // per, nc),
        in_specs=[pl.BlockSpec((s0_blk, GW, DV), s0_map), rowblk(GW), rowblk(GW), rowblk(VW), rowblk(GW),
                  pl.BlockSpec(tri.shape, lambda b, c: (0, 0))],
        out_specs=[rowblk(VW), pl.BlockSpec((per, GW, DV), lambda b, c: (b, 0, 0))],
        out_shape=[jax.ShapeDtypeStruct((nb, nc * rows_in, VW), BF16), jax.ShapeDtypeStruct((nb, GW, DV), F32)],
        scratch_shapes=[pltpu.VMEM((per, GW, DV), F32)],
        compiler_params=pltpu.CompilerParams(dimension_semantics=("arbitrary", "arbitrary"),
                                             vmem_limit_bytes=VMEM_LIMIT),
        name="gla",
    )(s0, seqs(gq), seqs(gk), seqs(gv), seqs(lg), tri)
    return o.reshape(nb * nc * rows_in, VW), s_fin


def _merge_kernel(x_ref, *rest, seq_tiles):
    if seq_tiles is None:
        od_ref, og_ref, gate_ref, dnw_ref, gnw_ref, wtop_ref, wbot_ref, y_ref = rest
        od = od_ref[...].astype(F32)
    else:
        oda_ref, odb_ref, og_ref, gate_ref, dnw_ref, gnw_ref, wtop_ref, wbot_ref, y_ref = rest
        od = jnp.where(pl.program_id(0) % seq_tiles < seq_tiles // 2, oda_ref[...], odb_ref[...]).astype(F32)

    def head_norm(o, w):
        parts = []
        for h in range(HEADS):
            sl = o[:, h * DV:(h + 1) * DV]
            parts.append(sl * lax.rsqrt(jnp.mean(sl * sl, axis=-1, keepdims=True) + RMS_EPS))
        return jnp.concatenate(parts, axis=-1) * w

    gate = gate_ref[...].astype(F32)
    mix_d = (head_norm(od, dnw_ref[...]) * gate[:, 0:VW]).astype(BF16)
    mix_g = (head_norm(og_ref[...].astype(F32), gnw_ref[...]) * gate[:, VW:]).astype(BF16)
    y_ref[...] = x_ref[...] + _dot(mix_d, wtop_ref[...]) + _dot(mix_g, wbot_ref[...])


def _merge(x, od_parts, og, gates, dnw, gnw, wtop, wbot, tr, seq_tiles=None):
    rows = x.shape[0]
    full = lambda a: pl.BlockSpec(a.shape, lambda i: (0,) * a.ndim)
    row = lambda w: pl.BlockSpec((tr, w), lambda i: (i, 0))
    if seq_tiles is None:
        od_specs = [row(VW)]
    else:
        half = seq_tiles // 2
        od_specs = [pl.BlockSpec((tr, VW), lambda i: ((i // seq_tiles) * half + jnp.minimum(i % seq_tiles, half - 1), 0)),
                    pl.BlockSpec((tr, VW), lambda i: ((i // seq_tiles) * half + jnp.maximum(i % seq_tiles - half, 0), 0))]
    return pl.pallas_call(
        functools.partial(_merge_kernel, seq_tiles=seq_tiles),
        grid=(rows // tr,),
        in_specs=[row(D_MODEL)] + od_specs + [row(VW), row(2 * VW), full(dnw), full(gnw), full(wtop), full(wbot)],
        out_specs=row(D_MODEL),
        out_shape=jax.ShapeDtypeStruct((rows, D_MODEL), F32),
        compiler_params=pltpu.CompilerParams(dimension_semantics=("arbitrary",), vmem_limit_bytes=VMEM_LIMIT),
        name="merge",
    )(x, *od_parts, og, gates, dnw, gnw, wtop, wbot)


def kernel(x_prompt, x_sample, cache_k, cache_v, state_gla, page_table, meta_tokens, rel_bias, norm_w, w_in,
           q_norm_w, k_norm_w, lam_q, lam_k, diff_norm_w, gla_wa2, gla_ba, gla_norm_w, w_out):
    batch, seq, _ = x_prompt.shape
    nb, tq, _ = x_sample.shape
    tile = 512
    rpb = 16

    w = w_in[0]
    wm = w[:, :Z_MAIN].astype(BF16)
    wa1 = jnp.pad(w[:, Z_MAIN:], ((0, 0), (0, LANES - GLA_GATE_RANK))).astype(BF16)
    wa2 = jnp.pad(gla_wa2[0], ((0, LANES - GLA_GATE_RANK), (0, 0))).astype(BF16)
    ba = gla_ba[0][None]
    nw = norm_w[0][None]
    qw = jnp.tile(q_norm_w[0].reshape(-1), HEADS)[None] * (DIFF_SCALE * LOG2E)
    kw = jnp.tile(k_norm_w[0].reshape(-1), HEADS)[None]
    grp = np.arange(256) // DK
    gmat = jnp.asarray(grp[:, None] == grp[None, :], BF16)
    consts = (nw, wm, wa1, wa2, ba, qw, kw, gmat)
    dnw = jnp.tile(diff_norm_w[0], HEADS)[None] * (1.0 - LAM_INIT)
    gnw = jnp.tile(gla_norm_w[0], HEADS)[None]
    wtop = w_out[0][:VW].astype(BF16)
    wbot = w_out[0][VW:].astype(BF16)
    t_idx = np.arange(CHUNK)
    tri = jnp.asarray(t_idx[:, None] >= t_idx[None, :], BF16)
    lq, lk = lam_q[0], lam_k[0]

    xs = jnp.pad(x_sample, ((0, 0), (0, rpb - tq), (0, 0))).reshape(nb * rpb, D_MODEL)
    xm = jnp.pad(meta_tokens, ((0, CHUNK - N_META), (0, 0)))
    x_small = jnp.concatenate([xs, xm], axis=0)
    ns = nb * rpb
    small = _inproj(x_small, x_small.shape[0], consts)
    s_qn, s_kn, s_knb, s_v, s_vb, s_gate, s_gq, s_gk, s_gv, s_lg = [a[:ns] for a in small]
    m_qn, m_kn, m_knb, m_v, m_vb, m_gate, m_gq, m_gk, m_gv, m_lg = [a[ns:] for a in small]

    xp = x_prompt.reshape(batch * seq, D_MODEL)
    meta_rows = (m_kn[:N_META].reshape(N_META * HEADS, 2 * DK), m_v[:N_META].reshape(N_META * HEADS, DV))
    p_qn, k_rows, p_knb, v_rows, p_vb, p_gate, p_gq, p_gk, p_gv, p_lg = _inproj(
        xp, tile, consts, seq_tiles=seq // tile, meta_rows=meta_rows)

    n_pool, page_size = cache_k.shape[1], cache_k.shape[2]
    ck = cache_k.reshape(n_pool, page_size * HEADS, 2 * DK)
    cv = cache_v.reshape(n_pool, page_size * HEADS, DV)
    o_first, o_second, os_d = _attn(
        page_table, rel_bias, lq, lk, p_qn.reshape(batch, seq, QKW), p_knb.reshape(batch, seq, QKW),
        p_vb.reshape(batch, seq, VW), m_knb, m_vb, s_qn, s_kn.reshape(ns * HEADS, 2 * DK),
        s_v.reshape(ns * HEADS, DV), ck, cv, tile, tq, ATTN_PAGES)
    o_d = (o_first, o_second)

    zero_state = jnp.zeros((1, GW, DV), F32)
    _, s_meta = _gla(zero_state, m_gq, m_gk, m_gv, m_lg, tri, 1, CHUNK, N_META)
    o_g, s_fin = _gla(s_meta, p_gq, p_gk, p_gv, p_lg, tri, batch, CHUNK, CHUNK)
    os_g, s_new = _gla(state_gla[0].reshape(nb, GW, DV), s_gq, s_gk, s_gv, s_lg, tri, nb, rpb, tq)

    half_rows = batch * seq // 2
    y_prompt = _merge(xp, [o.reshape(half_rows, VW) for o in o_d], o_g, p_gate, dnw, gnw, wtop, wbot, tile,
                      seq_tiles=seq // tile)
    y_small = _merge(xs, [os_d], os_g, s_gate, dnw, gnw, wtop, wbot, ns)

    y_prompt = y_prompt.reshape(batch, seq, D_MODEL)
    y_sample = y_small.reshape(nb, rpb, D_MODEL)[:, :tq]
    k_prompt = k_rows.reshape(1, batch, seq + N_META, HEADS, 2 * DK)
    v_prompt = v_rows.reshape(1, batch, seq + N_META, HEADS, DV)
    s_prompt = s_fin.reshape(1, batch, HEADS, DK, DV)
    k_sample = s_kn.reshape(nb, rpb, HEADS, 2 * DK)[None, :, :tq]
    v_sample = s_v.reshape(nb, rpb, HEADS, DV)[None, :, :tq]
    s_sample = s_new.reshape(1, nb, HEADS, DK, DV)
    return (y_prompt, y_sample, k_prompt, v_prompt, s_prompt, k_sample, v_sample, s_sample)
```

```python
import functools
import math

import numpy as np
import jax
import jax.numpy as jnp
from jax import lax
from jax.experimental import pallas as pl
from jax.experimental.pallas import tpu as pltpu

D_MODEL = 1024
N_META = 16
HEADS = 4
DK = 64
DV = 128
DIFF_SCALE = DK ** -0.5
GLA_GATE_RANK = 16
GLA_GATE_NORM = 16.0
NUM_BUCKETS = 32
MAX_DISTANCE = 128
RMS_EPS = 1e-6
LAM_INIT = 0.8 - 0.6 * math.exp(-0.3 * 0)
QKW = HEADS * 2 * DK
VW = HEADS * DV
GW = HEADS * DK
Z_MAIN = 3 * QKW + VW + 2 * GW + 2 * VW
LANES = 128
CHUNK = 128
NEG = -1e30
LOG2E = math.log2(math.e)
RING_SLOTS = 3
ATTN_PAGES = 16
MERGE_ROWS = 1024
GLA_CHUNKS_PER_STEP = 2
GLA_SEQS_PER_STEP = 4
VMEM_LIMIT = 56 * 1024 * 1024

F32 = jnp.float32
BF16 = jnp.bfloat16


def _dot(a, b):
    return jnp.dot(a, b, preferred_element_type=F32)


def _dot_nt(a, b):
    return lax.dot_general(a, b, (((1,), (1,)), ((), ())), preferred_element_type=F32)


def _dot_tn(a, b):
    return lax.dot_general(a, b, (((0,), (0,)), ((), ())), preferred_element_type=F32)


def _lam_value(lamq_ref, lamk_ref):
    e = jnp.exp(jnp.sum(lamq_ref[...] * lamk_ref[...], axis=-1, keepdims=True))
    return e[0:1, :] - e[1:2, :] + LAM_INIT


def _inproj_kernel(x_ref, nw_ref, wm_ref, wa1_ref, wa2_ref, ba_ref, qw_ref, kw_ref, g_ref, *rest, seq_tiles):
    if seq_tiles is None:
        qn_ref, kn_ref, knb_ref, v_ref, vb_ref, gate_ref, gq_ref, gk_ref, gv_ref, lg_ref = rest
    else:
        (mk_ref, mv_ref, qn_ref, kn_hbm, knb_ref, v_hbm, vb_ref, gate_ref, gq_ref, gk_ref, gv_ref, lg_ref,
         stage, sem, meta_sem) = rest
    x = x_ref[...]
    ms = jnp.mean(x * x, axis=-1, keepdims=True)
    hn = (x * lax.rsqrt(ms + RMS_EPS) * nw_ref[...]).astype(BF16)

    def proj(lo, hi):
        return _dot(hn, wm_ref[:, lo:hi])

    def group_norm(z, w):
        sq = (z * z).astype(BF16)
        ss = jnp.concatenate([_dot(sq[:, c:c + 256], g_ref[...]) for c in range(0, QKW, 256)], axis=-1)
        return z * lax.rsqrt(ss * (1.0 / DK) + RMS_EPS) * w

    qn_ref[...] = group_norm(proj(0, QKW), qw_ref[...]).astype(BF16)
    kn = group_norm(proj(QKW, 2 * QKW), kw_ref[...])
    knb_ref[...] = kn.astype(BF16)
    v = proj(2 * QKW, 2 * QKW + VW)
    vb_ref[...] = v.astype(BF16)
    if seq_tiles is None:
        kn_ref[...] = kn
        v_ref[...] = v
    o = 2 * QKW + VW
    dg = proj(o, o + VW)
    gate_ref[:, 0:VW] = (dg * jax.nn.sigmoid(dg)).astype(BF16)
    o += VW
    gq_ref[...] = proj(o, o + GW) * (DK ** -0.5)
    gk_ref[...] = proj(o + GW, o + 2 * GW)
    o += 2 * GW
    gv_ref[...] = proj(o, o + VW).astype(BF16)
    o += VW
    gg = proj(o, o + VW)
    gate_ref[:, VW:2 * VW] = (gg * jax.nn.sigmoid(gg)).astype(BF16)
    ga = _dot(hn, wa1_ref[...])
    xg = _dot(ga.astype(BF16), wa2_ref[...]) + ba_ref[...]
    lg_ref[...] = (jnp.minimum(xg, 0.0) - jnp.log(1.0 + jnp.exp(-jnp.abs(xg)))) * (1.0 / GLA_GATE_NORM)
    if seq_tiles is not None:
        _write_cache_rows(kn, v, mk_ref, mv_ref, kn_hbm, v_hbm, stage, sem, meta_sem, seq_tiles)


def _write_cache_rows(kn, v, mk_ref, mv_ref, k_hbm, v_hbm, stage, sem, meta_sem, seq_tiles):
    i = pl.program_id(0)
    n = pl.num_programs(0)
    tr = kn.shape[0]
    seq_rows = (seq_tiles * tr + N_META) * HEADS

    def tile_copies(step):
        row0 = (step // seq_tiles) * seq_rows + (N_META + (step % seq_tiles) * tr) * HEADS
        dst = pl.ds(pl.multiple_of(row0, 8), tr * HEADS)
        return [pltpu.make_async_copy(stage.at[step % 2, 0], k_hbm.at[dst], sem.at[step % 2, 0]),
                pltpu.make_async_copy(stage.at[step % 2, 1], v_hbm.at[dst], sem.at[step % 2, 1])]

    @pl.when(i >= 2)
    def _():
        for c in tile_copies(i - 2):
            c.wait()

    slot = i % 2
    for h in range(HEADS):
        stage[slot, 0, pl.ds(h, tr, stride=HEADS), :] = kn[:, h * LANES:(h + 1) * LANES]
        stage[slot, 1, pl.ds(h, tr, stride=HEADS), :] = v[:, h * LANES:(h + 1) * LANES]
    for c in tile_copies(i):
        c.start()

    @pl.when(i % seq_tiles == 0)
    def _():
        dst = pl.ds(pl.multiple_of((i // seq_tiles) * seq_rows, 8), N_META * HEADS)
        meta = [pltpu.make_async_copy(mk_ref, k_hbm.at[dst], meta_sem.at[0]),
                pltpu.make_async_copy(mv_ref, v_hbm.at[dst], meta_sem.at[1])]
        for c in meta:
            c.start()
        for c in meta:
            c.wait()

    @pl.when(i == n - 1)
    def _():
        for c in tile_copies(i - 1) + tile_copies(i):
            c.wait()


def _inproj(x, tr, consts, seq_tiles=None, meta_rows=None):
    rows = x.shape[0]
    steps = rows // tr
    full = lambda a: pl.BlockSpec(a.shape, lambda i: (0,) * a.ndim)
    row = lambda w: pl.BlockSpec((tr, w), lambda i: (i, 0))
    outs = [(QKW, BF16), (QKW, F32), (QKW, BF16), (VW, F32), (VW, BF16), (2 * VW, BF16),
            (GW, F32), (GW, F32), (VW, BF16), (GW, F32)]
    out_specs = [row(w) for w, _ in outs]
    out_shape = [jax.ShapeDtypeStruct((rows, w), dt) for w, dt in outs]
    extra_in, extra_specs, scratch = [], [], []
    if seq_tiles is not None:
        assert steps % seq_tiles == 0 and steps >= 2
        cache_rows = (steps // seq_tiles) * (seq_tiles * tr + N_META) * HEADS
        for o in (1, 3):
            out_specs[o] = pl.BlockSpec(memory_space=pl.ANY)
            out_shape[o] = jax.ShapeDtypeStruct((cache_rows, LANES), F32)
        extra_in = list(meta_rows)
        extra_specs = [full(a) for a in meta_rows]
        scratch = [pltpu.VMEM((2, 2, tr * HEADS, LANES), F32), pltpu.SemaphoreType.DMA((2, 2)),
                   pltpu.SemaphoreType.DMA((2,))]
    return pl.pallas_call(
        functools.partial(_inproj_kernel, seq_tiles=seq_tiles),
        grid=(steps,),
        in_specs=[row(D_MODEL)] + [full(a) for a in consts] + extra_specs,
        out_specs=out_specs,
        out_shape=out_shape,
        scratch_shapes=scratch,
        compiler_params=pltpu.CompilerParams(dimension_semantics=("arbitrary",), vmem_limit_bytes=VMEM_LIMIT),
        name="inproj",
    )(x, *consts, *extra_in)


def _bucket_ranges():
    n = np.arange(MAX_DISTANCE)
    max_exact = NUM_BUCKETS // 2
    nf = np.maximum(n, 1).astype(np.float32)
    large = max_exact + (np.log(nf / np.float32(max_exact)) / np.float32(math.log(MAX_DISTANCE / max_exact))
                         * np.float32(NUM_BUCKETS - max_exact)).astype(np.int32)
    bucket = np.where(n < max_exact, n, np.minimum(large, NUM_BUCKETS - 1))
    return [(int(n[bucket == b].min()), int(n[bucket == b].max())) for b in range(NUM_BUCKETS)]


_BUCKET_RANGES = _bucket_ranges()


def _bias_table(n, rb_ref, h):
    far = rb_ref[NUM_BUCKETS - 1, h]
    t = jnp.zeros(n.shape, F32)
    for b, (lo, hi) in enumerate(_BUCKET_RANGES[:-1]):
        val = (rb_ref[b, h] - far) * LOG2E
        cond = (n == lo) if lo == hi else ((n >= lo) & (n <= hi))
        t = jnp.where(cond, val, t)
    return jnp.where(n < 0, NEG, t)


def _attn_kernel(pt_ref, rb_ref, lamq_ref, lamk_ref, qa_ref, qb_ref, k_ref, v_ref, km_ref, vm_ref,
                 sq_ref, skn_ref, svn_ref, ck_hbm, cv_hbm, oa_ref, ob_ref, so_ref,
                 bias_sc, mt_sc, qs_sc, m_sc, l_sc, acc_sc, *sample_scratch, tile, nq, pages, tq, groups):
    h = pl.program_id(0)
    b = pl.program_id(1)
    ip = pl.program_id(2)
    half = nq // 2
    st = (h * pl.num_programs(1) + b) * half + ip
    total_steps = pl.num_programs(0) * pl.num_programs(1) * half
    sample_begin, sample_scores, sample_softmax, sample_values, sample_end = _sample_stream(
        st, total_steps, pt_ref, rb_ref, lamq_ref, lamk_ref, sq_ref, skn_ref, svn_ref, ck_hbm, cv_hbm, so_ref,
        *sample_scratch, pages=pages, tq=tq, groups=groups)
    sample_begin()

    @pl.when((b == 0) & (ip == 0))
    def _():
        kk = lax.broadcasted_iota(jnp.int32, (tile, tile), 0)
        qq = lax.broadcasted_iota(jnp.int32, (tile, tile), 1)
        bias_sc[0] = jnp.zeros((tile, tile), F32)
        bias_sc[1] = _bias_table(qq - kk + tile, rb_ref, h)
        bias_sc[2] = _bias_table(qq - kk, rb_ref, h)
        km = lax.broadcasted_iota(jnp.int32, (N_META, tile), 0)
        qm = lax.broadcasted_iota(jnp.int32, (N_META, tile), 1)
        mt_sc[...] = _bias_table(N_META + qm - km, rb_ref, h)

    lane = lax.broadcasted_iota(jnp.int32, (1, LANES), 1)
    for t, q_ref in enumerate((qa_ref, qb_ref)):
        q = q_ref[...]
        zero = jnp.zeros_like(q)
        qs_sc[t] = jnp.concatenate([jnp.where(lane < DK, q, zero), jnp.where(lane >= DK, q, zero)], axis=0)
    m_sc[...] = jnp.full(m_sc.shape, NEG, F32)
    l_sc[...] = jnp.zeros(l_sc.shape, F32)
    acc_sc[...] = jnp.zeros(acc_sc.shape, F32)

    def plan(n):
        if n == 0:
            return 0, ip, 2
        if n == 1:
            return 1, nq - 2 - ip, 1
        if n == 2:
            return 1, nq - 1 - ip, 2
        has_sub = ip >= 1
        if n == 3:
            return jnp.where(has_sub, 0, 1), jnp.where(has_sub, ip - 1, 0), jnp.where(has_sub, 1, 0)
        j = n - 4
        short_far = jnp.maximum(ip - 1, 0)
        long_first = jnp.where(has_sub, 0, 1)
        return jnp.where(j < short_far, 0, 1), jnp.where(j < short_far, j, j - short_far + long_first), None

    def rows(key_tile):
        return pl.ds(pl.multiple_of(key_tile * tile, tile), tile)

    def scores(slot, key_tile, kind):
        s = _dot_nt(k_ref[rows(key_tile), :], qs_sc[slot])
        if kind is None:
            return s
        bias = bias_sc[kind]
        return s + jnp.concatenate([bias, bias], axis=1)

    def update(slot, s, vt):
        m_prev = m_sc[slot]
        m_new = jnp.maximum(m_prev, jnp.max(s, axis=0, keepdims=True))
        alpha = jnp.exp2(m_prev - m_new)
        p = jnp.exp2(s - m_new)
        l_sc[slot] = alpha * l_sc[slot] + jnp.sum(p, axis=0, keepdims=True)
        acc_sc[slot] = alpha * acc_sc[slot] + _dot_tn(vt, p.astype(BF16))
        m_sc[slot] = m_new

    meta_bias = jnp.where(ip == 0, mt_sc[...], 0.0)
    update(0, _dot_nt(km_ref[...], qs_sc[0]) + jnp.concatenate([meta_bias, meta_bias], axis=1), vm_ref[...])
    update(1, _dot_nt(km_ref[...], qs_sc[1]), vm_ref[...])

    plans = [plan(n) for n in range(nq + 1)]
    spread = (nq + 1) // groups
    s_next = scores(*plans[0])
    for n in range(nq + 1):
        s_cur = s_next
        group = n // spread if (n % spread == 0 and n // spread < groups) else None
        if group is not None:
            sample_s = sample_scores(group)
        if n < nq:
            s_next = scores(*plans[n + 1])
        if group is not None:
            sample_alpha, sample_p = sample_softmax(sample_s)
        update(plans[n][0], s_cur, v_ref[rows(plans[n][1]), :])
        if group is not None:
            sample_values(group, sample_alpha, sample_p)

    lam = _lam_value(lamq_ref, lamk_ref)
    for t, o_ref in enumerate((oa_ref, ob_ref)):
        acc = acc_sc[t]
        inv = 1.0 / l_sc[t]
        o_t = acc[:, 0:tile] * inv[:, 0:tile] - (lam * inv[:, tile:]) * acc[:, tile:]
        o_ref[...] = o_t.T.astype(o_ref.dtype)
    sample_end()


def _attn(page_table, rel_bias, lam_q, lam_k, qn, knb, vb, km, vm, s_qn, s_kn, s_v, cache_k, cache_v, tile, tq, pages):
    batch, seq, _ = qn.shape
    nq = seq // tile
    half = nq // 2
    assert nq % 2 == 0
    steps = HEADS * batch * half
    nb, n_pages = page_table.shape
    rpb = s_qn.shape[0] // nb
    pcols = cache_k.shape[1]
    nrow = 2 * HEADS * tq
    total_groups = nb * (n_pages // pages)
    groups = total_groups // steps
    assert groups * steps == total_groups and (n_pages // pages) % groups == 0
    assert groups <= nq + 1 and total_groups >= RING_SLOTS - 1
    steps_per_elem = (n_pages // pages) // groups
    small = lambda a: pl.BlockSpec(a.shape, lambda h, b, ip, pt: (0,) * a.ndim)
    kv = pl.BlockSpec((None, seq, LANES), lambda h, b, ip, pt: (b, 0, h))
    meta = pl.BlockSpec((N_META, LANES), lambda h, b, ip, pt: (0, h))
    short = lambda h, b, ip, pt: (b, ip, h)
    elem = lambda h, b, ip, pt: (((h * batch + b) * half + ip) // steps_per_elem, 0)
    grid_spec = pltpu.PrefetchScalarGridSpec(
        num_scalar_prefetch=1,
        grid=(HEADS, batch, half),
        in_specs=[pl.BlockSpec(memory_space=pltpu.SMEM), small(lam_q), small(lam_k),
                  pl.BlockSpec((None, tile, LANES), short),
                  pl.BlockSpec((None, tile, LANES), lambda h, b, ip, pt: (b, nq - 1 - ip, h)),
                  kv, kv, meta, meta,
                  pl.BlockSpec((rpb, QKW), elem), pl.BlockSpec((rpb * HEADS, LANES), elem),
                  pl.BlockSpec((rpb * HEADS, LANES), elem),
                  pl.BlockSpec(memory_space=pl.ANY), pl.BlockSpec(memory_space=pl.ANY)],
        out_specs=[pl.BlockSpec((None, tile, LANES), short),
                   pl.BlockSpec((None, tile, LANES), lambda h, b, ip, pt: (b, half - 1 - ip, h)),
                   pl.BlockSpec((rpb, VW), elem)],
        scratch_shapes=[pltpu.VMEM((3, tile, tile), F32), pltpu.VMEM((N_META, tile), F32),
                        pltpu.VMEM((2, 2 * tile, LANES), BF16),
                        pltpu.VMEM((2, 1, 2 * tile), F32), pltpu.VMEM((2, 1, 2 * tile), F32),
                        pltpu.VMEM((2, LANES, 2 * tile), F32),
                        pltpu.VMEM((RING_SLOTS, pages, pcols, LANES), F32),
                        pltpu.VMEM((RING_SLOTS, pages, pcols, LANES), F32),
                        pltpu.SemaphoreType.DMA((RING_SLOTS, 2, pages)),
                        pltpu.VMEM((nrow, LANES), BF16), pltpu.VMEM((nrow, pcols), F32), pltpu.VMEM((nrow, pcols), F32),
                        pltpu.VMEM((nrow, LANES), F32), pltpu.VMEM((nrow, 1), F32), pltpu.VMEM((nrow, 1), F32),
                        pltpu.VMEM((nrow, LANES), F32)],
    )
    return pl.pallas_call(
        functools.partial(_attn_kernel, tile=tile, nq=nq, pages=pages, tq=tq, groups=groups),
        grid_spec=grid_spec,
        out_shape=[jax.ShapeDtypeStruct((batch, seq // 2, VW), BF16)] * 2
                  + [jax.ShapeDtypeStruct((nb * rpb, VW), F32)],
        compiler_params=pltpu.CompilerParams(dimension_semantics=("arbitrary",) * 3, vmem_limit_bytes=VMEM_LIMIT),
        name="attn",
    )(page_table, rel_bias, lam_q, lam_k, qn, qn, knb, vb, km, vm, s_qn, s_kn, s_v, cache_k, cache_v)


def _sample_stream(st, total_steps, pt_ref, rb_ref, lamq_ref, lamk_ref, q_ref, kn_ref, vn_ref, ck_hbm, cv_hbm, o_ref,
                   kbuf, vbuf, sem, w_sc, mask_sc, ptab_sc, ntab_sc, m_sc, l_sc, acc_sc, *, pages, tq, groups):
    nrow = 2 * HEADS * tq
    pcols = kbuf.shape[2]
    page_size = pcols // HEADS
    ng = pt_ref.shape[1] // pages
    steps_per_elem = ng // groups
    first = st % steps_per_elem == 0
    final = st % steps_per_elem == steps_per_elem - 1
    total = total_steps * groups

    def page_copies(gi):
        slot = gi % RING_SLOTS
        src = jnp.minimum(gi, total - 1)
        out = []
        for u in range(pages):
            page = pt_ref[src // ng, (src % ng) * pages + u]
            out.append(pltpu.make_async_copy(ck_hbm.at[page], kbuf.at[slot, u], sem.at[slot, 0, u]))
            out.append(pltpu.make_async_copy(cv_hbm.at[page], vbuf.at[slot, u], sem.at[slot, 1, u]))
        return out

    def build_tables():
        def tables(cols, offset):
            r = lax.broadcasted_iota(jnp.int32, (nrow, cols), 0)
            c = lax.broadcasted_iota(jnp.int32, (nrow, cols), 1)
            tok = r // (2 * HEADS)
            head = (r % (2 * HEADS)) // 2
            n = offset + tok - c // HEADS
            t = jnp.zeros((nrow, cols), F32)
            for h in range(HEADS):
                t = jnp.where(head == h, _bias_table(n, rb_ref, h), t)
            return jnp.where(head == c % HEADS, t, NEG), c // HEADS

        mask_sc[...] = jnp.where(tables(pcols, MAX_DISTANCE)[0] > 0.5 * NEG, 0.0, NEG)
        ptab_sc[...] = tables(pcols, page_size)[0]
        nt, slot = tables(LANES, 0)
        ntab_sc[...] = jnp.where(slot < tq, nt, NEG)

    def update(s, pv):
        m_prev = m_sc[...]
        m_new = jnp.maximum(m_prev, jnp.max(s, axis=-1, keepdims=True))
        alpha = jnp.exp2(m_prev - m_new)
        p = jnp.exp2(s - m_new)
        l_sc[...] = alpha * l_sc[...] + jnp.sum(p, axis=-1, keepdims=True)
        acc_sc[...] = alpha * acc_sc[...] + pv(p.astype(BF16))
        m_sc[...] = m_new

    def start_element():
        q = q_ref[...].astype(F32)
        r8 = lax.broadcasted_iota(jnp.int32, (2 * HEADS, LANES), 0)
        lane = lax.broadcasted_iota(jnp.int32, (2 * HEADS, LANES), 1)
        blocks = []
        for t in range(tq):
            blk = jnp.zeros((2 * HEADS, LANES), F32)
            for h in range(HEADS):
                blk = jnp.where(r8 // 2 == h, jnp.broadcast_to(q[t:t + 1, h * LANES:(h + 1) * LANES], blk.shape), blk)
            blocks.append(jnp.where(lane // DK == r8 % 2, blk, 0.0))
        w_sc[...] = jnp.concatenate(blocks, axis=0).astype(BF16)
        m_sc[...] = jnp.full(m_sc.shape, NEG, F32)
        l_sc[...] = jnp.zeros(l_sc.shape, F32)
        acc_sc[...] = jnp.zeros(acc_sc.shape, F32)
        pad = jnp.zeros((LANES - kn_ref.shape[0], LANES), F32)
        kn = jnp.concatenate([kn_ref[...], pad], axis=0).astype(BF16)
        vn = jnp.concatenate([vn_ref[...], pad], axis=0).astype(BF16)
        s = _dot_nt(w_sc[...], kn) + ntab_sc[...]
        update(s, lambda p: _dot(p, vn))

    def begin():
        @pl.when(st == 0)
        def _():
            for ahead in range(RING_SLOTS - 1):
                for c in page_copies(ahead):
                    c.start()
            build_tables()

        pl.when(first)(start_element)

    def scores(n):
        gi = st * groups + n
        for c in page_copies(gi + RING_SLOTS - 1):
            c.start()
        for c in page_copies(gi):
            c.wait()
        slot = gi % RING_SLOTS
        w = w_sc[...]
        parts = []
        for u in range(pages):
            s = _dot_nt(w, kbuf[slot, u].astype(BF16))
            if u == pages - 1 and n == groups - 1:
                s = s + jnp.where(final, ptab_sc[...], mask_sc[...])
            else:
                s = s + mask_sc[...]
            parts.append(s)
        return jnp.concatenate(parts, axis=-1)

    def softmax(s):
        m_prev = m_sc[...]
        m_new = jnp.maximum(m_prev, jnp.max(s, axis=-1, keepdims=True))
        alpha = jnp.exp2(m_prev - m_new)
        p = jnp.exp2(s - m_new)
        l_sc[...] = alpha * l_sc[...] + jnp.sum(p, axis=-1, keepdims=True)
        m_sc[...] = m_new
        return alpha, p.astype(BF16)

    def values(n, alpha, p):
        slot = (st * groups + n) % RING_SLOTS
        acc = _dot(p[:, 0:pcols], vbuf[slot, 0].astype(BF16))
        for u in range(1, pages):
            acc += _dot(p[:, u * pcols:(u + 1) * pcols], vbuf[slot, u].astype(BF16))
        acc_sc[...] = alpha * acc_sc[...] + acc

    def end():
        pl.when(final)(finish_element)

        @pl.when(st == total_steps - 1)
        def _():
            for extra in range(RING_SLOTS - 1):
                for c in page_copies(total + extra):
                    c.wait()

    def finish_element():
        lam = _lam_value(lamq_ref, lamk_ref)
        accn = acc_sc[...] * (1.0 / l_sc[...])
        o_ref[...] = jnp.zeros(o_ref.shape, F32)
        for t in range(tq):
            for h in range(HEADS):
                r = t * 2 * HEADS + 2 * h
                o_ref[t:t + 1, h * DV:(h + 1) * DV] = accn[r:r + 1, :] - lam * accn[r + 1:r + 2, :]

    return begin, scores, softmax, values, end


def _gla_kernel(s0_ref, q_ref, k_ref, v_ref, lg_ref, tri_ref, o_ref, sout_ref, s_sc, *, valid):
    c = pl.program_id(1)
    nseq = q_ref.shape[0]

    @pl.when(c == 0)
    def _():
        for i in range(nseq):
            s_sc[i] = s0_ref[0 if s0_ref.shape[0] != nseq else i]

    rows_in = min(q_ref.shape[1], CHUNK)
    for r0 in range(0, q_ref.shape[1], CHUNK):
        _gla_chunk(q_ref, k_ref, v_ref, lg_ref, tri_ref, o_ref, s_sc, r0, rows_in, valid)

    @pl.when(c == pl.num_programs(1) - 1)
    def _():
        sout_ref[...] = s_sc[...]


def _gla_chunk(q_ref, k_ref, v_ref, lg_ref, tri_ref, o_ref, s_sc, r0, rows_in, valid):
    nseq = q_ref.shape[0]

    def padded(a):
        if rows_in == CHUNK:
            return a
        return jnp.concatenate([a, jnp.zeros((CHUNK - rows_in, a.shape[1]), a.dtype)], axis=0)

    wide = lambda ref: jnp.concatenate([padded(ref[i, r0:r0 + rows_in, :]) for i in range(nseq)], axis=1)
    q = wide(q_ref)
    k = wide(k_ref)
    lg = wide(lg_ref)
    vb = wide(v_ref)
    if valid < CHUNK:
        live = lax.broadcasted_iota(jnp.int32, (CHUNK, 1), 0) < valid
        k = jnp.where(live, k, 0.0)
        lg = jnp.where(live, lg, 0.0)
        vb = jnp.where(live, vb, jnp.zeros_like(vb))

    lg_hi = lg.astype(BF16)
    lg_lo = (lg - lg_hi.astype(F32)).astype(BF16)
    tri = tri_ref[...]
    b = _dot(tri, lg_hi) + _dot(tri, lg_lo)
    b_mid = b[CHUNK // 2 - 1:CHUNK // 2, :]
    qs = q * jnp.exp(b)
    qt = q * jnp.exp(b - b_mid)
    kt = (k * jnp.exp(b_mid - b)).astype(BF16)
    b_t = b.T
    b_last = b_t[:, CHUNK - 1:CHUNK]
    kl_t = (k.T * jnp.exp(b_last - b_t)).astype(BF16)
    decay = jnp.exp(b_last)

    row = lax.broadcasted_iota(jnp.int32, (CHUNK, CHUNK), 0)
    col = lax.broadcasted_iota(jnp.int32, (CHUNK, CHUNK), 1)
    causal = row >= col
    lane = lax.broadcasted_iota(jnp.int32, (1, LANES), 1)
    heads = [(i, h) for i in range(nseq) for h in range(HEADS)]
    cols = lambda i, h: slice(i * GW + (h // 2) * LANES, i * GW + (h // 2 + 1) * LANES)
    mine = lambda h: (lane // DK) == (h % 2)
    v_of = lambda i, h: vb[:, i * VW + h * DV:i * VW + (h + 1) * DV]
    s_old = [s_sc[i] for i in range(nseq)]
    s_bf = [s.astype(BF16) for s in s_old]
    a = [_dot_nt(jnp.where(mine(h), qt[:, cols(i, h)], 0.0).astype(BF16), kt[:, cols(i, h)]) for i, h in heads]
    inter = [_dot(jnp.where(mine(h), qs[:, cols(i, h)], 0.0).astype(BF16),
                  s_bf[i][(h // 2) * LANES:(h // 2 + 1) * LANES, :]) for i, h in heads]
    upd = [_dot(kl_t[i * GW + h * DK:i * GW + (h + 1) * DK, :], v_of(i, h)) for i, h in heads]
    for n, (i, h) in enumerate(heads):
        o_h = inter[n] + _dot(jnp.where(causal, a[n], 0.0).astype(BF16), v_of(i, h))
        o_ref[i, r0:r0 + rows_in, h * DV:(h + 1) * DV] = o_h[0:rows_in].astype(o_ref.dtype)
        rows = slice(i * GW + h * DK, i * GW + (h + 1) * DK)
        s_sc[i, h * DK:(h + 1) * DK, :] = decay[rows, :] * s_old[i][h * DK:(h + 1) * DK, :] + upd[n]


def _gla(s0, gq, gk, gv, lg, tri, nb, rows_in, valid):
    nc = gq.shape[0] // (nb * rows_in)
    per = min(nb, GLA_SEQS_PER_STEP)
    cps = GLA_CHUNKS_PER_STEP if nc % GLA_CHUNKS_PER_STEP == 0 else 1
    assert nb % per == 0 and (cps == 1 or rows_in == CHUNK)
    s0_blk = per if s0.shape[0] == nb else 1
    s0_map = (lambda b, c: (b, 0, 0)) if s0.shape[0] == nb else (lambda b, c: (0, 0, 0))
    seqs = lambda a: a.reshape(nb, nc * rows_in, a.shape[-1])
    rowblk = lambda w: pl.BlockSpec((per, rows_in * cps, w), lambda b, c: (b, c, 0))
    o, s_fin = pl.pallas_call(
        functools.partial(_gla_kernel, valid=valid),
        grid=(nb // per, nc // cps),
        in_specs=[pl.BlockSpec((s0_blk, GW, DV), s0_map), rowblk(GW), rowblk(GW), rowblk(VW), rowblk(GW),
                  pl.BlockSpec(tri.shape, lambda b, c: (0, 0))],
        out_specs=[rowblk(VW), pl.BlockSpec((per, GW, DV), lambda b, c: (b, 0, 0))],
        out_shape=[jax.ShapeDtypeStruct((nb, nc * rows_in, VW), BF16), jax.ShapeDtypeStruct((nb, GW, DV), F32)],
        scratch_shapes=[pltpu.VMEM((per, GW, DV), F32)],
        compiler_params=pltpu.CompilerParams(dimension_semantics=("arbitrary", "arbitrary"),
                                             vmem_limit_bytes=VMEM_LIMIT),
        name="gla",
    )(s0, seqs(gq), seqs(gk), seqs(gv), seqs(lg), tri)
    return o.reshape(nb * nc * rows_in, VW), s_fin


def _merge_kernel(x_ref, *rest, seq_tiles):
    if seq_tiles is None:
        od_ref, og_ref, gate_ref, dnw_ref, gnw_ref, wtop_ref, wbot_ref, y_ref = rest
        od = od_ref[...].astype(F32)
    else:
        oda_ref, odb_ref, og_ref, gate_ref, dnw_ref, gnw_ref, wtop_ref, wbot_ref, y_ref = rest
        od = jnp.where(pl.program_id(0) % seq_tiles < seq_tiles // 2, oda_ref[...], odb_ref[...]).astype(F32)

    def head_norm(o, w):
        parts = []
        for h in range(HEADS):
            sl = o[:, h * DV:(h + 1) * DV]
            parts.append(sl * lax.rsqrt(jnp.mean(sl * sl, axis=-1, keepdims=True) + RMS_EPS))
        return jnp.concatenate(parts, axis=-1) * w

    gate = gate_ref[...].astype(F32)
    mix_d = (head_norm(od, dnw_ref[...]) * gate[:, 0:VW]).astype(BF16)
    mix_g = (head_norm(og_ref[...].astype(F32), gnw_ref[...]) * gate[:, VW:]).astype(BF16)
    y_ref[...] = x_ref[...] + _dot(mix_d, wtop_ref[...]) + _dot(mix_g, wbot_ref[...])


def _merge(x, od_parts, og, gates, dnw, gnw, wtop, wbot, tr, seq_tiles=None):
    rows = x.shape[0]
    full = lambda a: pl.BlockSpec(a.shape, lambda i: (0,) * a.ndim)
    row = lambda w: pl.BlockSpec((tr, w), lambda i: (i, 0))
    if seq_tiles is None:
        od_specs = [row(VW)]
    else:
        assert seq_tiles % 2 == 0
        half = seq_tiles // 2
        od_specs = [pl.BlockSpec((tr, VW), lambda i: ((i // seq_tiles) * half + jnp.minimum(i % seq_tiles, half - 1), 0)),
                    pl.BlockSpec((tr, VW), lambda i: ((i // seq_tiles) * half + jnp.maximum(i % seq_tiles - half, 0), 0))]
    return pl.pallas_call(
        functools.partial(_merge_kernel, seq_tiles=seq_tiles),
        grid=(rows // tr,),
        in_specs=[row(D_MODEL)] + od_specs + [row(VW), row(2 * VW), full(dnw), full(gnw), full(wtop), full(wbot)],
        out_specs=row(D_MODEL),
        out_shape=jax.ShapeDtypeStruct((rows, D_MODEL), F32),
        compiler_params=pltpu.CompilerParams(dimension_semantics=("arbitrary",), vmem_limit_bytes=VMEM_LIMIT),
        name="merge",
    )(x, *od_parts, og, gates, dnw, gnw, wtop, wbot)


def kernel(x_prompt, x_sample, cache_k, cache_v, state_gla, page_table, meta_tokens, rel_bias, norm_w, w_in,
           q_norm_w, k_norm_w, lam_q, lam_k, diff_norm_w, gla_wa2, gla_ba, gla_norm_w, w_out):
    batch, seq, _ = x_prompt.shape
    nb, tq, _ = x_sample.shape
    tile = 512
    rpb = 16

    w = w_in[0]
    wm = w.astype(BF16)
    wa1 = jnp.pad(w[:, Z_MAIN:], ((0, 0), (0, LANES - GLA_GATE_RANK))).astype(BF16)
    wa2 = jnp.pad(gla_wa2[0], ((0, LANES - GLA_GATE_RANK), (0, 0))).astype(BF16)
    ba = gla_ba[0][None]
    nw = norm_w[0][None]
    qw = jnp.tile(q_norm_w[0].reshape(-1), HEADS)[None] * (DIFF_SCALE * LOG2E)
    kw = jnp.tile(k_norm_w[0].reshape(-1), HEADS)[None]
    grp = np.arange(256) // DK
    gmat = jnp.asarray(grp[:, None] == grp[None, :], BF16)
    consts = (nw, wm, wa1, wa2, ba, qw, kw, gmat)
    dnw = jnp.tile(diff_norm_w[0], HEADS)[None] * (1.0 - LAM_INIT)
    gnw = jnp.tile(gla_norm_w[0], HEADS)[None]
    wtop = w_out[0][:VW].astype(BF16)
    wbot = w_out[0][VW:].astype(BF16)
    t_idx = np.arange(CHUNK)
    tri = jnp.asarray(t_idx[:, None] >= t_idx[None, :], BF16)
    lq, lk = lam_q[0], lam_k[0]

    xs = jnp.pad(x_sample, ((0, 0), (0, rpb - tq), (0, 0))).reshape(nb * rpb, D_MODEL)
    xm = jnp.pad(meta_tokens, ((0, CHUNK - N_META), (0, 0)))
    x_small = jnp.concatenate([xs, xm], axis=0)
    ns = nb * rpb
    small = _inproj(x_small, x_small.shape[0], consts)
    s_qn, s_kn, s_knb, s_v, s_vb, s_gate, s_gq, s_gk, s_gv, s_lg = [a[:ns] for a in small]
    m_qn, m_kn, m_knb, m_v, m_vb, m_gate, m_gq, m_gk, m_gv, m_lg = [a[ns:] for a in small]

    xp = x_prompt.reshape(batch * seq, D_MODEL)
    meta_rows = (m_kn[:N_META].reshape(N_META * HEADS, 2 * DK), m_v[:N_META].reshape(N_META * HEADS, DV))
    p_qn, k_rows, p_knb, v_rows, p_vb, p_gate, p_gq, p_gk, p_gv, p_lg = _inproj(
        xp, tile, consts, seq_tiles=seq // tile, meta_rows=meta_rows)

    n_pool, page_size = cache_k.shape[1], cache_k.shape[2]
    ck = cache_k.reshape(n_pool, page_size * HEADS, 2 * DK)
    cv = cache_v.reshape(n_pool, page_size * HEADS, DV)
    o_first, o_second, os_d = _attn(
        page_table, rel_bias, lq, lk, p_qn.reshape(batch, seq, QKW), p_knb.reshape(batch, seq, QKW),
        p_vb.reshape(batch, seq, VW), m_knb, m_vb, s_qn, s_kn.reshape(ns * HEADS, 2 * DK),
        s_v.reshape(ns * HEADS, DV), ck, cv, tile, tq, ATTN_PAGES)
    o_d = (o_first, o_second)

    zero_state = jnp.zeros((1, GW, DV), F32)
    _, s_meta = _gla(zero_state, m_gq, m_gk, m_gv, m_lg, tri, 1, CHUNK, N_META)
    o_g, s_fin = _gla(s_meta, p_gq, p_gk, p_gv, p_lg, tri, batch, CHUNK, CHUNK)
    os_g, s_new = _gla(state_gla[0].reshape(nb, GW, DV), s_gq, s_gk, s_gv, s_lg, tri, nb, rpb, tq)

    half_rows = batch * seq // 2
    y_prompt = _merge(xp, [o.reshape(half_rows, VW) for o in o_d], o_g, p_gate, dnw, gnw, wtop, wbot, MERGE_ROWS,
                      seq_tiles=seq // MERGE_ROWS)
    y_small = _merge(xs, [os_d], os_g, s_gate, dnw, gnw, wtop, wbot, ns)

    y_prompt = y_prompt.reshape(batch, seq, D_MODEL)
    y_sample = y_small.reshape(nb, rpb, D_MODEL)[:, :tq]
    k_prompt = k_rows.reshape(1, batch, seq + N_META, HEADS, 2 * DK)
    v_prompt = v_rows.reshape(1, batch, seq + N_META, HEADS, DV)
    s_prompt = s_fin.reshape(1, batch, HEADS, DK, DV)
    k_sample = s_kn.reshape(nb, rpb, HEADS, 2 * DK)[None, :, :tq]
    v_sample = s_v.reshape(nb, rpb, HEADS, DV)[None, :, :tq]
    s_sample = s_new.reshape(1, nb, HEADS, DK, DV)
    return (y_prompt, y_sample, k_prompt, v_prompt, s_prompt, k_sample, v_sample, s_sample)
```

```python
import functools
import math

import numpy as np
import jax
import jax.numpy as jnp
from jax import lax
from jax.experimental import pallas as pl
from jax.experimental.pallas import tpu as pltpu

D_MODEL = 1024
N_META = 16
HEADS = 4
DK = 64
DV = 128
DIFF_SCALE = DK ** -0.5
GLA_GATE_RANK = 16
GLA_GATE_NORM = 16.0
NUM_BUCKETS = 32
MAX_DISTANCE = 128
RMS_EPS = 1e-6
LAM_INIT = 0.8 - 0.6 * math.exp(-0.3 * 0)
QKW = HEADS * 2 * DK
VW = HEADS * DV
GW = HEADS * DK
Z_MAIN = 3 * QKW + VW + 2 * GW + 2 * VW
LANES = 128
CHUNK = 128
NEG = -1e30
LOG2E = math.log2(math.e)
RING_SLOTS = 3
ATTN_PAGES = 16
MERGE_ROWS = 1024
GLA_CHUNKS_PER_STEP = 2
GLA_SEQS_PER_STEP = 4
VMEM_LIMIT = 56 * 1024 * 1024

F32 = jnp.float32
BF16 = jnp.bfloat16


def _dot(a, b):
    return jnp.dot(a, b, preferred_element_type=F32)


def _dot_nt(a, b):
    return lax.dot_general(a, b, (((1,), (1,)), ((), ())), preferred_element_type=F32)


def _dot_tn(a, b):
    return lax.dot_general(a, b, (((0,), (0,)), ((), ())), preferred_element_type=F32)


def _lam_value(lamq_ref, lamk_ref):
    e = jnp.exp(jnp.sum(lamq_ref[...] * lamk_ref[...], axis=-1, keepdims=True))
    return e[0:1, :] - e[1:2, :] + LAM_INIT


def _inproj_kernel(x_ref, nw_ref, wm_ref, wa1_ref, wa2_ref, ba_ref, qw_ref, kw_ref, g_ref, *rest, seq_tiles):
    if seq_tiles is None:
        qn_ref, kn_ref, knb_ref, v_ref, vb_ref, gate_ref, gq_ref, gk_ref, gv_ref, lg_ref = rest
    else:
        (mk_ref, mv_ref, qn_ref, kn_hbm, knb_ref, v_hbm, vb_ref, gate_ref, gq_ref, gk_ref, gv_ref, lg_ref,
         stage, sem, meta_sem) = rest
    x = x_ref[...]
    ms = jnp.mean(x * x, axis=-1, keepdims=True)
    hn = (x * lax.rsqrt(ms + RMS_EPS) * nw_ref[...]).astype(BF16)

    def proj(lo, hi):
        return _dot(hn, wm_ref[:, lo:hi])

    def group_norm(z, w):
        sq = (z * z).astype(BF16)
        ss = jnp.concatenate([_dot(sq[:, c:c + 256], g_ref[...]) for c in range(0, QKW, 256)], axis=-1)
        return z * lax.rsqrt(ss * (1.0 / DK) + RMS_EPS) * w

    qn_ref[...] = group_norm(proj(0, QKW), qw_ref[...]).astype(BF16)
    kn = group_norm(proj(QKW, 2 * QKW), kw_ref[...])
    knb_ref[...] = kn.astype(BF16)
    v = proj(2 * QKW, 2 * QKW + VW)
    vb_ref[...] = v.astype(BF16)
    if seq_tiles is None:
        kn_ref[...] = kn
        v_ref[...] = v
    o = 2 * QKW + VW
    dg = proj(o, o + VW)
    gate_ref[:, 0:VW] = (dg * jax.nn.sigmoid(dg)).astype(BF16)
    o += VW
    gq_ref[...] = proj(o, o + GW) * (DK ** -0.5)
    gk_ref[...] = proj(o + GW, o + 2 * GW)
    o += 2 * GW
    gv_ref[...] = proj(o, o + VW).astype(BF16)
    o += VW
    gg = proj(o, o + VW)
    gate_ref[:, VW:2 * VW] = (gg * jax.nn.sigmoid(gg)).astype(BF16)
    ga = _dot(hn, wa1_ref[...])
    xg = _dot(ga.astype(BF16), wa2_ref[...]) + ba_ref[...]
    lg_ref[...] = (jnp.minimum(xg, 0.0) - jnp.log(1.0 + jnp.exp(-jnp.abs(xg)))) * (1.0 / GLA_GATE_NORM)
    if seq_tiles is not None:
        _write_cache_rows(kn, v, mk_ref, mv_ref, kn_hbm, v_hbm, stage, sem, meta_sem, seq_tiles)


def _write_cache_rows(kn, v, mk_ref, mv_ref, k_hbm, v_hbm, stage, sem, meta_sem, seq_tiles):
    i = pl.program_id(0)
    n = pl.num_programs(0)
    tr = kn.shape[0]
    seq_rows = (seq_tiles * tr + N_META) * HEADS

    def tile_copies(step):
        row0 = (step // seq_tiles) * seq_rows + (N_META + (step % seq_tiles) * tr) * HEADS
        dst = pl.ds(pl.multiple_of(row0, 8), tr * HEADS)
        return [pltpu.make_async_copy(stage.at[step % 2, 0], k_hbm.at[dst], sem.at[step % 2, 0]),
                pltpu.make_async_copy(stage.at[step % 2, 1], v_hbm.at[dst], sem.at[step % 2, 1])]

    @pl.when(i >= 2)
    def _():
        for c in tile_copies(i - 2):
            c.wait()

    slot = i % 2
    for h in range(HEADS):
        stage[slot, 0, pl.ds(h, tr, stride=HEADS), :] = kn[:, h * LANES:(h + 1) * LANES]
        stage[slot, 1, pl.ds(h, tr, stride=HEADS), :] = v[:, h * LANES:(h + 1) * LANES]
    for c in tile_copies(i):
        c.start()

    @pl.when(i % seq_tiles == 0)
    def _():
        dst = pl.ds(pl.multiple_of((i // seq_tiles) * seq_rows, 8), N_META * HEADS)
        meta = [pltpu.make_async_copy(mk_ref, k_hbm.at[dst], meta_sem.at[0]),
                pltpu.make_async_copy(mv_ref, v_hbm.at[dst], meta_sem.at[1])]
        for c in meta:
            c.start()
        for c in meta:
            c.wait()

    @pl.when(i == n - 1)
    def _():
        for c in tile_copies(i - 1) + tile_copies(i):
            c.wait()


def _inproj(x, tr, consts, seq_tiles=None, meta_rows=None):
    rows = x.shape[0]
    steps = rows // tr
    full = lambda a: pl.BlockSpec(a.shape, lambda i: (0,) * a.ndim)
    row = lambda w: pl.BlockSpec((tr, w), lambda i: (i, 0))
    outs = [(QKW, BF16), (QKW, F32), (QKW, BF16), (VW, F32), (VW, BF16), (2 * VW, BF16),
            (GW, F32), (GW, F32), (VW, BF16), (GW, F32)]
    out_specs = [row(w) for w, _ in outs]
    out_shape = [jax.ShapeDtypeStruct((rows, w), dt) for w, dt in outs]
    extra_in, extra_specs, scratch = [], [], []
    if seq_tiles is not None:
        assert steps % seq_tiles == 0 and steps >= 2
        cache_rows = (steps // seq_tiles) * (seq_tiles * tr + N_META) * HEADS
        for o in (1, 3):
            out_specs[o] = pl.BlockSpec(memory_space=pl.ANY)
            out_shape[o] = jax.ShapeDtypeStruct((cache_rows, LANES), F32)
        extra_in = list(meta_rows)
        extra_specs = [full(a) for a in meta_rows]
        scratch = [pltpu.VMEM((2, 2, tr * HEADS, LANES), F32), pltpu.SemaphoreType.DMA((2, 2)),
                   pltpu.SemaphoreType.DMA((2,))]
    return pl.pallas_call(
        functools.partial(_inproj_kernel, seq_tiles=seq_tiles),
        grid=(steps,),
        in_specs=[row(D_MODEL)] + [full(a) for a in consts] + extra_specs,
        out_specs=out_specs,
        out_shape=out_shape,
        scratch_shapes=scratch,
        compiler_params=pltpu.CompilerParams(dimension_semantics=("arbitrary",), vmem_limit_bytes=VMEM_LIMIT),
        name="inproj",
    )(x, *consts, *extra_in)


def _bucket_ranges():
    n = np.arange(MAX_DISTANCE)
    max_exact = NUM_BUCKETS // 2
    nf = np.maximum(n, 1).astype(np.float32)
    large = max_exact + (np.log(nf / np.float32(max_exact)) / np.float32(math.log(MAX_DISTANCE / max_exact))
                         * np.float32(NUM_BUCKETS - max_exact)).astype(np.int32)
    bucket = np.where(n < max_exact, n, np.minimum(large, NUM_BUCKETS - 1))
    return [(int(n[bucket == b].min()), int(n[bucket == b].max())) for b in range(NUM_BUCKETS)]


_BUCKET_RANGES = _bucket_ranges()


def _bias_table(n, rb_ref, h):
    far = rb_ref[NUM_BUCKETS - 1, h]
    t = jnp.zeros(n.shape, F32)
    for b, (lo, hi) in enumerate(_BUCKET_RANGES[:-1]):
        val = (rb_ref[b, h] - far) * LOG2E
        cond = (n == lo) if lo == hi else ((n >= lo) & (n <= hi))
        t = jnp.where(cond, val, t)
    return jnp.where(n < 0, NEG, t)


def _attn_kernel(pt_ref, rb_ref, lamq_ref, lamk_ref, qa_ref, qb_ref, k_ref, v_ref, km_ref, vm_ref,
                 sq_ref, skn_ref, svn_ref, ck_hbm, cv_hbm, oa_ref, ob_ref, so_ref,
                 bias_sc, mt_sc, qs_sc, m_sc, l_sc, acc_sc, *sample_scratch, tile, nq, pages, tq, groups):
    h = pl.program_id(0)
    b = pl.program_id(1)
    ip = pl.program_id(2)
    half = nq // 2
    st = (h * pl.num_programs(1) + b) * half + ip
    total_steps = pl.num_programs(0) * pl.num_programs(1) * half
    sample_begin, sample_scores, sample_softmax, sample_values, sample_end = _sample_stream(
        st, total_steps, pt_ref, rb_ref, lamq_ref, lamk_ref, sq_ref, skn_ref, svn_ref, ck_hbm, cv_hbm, so_ref,
        *sample_scratch, pages=pages, tq=tq, groups=groups)
    sample_begin()

    @pl.when((b == 0) & (ip == 0))
    def _():
        kk = lax.broadcasted_iota(jnp.int32, (tile, tile), 0)
        qq = lax.broadcasted_iota(jnp.int32, (tile, tile), 1)
        bias_sc[0] = jnp.zeros((tile, tile), F32)
        bias_sc[1] = _bias_table(qq - kk + tile, rb_ref, h)
        bias_sc[2] = _bias_table(qq - kk, rb_ref, h)
        km = lax.broadcasted_iota(jnp.int32, (N_META, tile), 0)
        qm = lax.broadcasted_iota(jnp.int32, (N_META, tile), 1)
        mt_sc[...] = _bias_table(N_META + qm - km, rb_ref, h)

    lane = lax.broadcasted_iota(jnp.int32, (1, LANES), 1)
    for t, q_ref in enumerate((qa_ref, qb_ref)):
        q = q_ref[...]
        zero = jnp.zeros_like(q)
        qs_sc[t] = jnp.concatenate([jnp.where(lane < DK, q, zero), jnp.where(lane >= DK, q, zero)], axis=0)
    m_sc[...] = jnp.full(m_sc.shape, NEG, F32)
    l_sc[...] = jnp.zeros(l_sc.shape, F32)
    acc_sc[...] = jnp.zeros(acc_sc.shape, F32)

    def plan(n):
        if n == 0:
            return 0, ip, 2
        if n == 1:
            return 1, nq - 2 - ip, 1
        if n == 2:
            return 1, nq - 1 - ip, 2
        has_sub = ip >= 1
        if n == 3:
            return jnp.where(has_sub, 0, 1), jnp.where(has_sub, ip - 1, 0), jnp.where(has_sub, 1, 0)
        j = n - 4
        short_far = jnp.maximum(ip - 1, 0)
        long_first = jnp.where(has_sub, 0, 1)
        return jnp.where(j < short_far, 0, 1), jnp.where(j < short_far, j, j - short_far + long_first), None

    def rows(key_tile):
        return pl.ds(pl.multiple_of(key_tile * tile, tile), tile)

    def scores(slot, key_tile, kind):
        s = _dot_nt(k_ref[rows(key_tile), :], qs_sc[slot])
        if kind is None:
            return s
        bias = bias_sc[kind]
        return s + jnp.concatenate([bias, bias], axis=1)

    def update(slot, s, vt):
        m_prev = m_sc[slot]
        m_new = jnp.maximum(m_prev, jnp.max(s, axis=0, keepdims=True))
        alpha = jnp.exp2(m_prev - m_new)
        p = jnp.exp2(s - m_new)
        l_sc[slot] = alpha * l_sc[slot] + jnp.sum(p, axis=0, keepdims=True)
        acc_sc[slot] = alpha * acc_sc[slot] + _dot_tn(vt, p.astype(BF16))
        m_sc[slot] = m_new

    meta_bias = jnp.where(ip == 0, mt_sc[...], 0.0)
    update(0, _dot_nt(km_ref[...], qs_sc[0]) + jnp.concatenate([meta_bias, meta_bias], axis=1), vm_ref[...])
    update(1, _dot_nt(km_ref[...], qs_sc[1]), vm_ref[...])

    plans = [plan(n) for n in range(nq + 1)]
    spread = (nq + 1) // groups
    assert spread >= 2
    s_next = scores(*plans[0])
    for n in range(nq + 1):
        s_cur = s_next
        slot, key_tile, _ = plans[n]
        if n < nq:
            s_next = scores(*plans[n + 1])
        m_prev = m_sc[slot]
        m_new = jnp.maximum(m_prev, jnp.max(s_cur, axis=0, keepdims=True))
        alpha = jnp.exp2(m_prev - m_new)
        if n // spread < groups:
            if n % spread == 0:
                sample_s = sample_scores(n // spread)
            elif n % spread == 1:
                sample_values(n // spread, *sample_softmax(sample_s))
        p = jnp.exp2(s_cur - m_new)
        l_sc[slot] = alpha * l_sc[slot] + jnp.sum(p, axis=0, keepdims=True)
        acc_sc[slot] = alpha * acc_sc[slot] + _dot_tn(v_ref[rows(key_tile), :], p.astype(BF16))
        m_sc[slot] = m_new

    lam = _lam_value(lamq_ref, lamk_ref)
    for t, o_ref in enumerate((oa_ref, ob_ref)):
        acc = acc_sc[t]
        inv = 1.0 / l_sc[t]
        o_t = acc[:, 0:tile] * inv[:, 0:tile] - (lam * inv[:, tile:]) * acc[:, tile:]
        o_ref[...] = o_t.T.astype(o_ref.dtype)
    sample_end()


def _attn(page_table, rel_bias, lam_q, lam_k, qn, knb, vb, km, vm, s_qn, s_kn, s_v, cache_k, cache_v, tile, tq, pages):
    batch, seq, _ = qn.shape
    nq = seq // tile
    half = nq // 2
    assert nq % 2 == 0
    steps = HEADS * batch * half
    nb, n_pages = page_table.shape
    rpb = s_qn.shape[0] // nb
    pcols = cache_k.shape[1]
    nrow = 2 * HEADS * tq
    total_groups = nb * (n_pages // pages)
    groups = total_groups // steps
    assert groups * steps == total_groups and (n_pages // pages) % groups == 0
    assert groups <= nq + 1 and total_groups >= RING_SLOTS - 1
    steps_per_elem = (n_pages // pages) // groups
    small = lambda a: pl.BlockSpec(a.shape, lambda h, b, ip, pt: (0,) * a.ndim)
    kv = pl.BlockSpec((None, seq, LANES), lambda h, b, ip, pt: (b, 0, h))
    meta = pl.BlockSpec((N_META, LANES), lambda h, b, ip, pt: (0, h))
    short = lambda h, b, ip, pt: (b, ip, h)
    elem = lambda h, b, ip, pt: (((h * batch + b) * half + ip) // steps_per_elem, 0)
    grid_spec = pltpu.PrefetchScalarGridSpec(
        num_scalar_prefetch=1,
        grid=(HEADS, batch, half),
        in_specs=[pl.BlockSpec(memory_space=pltpu.SMEM), small(lam_q), small(lam_k),
                  pl.BlockSpec((None, tile, LANES), short),
                  pl.BlockSpec((None, tile, LANES), lambda h, b, ip, pt: (b, nq - 1 - ip, h)),
                  kv, kv, meta, meta,
                  pl.BlockSpec((rpb, QKW), elem), pl.BlockSpec((rpb * HEADS, LANES), elem),
                  pl.BlockSpec((rpb * HEADS, LANES), elem),
                  pl.BlockSpec(memory_space=pl.ANY), pl.BlockSpec(memory_space=pl.ANY)],
        out_specs=[pl.BlockSpec((None, tile, LANES), short),
                   pl.BlockSpec((None, tile, LANES), lambda h, b, ip, pt: (b, half - 1 - ip, h)),
                   pl.BlockSpec((rpb, VW), elem)],
        scratch_shapes=[pltpu.VMEM((3, tile, tile), F32), pltpu.VMEM((N_META, tile), F32),
                        pltpu.VMEM((2, 2 * tile, LANES), BF16),
                        pltpu.VMEM((2, 1, 2 * tile), F32), pltpu.VMEM((2, 1, 2 * tile), F32),
                        pltpu.VMEM((2, LANES, 2 * tile), F32),
                        pltpu.VMEM((RING_SLOTS, pages, pcols, LANES), F32),
                        pltpu.VMEM((RING_SLOTS, pages, pcols, LANES), F32),
                        pltpu.SemaphoreType.DMA((RING_SLOTS, 2, pages)),
                        pltpu.VMEM((nrow, LANES), BF16), pltpu.VMEM((nrow, pcols), F32), pltpu.VMEM((nrow, pcols), F32),
                        pltpu.VMEM((nrow, LANES), F32), pltpu.VMEM((nrow, 1), F32), pltpu.VMEM((nrow, 1), F32),
                        pltpu.VMEM((nrow, LANES), F32)],
    )
    return pl.pallas_call(
        functools.partial(_attn_kernel, tile=tile, nq=nq, pages=pages, tq=tq, groups=groups),
        grid_spec=grid_spec,
        out_shape=[jax.ShapeDtypeStruct((batch, seq // 2, VW), BF16)] * 2
                  + [jax.ShapeDtypeStruct((nb * rpb, VW), F32)],
        compiler_params=pltpu.CompilerParams(dimension_semantics=("arbitrary",) * 3, vmem_limit_bytes=VMEM_LIMIT),
        name="attn",
    )(page_table, rel_bias, lam_q, lam_k, qn, qn, knb, vb, km, vm, s_qn, s_kn, s_v, cache_k, cache_v)


def _sample_stream(st, total_steps, pt_ref, rb_ref, lamq_ref, lamk_ref, q_ref, kn_ref, vn_ref, ck_hbm, cv_hbm, o_ref,
                   kbuf, vbuf, sem, w_sc, mask_sc, ptab_sc, ntab_sc, m_sc, l_sc, acc_sc, *, pages, tq, groups):
    nrow = 2 * HEADS * tq
    pcols = kbuf.shape[2]
    page_size = pcols // HEADS
    ng = pt_ref.shape[1] // pages
    steps_per_elem = ng // groups
    first = st % steps_per_elem == 0
    final = st % steps_per_elem == steps_per_elem - 1
    total = total_steps * groups

    def page_copies(gi):
        slot = gi % RING_SLOTS
        src = jnp.minimum(gi, total - 1)
        out = []
        for u in range(pages):
            page = pt_ref[src // ng, (src % ng) * pages + u]
            out.append(pltpu.make_async_copy(ck_hbm.at[page], kbuf.at[slot, u], sem.at[slot, 0, u]))
            out.append(pltpu.make_async_copy(cv_hbm.at[page], vbuf.at[slot, u], sem.at[slot, 1, u]))
        return out

    def build_tables():
        def tables(cols, offset):
            r = lax.broadcasted_iota(jnp.int32, (nrow, cols), 0)
            c = lax.broadcasted_iota(jnp.int32, (nrow, cols), 1)
            tok = r // (2 * HEADS)
            head = (r % (2 * HEADS)) // 2
            n = offset + tok - c // HEADS
            t = jnp.zeros((nrow, cols), F32)
            for h in range(HEADS):
                t = jnp.where(head == h, _bias_table(n, rb_ref, h), t)
            return jnp.where(head == c % HEADS, t, NEG), c // HEADS

        mask_sc[...] = jnp.where(tables(pcols, MAX_DISTANCE)[0] > 0.5 * NEG, 0.0, NEG)
        ptab_sc[...] = tables(pcols, page_size)[0]
        nt, slot = tables(LANES, 0)
        ntab_sc[...] = jnp.where(slot < tq, nt, NEG)

    def update(s, pv):
        m_prev = m_sc[...]
        m_new = jnp.maximum(m_prev, jnp.max(s, axis=-1, keepdims=True))
        alpha = jnp.exp2(m_prev - m_new)
        p = jnp.exp2(s - m_new)
        l_sc[...] = alpha * l_sc[...] + jnp.sum(p, axis=-1, keepdims=True)
        acc_sc[...] = alpha * acc_sc[...] + pv(p.astype(BF16))
        m_sc[...] = m_new

    def start_element():
        q = q_ref[...].astype(F32)
        r8 = lax.broadcasted_iota(jnp.int32, (2 * HEADS, LANES), 0)
        lane = lax.broadcasted_iota(jnp.int32, (2 * HEADS, LANES), 1)
        blocks = []
        for t in range(tq):
            blk = jnp.zeros((2 * HEADS, LANES), F32)
            for h in range(HEADS):
                blk = jnp.where(r8 // 2 == h, jnp.broadcast_to(q[t:t + 1, h * LANES:(h + 1) * LANES], blk.shape), blk)
            blocks.append(jnp.where(lane // DK == r8 % 2, blk, 0.0))
        w_sc[...] = jnp.concatenate(blocks, axis=0).astype(BF16)
        m_sc[...] = jnp.full(m_sc.shape, NEG, F32)
        l_sc[...] = jnp.zeros(l_sc.shape, F32)
        acc_sc[...] = jnp.zeros(acc_sc.shape, F32)
        pad = jnp.zeros((LANES - kn_ref.shape[0], LANES), F32)
        kn = jnp.concatenate([kn_ref[...], pad], axis=0).astype(BF16)
        vn = jnp.concatenate([vn_ref[...], pad], axis=0).astype(BF16)
        s = _dot_nt(w_sc[...], kn) + ntab_sc[...]
        update(s, lambda p: _dot(p, vn))

    def begin():
        @pl.when(st == 0)
        def _():
            for ahead in range(RING_SLOTS - 1):
                for c in page_copies(ahead):
                    c.start()
            build_tables()

        pl.when(first)(start_element)

    def scores(n):
        gi = st * groups + n
        for c in page_copies(gi + RING_SLOTS - 1):
            c.start()
        for c in page_copies(gi):
            c.wait()
        slot = gi % RING_SLOTS
        w = w_sc[...]
        parts = []
        for u in range(pages):
            s = _dot_nt(w, kbuf[slot, u].astype(BF16))
            if u == pages - 1 and n == groups - 1:
                s = s + jnp.where(final, ptab_sc[...], mask_sc[...])
            else:
                s = s + mask_sc[...]
            parts.append(s)
        return jnp.concatenate(parts, axis=-1)

    def softmax(s):
        m_prev = m_sc[...]
        m_new = jnp.maximum(m_prev, jnp.max(s, axis=-1, keepdims=True))
        alpha = jnp.exp2(m_prev - m_new)
        p = jnp.exp2(s - m_new)
        l_sc[...] = alpha * l_sc[...] + jnp.sum(p, axis=-1, keepdims=True)
        m_sc[...] = m_new
        return alpha, p.astype(BF16)

    def values(n, alpha, p):
        slot = (st * groups + n) % RING_SLOTS
        acc = _dot(p[:, 0:pcols], vbuf[slot, 0].astype(BF16))
        for u in range(1, pages):
            acc += _dot(p[:, u * pcols:(u + 1) * pcols], vbuf[slot, u].astype(BF16))
        acc_sc[...] = alpha * acc_sc[...] + acc

    def end():
        pl.when(final)(finish_element)

        @pl.when(st == total_steps - 1)
        def _():
            for extra in range(RING_SLOTS - 1):
                for c in page_copies(total + extra):
                    c.wait()

    def finish_element():
        lam = _lam_value(lamq_ref, lamk_ref)
        accn = acc_sc[...] * (1.0 / l_sc[...])
        o_ref[...] = jnp.zeros(o_ref.shape, F32)
        for t in range(tq):
            for h in range(HEADS):
                r = t * 2 * HEADS + 2 * h
                o_ref[t:t + 1, h * DV:(h + 1) * DV] = accn[r:r + 1, :] - lam * accn[r + 1:r + 2, :]

    return begin, scores, softmax, values, end


def _gla_kernel(s0_ref, q_ref, k_ref, v_ref, lg_ref, tri_ref, o_ref, sout_ref, s_sc, *, valid):
    c = pl.program_id(1)
    nseq = q_ref.shape[0]

    @pl.when(c == 0)
    def _():
        for i in range(nseq):
            s_sc[i] = s0_ref[0 if s0_ref.shape[0] != nseq else i]

    rows_in = min(q_ref.shape[1], CHUNK)
    for r0 in range(0, q_ref.shape[1], CHUNK):
        _gla_chunk(q_ref, k_ref, v_ref, lg_ref, tri_ref, o_ref, s_sc, r0, rows_in, valid)

    @pl.when(c == pl.num_programs(1) - 1)
    def _():
        sout_ref[...] = s_sc[...]


def _gla_chunk(q_ref, k_ref, v_ref, lg_ref, tri_ref, o_ref, s_sc, r0, rows_in, valid):
    nseq = q_ref.shape[0]

    def padded(a):
        if rows_in == CHUNK:
            return a
        return jnp.concatenate([a, jnp.zeros((CHUNK - rows_in, a.shape[1]), a.dtype)], axis=0)

    wide = lambda ref: jnp.concatenate([padded(ref[i, r0:r0 + rows_in, :]) for i in range(nseq)], axis=1)
    q = wide(q_ref)
    k = wide(k_ref)
    lg = wide(lg_ref)
    vb = wide(v_ref)
    if valid < CHUNK:
        live = lax.broadcasted_iota(jnp.int32, (CHUNK, 1), 0) < valid
        k = jnp.where(live, k, 0.0)
        lg = jnp.where(live, lg, 0.0)
        vb = jnp.where(live, vb, jnp.zeros_like(vb))

    lg_hi = lg.astype(BF16)
    lg_lo = (lg - lg_hi.astype(F32)).astype(BF16)
    tri = tri_ref[...]
    b = _dot(tri, lg_hi) + _dot(tri, lg_lo)
    b_mid = b[CHUNK // 2 - 1:CHUNK // 2, :]
    qs = q * jnp.exp(b)
    qt = q * jnp.exp(b - b_mid)
    kt = (k * jnp.exp(b_mid - b)).astype(BF16)
    b_t = b.T
    b_last = b_t[:, CHUNK - 1:CHUNK]
    kl_t = (k.T * jnp.exp(b_last - b_t)).astype(BF16)
    decay = jnp.exp(b_last)

    row = lax.broadcasted_iota(jnp.int32, (CHUNK, CHUNK), 0)
    col = lax.broadcasted_iota(jnp.int32, (CHUNK, CHUNK), 1)
    causal = row >= col
    lane = lax.broadcasted_iota(jnp.int32, (1, LANES), 1)
    heads = [(i, h) for i in range(nseq) for h in range(HEADS)]
    cols = lambda i, h: slice(i * GW + (h // 2) * LANES, i * GW + (h // 2 + 1) * LANES)
    mine = lambda h: (lane // DK) == (h % 2)
    v_of = lambda i, h: vb[:, i * VW + h * DV:i * VW + (h + 1) * DV]
    s_old = [s_sc[i] for i in range(nseq)]
    s_bf = [s.astype(BF16) for s in s_old]
    a = [_dot_nt(jnp.where(mine(h), qt[:, cols(i, h)], 0.0).astype(BF16), kt[:, cols(i, h)]) for i, h in heads]
    inter = [_dot(jnp.where(mine(h), qs[:, cols(i, h)], 0.0).astype(BF16),
                  s_bf[i][(h // 2) * LANES:(h // 2 + 1) * LANES, :]) for i, h in heads]
    upd = [_dot(kl_t[i * GW + h * DK:i * GW + (h + 1) * DK, :], v_of(i, h)) for i, h in heads]
    for n, (i, h) in enumerate(heads):
        o_h = inter[n] + _dot(jnp.where(causal, a[n], 0.0).astype(BF16), v_of(i, h))
        o_ref[i, r0:r0 + rows_in, h * DV:(h + 1) * DV] = o_h[0:rows_in].astype(o_ref.dtype)
        rows = slice(i * GW + h * DK, i * GW + (h + 1) * DK)
        s_sc[i, h * DK:(h + 1) * DK, :] = decay[rows, :] * s_old[i][h * DK:(h + 1) * DK, :] + upd[n]


def _gla(s0, gq, gk, gv, lg, tri, nb, rows_in, valid):
    nc = gq.shape[0] // (nb * rows_in)
    per = min(nb, GLA_SEQS_PER_STEP)
    cps = GLA_CHUNKS_PER_STEP if nc % GLA_CHUNKS_PER_STEP == 0 else 1
    assert nb % per == 0 and (cps == 1 or rows_in == CHUNK)
    s0_blk = per if s0.shape[0] == nb else 1
    s0_map = (lambda b, c: (b, 0, 0)) if s0.shape[0] == nb else (lambda b, c: (0, 0, 0))
    seqs = lambda a: a.reshape(nb, nc * rows_in, a.shape[-1])
    rowblk = lambda w: pl.BlockSpec((per, rows_in * cps, w), lambda b, c: (b, c, 0))
    o, s_fin = pl.pallas_call(
        functools.partial(_gla_kernel, valid=valid),
        grid=(nb // per, nc // cps),
        in_specs=[pl.BlockSpec((s0_blk, GW, DV), s0_map), rowblk(GW), rowblk(GW), rowblk(VW), rowblk(GW),
                  pl.BlockSpec(tri.shape, lambda b, c: (0, 0))],
        out_specs=[rowblk(VW), pl.BlockSpec((per, GW, DV), lambda b, c: (b, 0, 0))],
        out_shape=[jax.ShapeDtypeStruct((nb, nc * rows_in, VW), BF16), jax.ShapeDtypeStruct((nb, GW, DV), F32)],
        scratch_shapes=[pltpu.VMEM((per, GW, DV), F32)],
        compiler_params=pltpu.CompilerParams(dimension_semantics=("arbitrary", "arbitrary"),
                                             vmem_limit_bytes=VMEM_LIMIT),
        name="gla",
    )(s0, seqs(gq), seqs(gk), seqs(gv), seqs(lg), tri)
    return o.reshape(nb * nc * rows_in, VW), s_fin


def _merge_kernel(x_ref, *rest, seq_tiles):
    if seq_tiles is None:
        od_ref, og_ref, gate_ref, dnw_ref, gnw_ref, wtop_ref, wbot_ref, y_ref = rest
        od = od_ref[...].astype(F32)
    else:
        oda_ref, odb_ref, og_ref, gate_ref, dnw_ref, gnw_ref, wtop_ref, wbot_ref, y_ref = rest
        od = jnp.where(pl.program_id(0) % seq_tiles < seq_tiles // 2, oda_ref[...], odb_ref[...]).astype(F32)

    def head_norm(o, w):
        parts = []
        for h in range(HEADS):
            sl = o[:, h * DV:(h + 1) * DV]
            parts.append(sl * lax.rsqrt(jnp.mean(sl * sl, axis=-1, keepdims=True) + RMS_EPS))
        return jnp.concatenate(parts, axis=-1) * w

    gate = gate_ref[...].astype(F32)
    mix_d = (head_norm(od, dnw_ref[...]) * gate[:, 0:VW]).astype(BF16)
    mix_g = (head_norm(og_ref[...].astype(F32), gnw_ref[...]) * gate[:, VW:]).astype(BF16)
    y_ref[...] = x_ref[...] + _dot(mix_d, wtop_ref[...]) + _dot(mix_g, wbot_ref[...])


def _merge(x, od_parts, og, gates, dnw, gnw, wtop, wbot, tr, seq_tiles=None):
    rows = x.shape[0]
    full = lambda a: pl.BlockSpec(a.shape, lambda i: (0,) * a.ndim)
    row = lambda w: pl.BlockSpec((tr, w), lambda i: (i, 0))
    if seq_tiles is None:
        od_specs = [row(VW)]
    else:
        assert seq_tiles % 2 == 0
        half = seq_tiles // 2
        od_specs = [pl.BlockSpec((tr, VW), lambda i: ((i // seq_tiles) * half + jnp.minimum(i % seq_tiles, half - 1), 0)),
                    pl.BlockSpec((tr, VW), lambda i: ((i // seq_tiles) * half + jnp.maximum(i % seq_tiles - half, 0), 0))]
    return pl.pallas_call(
        functools.partial(_merge_kernel, seq_tiles=seq_tiles),
        grid=(rows // tr,),
        in_specs=[row(D_MODEL)] + od_specs + [row(VW), row(2 * VW), full(dnw), full(gnw), full(wtop), full(wbot)],
        out_specs=row(D_MODEL),
        out_shape=jax.ShapeDtypeStruct((rows, D_MODEL), F32),
        compiler_params=pltpu.CompilerParams(dimension_semantics=("arbitrary",), vmem_limit_bytes=VMEM_LIMIT),
        name="merge",
    )(x, *od_parts, og, gates, dnw, gnw, wtop, wbot)


def kernel(x_prompt, x_sample, cache_k, cache_v, state_gla, page_table, meta_tokens, rel_bias, norm_w, w_in,
           q_norm_w, k_norm_w, lam_q, lam_k, diff_norm_w, gla_wa2, gla_ba, gla_norm_w, w_out):
    batch, seq, _ = x_prompt.shape
    nb, tq, _ = x_sample.shape
    tile = 512
    rpb = 16

    w = w_in[0]
    wm = w.astype(BF16)
    wa1 = jnp.pad(w[:, Z_MAIN:], ((0, 0), (0, LANES - GLA_GATE_RANK))).astype(BF16)
    wa2 = jnp.pad(gla_wa2[0], ((0, LANES - GLA_GATE_RANK), (0, 0))).astype(BF16)
    ba = gla_ba[0][None]
    nw = norm_w[0][None]
    qw = jnp.tile(q_norm_w[0].reshape(-1), HEADS)[None] * (DIFF_SCALE * LOG2E)
    kw = jnp.tile(k_norm_w[0].reshape(-1), HEADS)[None]
    grp = np.arange(256) // DK
    gmat = jnp.asarray(grp[:, None] == grp[None, :], BF16)
    consts = (nw, wm, wa1, wa2, ba, qw, kw, gmat)
    dnw = jnp.tile(diff_norm_w[0], HEADS)[None] * (1.0 - LAM_INIT)
    gnw = jnp.tile(gla_norm_w[0], HEADS)[None]
    wtop = w_out[0][:VW].astype(BF16)
    wbot = w_out[0][VW:].astype(BF16)
    t_idx = np.arange(CHUNK)
    tri = jnp.asarray(t_idx[:, None] >= t_idx[None, :], BF16)
    lq, lk = lam_q[0], lam_k[0]

    xs = jnp.pad(x_sample, ((0, 0), (0, rpb - tq), (0, 0))).reshape(nb * rpb, D_MODEL)
    xm = jnp.pad(meta_tokens, ((0, CHUNK - N_META), (0, 0)))
    x_small = jnp.concatenate([xs, xm], axis=0)
    ns = nb * rpb
    small = _inproj(x_small, x_small.shape[0], consts)
    s_qn, s_kn, s_knb, s_v, s_vb, s_gate, s_gq, s_gk, s_gv, s_lg = [a[:ns] for a in small]
    m_qn, m_kn, m_knb, m_v, m_vb, m_gate, m_gq, m_gk, m_gv, m_lg = [a[ns:] for a in small]

    xp = x_prompt.reshape(batch * seq, D_MODEL)
    meta_rows = (m_kn[:N_META].reshape(N_META * HEADS, 2 * DK), m_v[:N_META].reshape(N_META * HEADS, DV))
    p_qn, k_rows, p_knb, v_rows, p_vb, p_gate, p_gq, p_gk, p_gv, p_lg = _inproj(
        xp, tile, consts, seq_tiles=seq // tile, meta_rows=meta_rows)

    n_pool, page_size = cache_k.shape[1], cache_k.shape[2]
    ck = cache_k.reshape(n_pool, page_size * HEADS, 2 * DK)
    cv = cache_v.reshape(n_pool, page_size * HEADS, DV)
    o_first, o_second, os_d = _attn(
        page_table, rel_bias, lq, lk, p_qn.reshape(batch, seq, QKW), p_knb.reshape(batch, seq, QKW),
        p_vb.reshape(batch, seq, VW), m_knb, m_vb, s_qn, s_kn.reshape(ns * HEADS, 2 * DK),
        s_v.reshape(ns * HEADS, DV), ck, cv, tile, tq, ATTN_PAGES)
    o_d = (o_first, o_second)

    zero_state = jnp.zeros((1, GW, DV), F32)
    _, s_meta = _gla(zero_state, m_gq, m_gk, m_gv, m_lg, tri, 1, CHUNK, N_META)
    o_g, s_fin = _gla(s_meta, p_gq, p_gk, p_gv, p_lg, tri, batch, CHUNK, CHUNK)
    os_g, s_new = _gla(state_gla[0].reshape(nb, GW, DV), s_gq, s_gk, s_gv, s_lg, tri, nb, rpb, tq)

    half_rows = batch * seq // 2
    y_prompt = _merge(xp, [o.reshape(half_rows, VW) for o in o_d], o_g, p_gate, dnw, gnw, wtop, wbot, MERGE_ROWS,
                      seq_tiles=seq // MERGE_ROWS)
    y_small = _merge(xs, [os_d], os_g, s_gate, dnw, gnw, wtop, wbot, ns)

    y_prompt = y_prompt.reshape(batch, seq, D_MODEL)
    y_sample = y_small.reshape(nb, rpb, D_MODEL)[:, :tq]
    k_prompt = k_rows.reshape(1, batch, seq + N_META, HEADS, 2 * DK)
    v_prompt = v_rows.reshape(1, batch, seq + N_META, HEADS, DV)
    s_prompt = s_fin.reshape(1, batch, HEADS, DK, DV)
    k_sample = s_kn.reshape(nb, rpb, HEADS, 2 * DK)[None, :, :tq]
    v_sample = s_v.reshape(nb, rpb, HEADS, DV)[None, :, :tq]
    s_sample = s_new.reshape(1, nb, HEADS, DK, DV)
    return (y_prompt, y_sample, k_prompt, v_prompt, s_prompt, k_sample, v_sample, s_sample)
```

```python
import functools
import math

import numpy as np
import jax
import jax.numpy as jnp
from jax import lax
from jax.experimental import pallas as pl
from jax.experimental.pallas import tpu as pltpu

D_MODEL = 1024
N_META = 16
HEADS = 4
DK = 64
DV = 128
DIFF_SCALE = DK ** -0.5
GLA_GATE_RANK = 16
GLA_GATE_NORM = 16.0
NUM_BUCKETS = 32
MAX_DISTANCE = 128
RMS_EPS = 1e-6
LAM_INIT = 0.8 - 0.6 * math.exp(-0.3 * 0)
QKW = HEADS * 2 * DK
VW = HEADS * DV
GW = HEADS * DK
Z_MAIN = 3 * QKW + VW + 2 * GW + 2 * VW
LANES = 128
CHUNK = 128
NEG = -1e30
LOG2E = math.log2(math.e)
RING_SLOTS = 3
ATTN_PAGES = 16
MERGE_ROWS = 1024
GLA_CHUNKS_PER_STEP = 4
GLA_SEQS_PER_STEP = 4
VMEM_LIMIT = 56 * 1024 * 1024

F32 = jnp.float32
BF16 = jnp.bfloat16


def _dot(a, b):
    return jnp.dot(a, b, preferred_element_type=F32)


def _dot_nt(a, b):
    return lax.dot_general(a, b, (((1,), (1,)), ((), ())), preferred_element_type=F32)


def _dot_tn(a, b):
    return lax.dot_general(a, b, (((0,), (0,)), ((), ())), preferred_element_type=F32)


def _lam_value(lamq_ref, lamk_ref):
    e = jnp.exp(jnp.sum(lamq_ref[...] * lamk_ref[...], axis=-1, keepdims=True))
    return e[0:1, :] - e[1:2, :] + LAM_INIT


def _inproj_kernel(x_ref, nw_ref, wm_ref, wa1_ref, wa2_ref, ba_ref, qw_ref, kw_ref, g_ref, *rest, seq_tiles):
    if seq_tiles is None:
        qn_ref, kn_ref, knb_ref, v_ref, vb_ref, gate_ref, gq_ref, gk_ref, gv_ref, lg_ref = rest
    else:
        (mk_ref, mv_ref, qn_ref, kn_hbm, knb_ref, v_hbm, vb_ref, gate_ref, gq_ref, gk_ref, gv_ref, lg_ref,
         stage, sem, meta_sem) = rest
    x = x_ref[...]
    ms = jnp.mean(x * x, axis=-1, keepdims=True)
    hn = (x * lax.rsqrt(ms + RMS_EPS) * nw_ref[...]).astype(BF16)

    def proj(lo, hi):
        return _dot(hn, wm_ref[:, lo:hi])

    def group_norm(z, w):
        sq = (z * z).astype(BF16)
        ss = jnp.concatenate([_dot(sq[:, c:c + 256], g_ref[...]) for c in range(0, QKW, 256)], axis=-1)
        return z * lax.rsqrt(ss * (1.0 / DK) + RMS_EPS) * w

    qn_ref[...] = group_norm(proj(0, QKW), qw_ref[...]).astype(BF16)
    kn = group_norm(proj(QKW, 2 * QKW), kw_ref[...])
    knb_ref[...] = kn.astype(BF16)
    v = proj(2 * QKW, 2 * QKW + VW)
    vb_ref[...] = v.astype(BF16)
    if seq_tiles is None:
        kn_ref[...] = kn
        v_ref[...] = v
    o = 2 * QKW + VW
    dg = proj(o, o + VW)
    gate_ref[:, 0:VW] = (dg * jax.nn.sigmoid(dg)).astype(BF16)
    o += VW
    gq_ref[...] = proj(o, o + GW) * (DK ** -0.5)
    gk_ref[...] = proj(o + GW, o + 2 * GW)
    o += 2 * GW
    gv_ref[...] = proj(o, o + VW).astype(BF16)
    o += VW
    gg = proj(o, o + VW)
    gate_ref[:, VW:2 * VW] = (gg * jax.nn.sigmoid(gg)).astype(BF16)
    ga = _dot(hn, wa1_ref[...])
    xg = _dot(ga.astype(BF16), wa2_ref[...]) + ba_ref[...]
    lg_ref[...] = (jnp.minimum(xg, 0.0) - jnp.log(1.0 + jnp.exp(-jnp.abs(xg)))) * (1.0 / GLA_GATE_NORM)
    if seq_tiles is not None:
        _write_cache_rows(kn, v, mk_ref, mv_ref, kn_hbm, v_hbm, stage, sem, meta_sem, seq_tiles)


def _write_cache_rows(kn, v, mk_ref, mv_ref, k_hbm, v_hbm, stage, sem, meta_sem, seq_tiles):
    i = pl.program_id(0)
    n = pl.num_programs(0)
    tr = kn.shape[0]
    seq_rows = (seq_tiles * tr + N_META) * HEADS

    def tile_copies(step):
        row0 = (step // seq_tiles) * seq_rows + (N_META + (step % seq_tiles) * tr) * HEADS
        dst = pl.ds(pl.multiple_of(row0, 8), tr * HEADS)
        return [pltpu.make_async_copy(stage.at[step % 2, 0], k_hbm.at[dst], sem.at[step % 2, 0]),
                pltpu.make_async_copy(stage.at[step % 2, 1], v_hbm.at[dst], sem.at[step % 2, 1])]

    @pl.when(i >= 2)
    def _():
        for c in tile_copies(i - 2):
            c.wait()

    slot = i % 2
    for h in range(HEADS):
        stage[slot, 0, pl.ds(h, tr, stride=HEADS), :] = kn[:, h * LANES:(h + 1) * LANES]
        stage[slot, 1, pl.ds(h, tr, stride=HEADS), :] = v[:, h * LANES:(h + 1) * LANES]
    for c in tile_copies(i):
        c.start()

    @pl.when(i % seq_tiles == 0)
    def _():
        dst = pl.ds(pl.multiple_of((i // seq_tiles) * seq_rows, 8), N_META * HEADS)
        meta = [pltpu.make_async_copy(mk_ref, k_hbm.at[dst], meta_sem.at[0]),
                pltpu.make_async_copy(mv_ref, v_hbm.at[dst], meta_sem.at[1])]
        for c in meta:
            c.start()
        for c in meta:
            c.wait()

    @pl.when(i == n - 1)
    def _():
        for c in tile_copies(i - 1) + tile_copies(i):
            c.wait()


def _inproj(x, tr, consts, seq_tiles=None, meta_rows=None):
    rows = x.shape[0]
    steps = rows // tr
    full = lambda a: pl.BlockSpec(a.shape, lambda i: (0,) * a.ndim)
    row = lambda w: pl.BlockSpec((tr, w), lambda i: (i, 0))
    outs = [(QKW, BF16), (QKW, F32), (QKW, BF16), (VW, F32), (VW, BF16), (2 * VW, BF16),
            (GW, F32), (GW, F32), (VW, BF16), (GW, F32)]
    out_specs = [row(w) for w, _ in outs]
    out_shape = [jax.ShapeDtypeStruct((rows, w), dt) for w, dt in outs]
    extra_in, extra_specs, scratch = [], [], []
    if seq_tiles is not None:
        assert steps % seq_tiles == 0 and steps >= 2
        cache_rows = (steps // seq_tiles) * (seq_tiles * tr + N_META) * HEADS
        for o in (1, 3):
            out_specs[o] = pl.BlockSpec(memory_space=pl.ANY)
            out_shape[o] = jax.ShapeDtypeStruct((cache_rows, LANES), F32)
        extra_in = list(meta_rows)
        extra_specs = [full(a) for a in meta_rows]
        scratch = [pltpu.VMEM((2, 2, tr * HEADS, LANES), F32), pltpu.SemaphoreType.DMA((2, 2)),
                   pltpu.SemaphoreType.DMA((2,))]
    return pl.pallas_call(
        functools.partial(_inproj_kernel, seq_tiles=seq_tiles),
        grid=(steps,),
        in_specs=[row(D_MODEL)] + [full(a) for a in consts] + extra_specs,
        out_specs=out_specs,
        out_shape=out_shape,
        scratch_shapes=scratch,
        compiler_params=pltpu.CompilerParams(dimension_semantics=("arbitrary",), vmem_limit_bytes=VMEM_LIMIT),
        name="inproj",
    )(x, *consts, *extra_in)


def _bucket_ranges():
    n = np.arange(MAX_DISTANCE)
    max_exact = NUM_BUCKETS // 2
    nf = np.maximum(n, 1).astype(np.float32)
    large = max_exact + (np.log(nf / np.float32(max_exact)) / np.float32(math.log(MAX_DISTANCE / max_exact))
                         * np.float32(NUM_BUCKETS - max_exact)).astype(np.int32)
    bucket = np.where(n < max_exact, n, np.minimum(large, NUM_BUCKETS - 1))
    return [(int(n[bucket == b].min()), int(n[bucket == b].max())) for b in range(NUM_BUCKETS)]


_BUCKET_RANGES = _bucket_ranges()


def _bias_table(n, rb_ref, h):
    far = rb_ref[NUM_BUCKETS - 1, h]
    t = jnp.zeros(n.shape, F32)
    for b, (lo, hi) in enumerate(_BUCKET_RANGES[:-1]):
        val = (rb_ref[b, h] - far) * LOG2E
        cond = (n == lo) if lo == hi else ((n >= lo) & (n <= hi))
        t = jnp.where(cond, val, t)
    return jnp.where(n < 0, NEG, t)


def _attn_kernel(pt_ref, rb_ref, lamq_ref, lamk_ref, qa_ref, qb_ref, k_ref, v_ref, km_ref, vm_ref,
                 sq_ref, skn_ref, svn_ref, ck_hbm, cv_hbm, oa_ref, ob_ref, so_ref,
                 bias_sc, mt_sc, qs_sc, m_sc, l_sc, acc_sc, *sample_scratch, tile, nq, pages, tq, groups):
    h = pl.program_id(0)
    b = pl.program_id(1)
    ip = pl.program_id(2)
    half = nq // 2
    st = (h * pl.num_programs(1) + b) * half + ip
    total_steps = pl.num_programs(0) * pl.num_programs(1) * half
    sample_begin, sample_scores, sample_softmax, sample_values, sample_end = _sample_stream(
        st, total_steps, pt_ref, rb_ref, lamq_ref, lamk_ref, sq_ref, skn_ref, svn_ref, ck_hbm, cv_hbm, so_ref,
        *sample_scratch, pages=pages, tq=tq, groups=groups)
    sample_begin()

    @pl.when((b == 0) & (ip == 0))
    def _():
        kk = lax.broadcasted_iota(jnp.int32, (tile, tile), 0)
        qq = lax.broadcasted_iota(jnp.int32, (tile, tile), 1)
        bias_sc[0] = jnp.zeros((tile, tile), F32)
        bias_sc[1] = _bias_table(qq - kk + tile, rb_ref, h)
        bias_sc[2] = _bias_table(qq - kk, rb_ref, h)
        km = lax.broadcasted_iota(jnp.int32, (N_META, tile), 0)
        qm = lax.broadcasted_iota(jnp.int32, (N_META, tile), 1)
        mt_sc[...] = _bias_table(N_META + qm - km, rb_ref, h)

    lane = lax.broadcasted_iota(jnp.int32, (1, LANES), 1)
    for t, q_ref in enumerate((qa_ref, qb_ref)):
        q = q_ref[...]
        zero = jnp.zeros_like(q)
        qs_sc[t] = jnp.concatenate([jnp.where(lane < DK, q, zero), jnp.where(lane >= DK, q, zero)], axis=0)

    def plan(n):
        if n == 0:
            return 0, ip, 2
        if n == 1:
            return 1, nq - 2 - ip, 1
        if n == 2:
            return 1, nq - 1 - ip, 2
        has_sub = ip >= 1
        if n == 3:
            return jnp.where(has_sub, 0, 1), jnp.where(has_sub, ip - 1, 0), jnp.where(has_sub, 1, 0)
        j = n - 4
        short_far = jnp.maximum(ip - 1, 0)
        long_first = jnp.where(has_sub, 0, 1)
        return jnp.where(j < short_far, 0, 1), jnp.where(j < short_far, j, j - short_far + long_first), None

    def rows(key_tile):
        return pl.ds(pl.multiple_of(key_tile * tile, tile), tile)

    def scores(slot, key_tile, kind):
        s = _dot_nt(k_ref[rows(key_tile), :], qs_sc[slot])
        if kind is None:
            return s
        bias = bias_sc[kind]
        return s + jnp.concatenate([bias, bias], axis=1)

    meta_bias = jnp.where(ip == 0, mt_sc[...], 0.0)
    for t in range(2):
        s = _dot_nt(km_ref[...], qs_sc[t])
        if t == 0:
            s = s + jnp.concatenate([meta_bias, meta_bias], axis=1)
        m = jnp.max(s, axis=0, keepdims=True)
        p = jnp.exp2(s - m)
        m_sc[t] = m
        l_sc[t] = jnp.sum(p, axis=0, keepdims=True)
        acc_sc[t] = _dot_tn(vm_ref[...], p.astype(BF16))

    plans = [plan(n) for n in range(nq + 1)]
    spread = (nq + 1) // groups
    assert spread >= 2
    s_next = scores(*plans[0])
    for n in range(nq + 1):
        s_cur = s_next
        slot, key_tile, _ = plans[n]
        if n < nq:
            s_next = scores(*plans[n + 1])
        m_prev = m_sc[slot]
        m_new = jnp.maximum(m_prev, jnp.max(s_cur, axis=0, keepdims=True))
        alpha = jnp.exp2(m_prev - m_new)
        if n // spread < groups:
            if n % spread == 0:
                sample_s = sample_scores(n // spread)
            elif n % spread == 1:
                sample_values(n // spread, *sample_softmax(sample_s))
        p = jnp.exp2(s_cur - m_new)
        l_sc[slot] = alpha * l_sc[slot] + jnp.sum(p, axis=0, keepdims=True)
        acc_sc[slot] = alpha * acc_sc[slot] + _dot_tn(v_ref[rows(key_tile), :], p.astype(BF16))
        m_sc[slot] = m_new

    lam = _lam_value(lamq_ref, lamk_ref)
    for t, o_ref in enumerate((oa_ref, ob_ref)):
        acc = acc_sc[t]
        inv = 1.0 / l_sc[t]
        o_t = acc[:, 0:tile] * inv[:, 0:tile] - (lam * inv[:, tile:]) * acc[:, tile:]
        o_ref[...] = o_t.T.astype(o_ref.dtype)
    sample_end()


def _attn(page_table, rel_bias, lam_q, lam_k, qn, knb, vb, km, vm, s_qn, s_kn, s_v, cache_k, cache_v, tile, tq, pages):
    batch, seq, _ = qn.shape
    nq = seq // tile
    half = nq // 2
    assert nq % 2 == 0
    steps = HEADS * batch * half
    nb, n_pages = page_table.shape
    rpb = s_qn.shape[0] // nb
    pcols = cache_k.shape[1]
    nrow = 2 * HEADS * tq
    total_groups = nb * (n_pages // pages)
    groups = total_groups // steps
    assert groups * steps == total_groups and (n_pages // pages) % groups == 0
    assert groups <= nq + 1 and total_groups >= RING_SLOTS - 1
    steps_per_elem = (n_pages // pages) // groups
    small = lambda a: pl.BlockSpec(a.shape, lambda h, b, ip, pt: (0,) * a.ndim)
    kv = pl.BlockSpec((None, seq, LANES), lambda h, b, ip, pt: (b, 0, h))
    meta = pl.BlockSpec((N_META, LANES), lambda h, b, ip, pt: (0, h))
    short = lambda h, b, ip, pt: (b, ip, h)
    elem = lambda h, b, ip, pt: (((h * batch + b) * half + ip) // steps_per_elem, 0)
    grid_spec = pltpu.PrefetchScalarGridSpec(
        num_scalar_prefetch=1,
        grid=(HEADS, batch, half),
        in_specs=[pl.BlockSpec(memory_space=pltpu.SMEM), small(lam_q), small(lam_k),
                  pl.BlockSpec((None, tile, LANES), short),
                  pl.BlockSpec((None, tile, LANES), lambda h, b, ip, pt: (b, nq - 1 - ip, h)),
                  kv, kv, meta, meta,
                  pl.BlockSpec((rpb, QKW), elem), pl.BlockSpec((rpb * HEADS, LANES), elem),
                  pl.BlockSpec((rpb * HEADS, LANES), elem),
                  pl.BlockSpec(memory_space=pl.ANY), pl.BlockSpec(memory_space=pl.ANY)],
        out_specs=[pl.BlockSpec((None, tile, LANES), short),
                   pl.BlockSpec((None, tile, LANES), lambda h, b, ip, pt: (b, half - 1 - ip, h)),
                   pl.BlockSpec((rpb, VW), elem)],
        scratch_shapes=[pltpu.VMEM((3, tile, tile), F32), pltpu.VMEM((N_META, tile), F32),
                        pltpu.VMEM((2, 2 * tile, LANES), BF16),
                        pltpu.VMEM((2, 1, 2 * tile), F32), pltpu.VMEM((2, 1, 2 * tile), F32),
                        pltpu.VMEM((2, LANES, 2 * tile), F32),
                        pltpu.VMEM((RING_SLOTS, pages, pcols, LANES), F32),
                        pltpu.VMEM((RING_SLOTS, pages, pcols, LANES), F32),
                        pltpu.SemaphoreType.DMA((RING_SLOTS, 2, pages)),
                        pltpu.VMEM((nrow, LANES), BF16), pltpu.VMEM((nrow, pcols), F32), pltpu.VMEM((nrow, pcols), F32),
                        pltpu.VMEM((nrow, LANES), F32), pltpu.VMEM((nrow, 1), F32), pltpu.VMEM((nrow, 1), F32),
                        pltpu.VMEM((nrow, LANES), F32)],
    )
    return pl.pallas_call(
        functools.partial(_attn_kernel, tile=tile, nq=nq, pages=pages, tq=tq, groups=groups),
        grid_spec=grid_spec,
        out_shape=[jax.ShapeDtypeStruct((batch, seq // 2, VW), BF16)] * 2
                  + [jax.ShapeDtypeStruct((nb * rpb, VW), F32)],
        compiler_params=pltpu.CompilerParams(dimension_semantics=("arbitrary",) * 3, vmem_limit_bytes=VMEM_LIMIT),
        name="attn",
    )(page_table, rel_bias, lam_q, lam_k, qn, qn, knb, vb, km, vm, s_qn, s_kn, s_v, cache_k, cache_v)


def _sample_stream(st, total_steps, pt_ref, rb_ref, lamq_ref, lamk_ref, q_ref, kn_ref, vn_ref, ck_hbm, cv_hbm, o_ref,
                   kbuf, vbuf, sem, w_sc, mask_sc, ptab_sc, ntab_sc, m_sc, l_sc, acc_sc, *, pages, tq, groups):
    nrow = 2 * HEADS * tq
    pcols = kbuf.shape[2]
    page_size = pcols // HEADS
    ng = pt_ref.shape[1] // pages
    steps_per_elem = ng // groups
    first = st % steps_per_elem == 0
    final = st % steps_per_elem == steps_per_elem - 1
    total = total_steps * groups

    def page_copies(gi):
        slot = gi % RING_SLOTS
        src = jnp.minimum(gi, total - 1)
        out = []
        for u in range(pages):
            page = pt_ref[src // ng, (src % ng) * pages + u]
            out.append(pltpu.make_async_copy(ck_hbm.at[page], kbuf.at[slot, u], sem.at[slot, 0, u]))
            out.append(pltpu.make_async_copy(cv_hbm.at[page], vbuf.at[slot, u], sem.at[slot, 1, u]))
        return out

    def build_tables():
        def tables(cols, offset):
            r = lax.broadcasted_iota(jnp.int32, (nrow, cols), 0)
            c = lax.broadcasted_iota(jnp.int32, (nrow, cols), 1)
            tok = r // (2 * HEADS)
            head = (r % (2 * HEADS)) // 2
            n = offset + tok - c // HEADS
            t = jnp.zeros((nrow, cols), F32)
            for h in range(HEADS):
                t = jnp.where(head == h, _bias_table(n, rb_ref, h), t)
            return jnp.where(head == c % HEADS, t, NEG), c // HEADS

        mask_sc[...] = jnp.where(tables(pcols, MAX_DISTANCE)[0] > 0.5 * NEG, 0.0, NEG)
        ptab_sc[...] = tables(pcols, page_size)[0]
        nt, slot = tables(LANES, 0)
        ntab_sc[...] = jnp.where(slot < tq, nt, NEG)

    def update(s, pv):
        m_prev = m_sc[...]
        m_new = jnp.maximum(m_prev, jnp.max(s, axis=-1, keepdims=True))
        alpha = jnp.exp2(m_prev - m_new)
        p = jnp.exp2(s - m_new)
        l_sc[...] = alpha * l_sc[...] + jnp.sum(p, axis=-1, keepdims=True)
        acc_sc[...] = alpha * acc_sc[...] + pv(p.astype(BF16))
        m_sc[...] = m_new

    def start_element():
        q = q_ref[...].astype(F32)
        r8 = lax.broadcasted_iota(jnp.int32, (2 * HEADS, LANES), 0)
        lane = lax.broadcasted_iota(jnp.int32, (2 * HEADS, LANES), 1)
        blocks = []
        for t in range(tq):
            blk = jnp.zeros((2 * HEADS, LANES), F32)
            for h in range(HEADS):
                blk = jnp.where(r8 // 2 == h, jnp.broadcast_to(q[t:t + 1, h * LANES:(h + 1) * LANES], blk.shape), blk)
            blocks.append(jnp.where(lane // DK == r8 % 2, blk, 0.0))
        w_sc[...] = jnp.concatenate(blocks, axis=0).astype(BF16)
        m_sc[...] = jnp.full(m_sc.shape, NEG, F32)
        l_sc[...] = jnp.zeros(l_sc.shape, F32)
        acc_sc[...] = jnp.zeros(acc_sc.shape, F32)
        pad = jnp.zeros((LANES - kn_ref.shape[0], LANES), F32)
        kn = jnp.concatenate([kn_ref[...], pad], axis=0).astype(BF16)
        vn = jnp.concatenate([vn_ref[...], pad], axis=0).astype(BF16)
        s = _dot_nt(w_sc[...], kn) + ntab_sc[...]
        update(s, lambda p: _dot(p, vn))

    def begin():
        @pl.when(st == 0)
        def _():
            for ahead in range(RING_SLOTS - 1):
                for c in page_copies(ahead):
                    c.start()
            build_tables()

        pl.when(first)(start_element)

    def scores(n):
        gi = st * groups + n
        for c in page_copies(gi + RING_SLOTS - 1):
            c.start()
        for c in page_copies(gi):
            c.wait()
        slot = gi % RING_SLOTS
        w = w_sc[...]
        parts = []
        for u in range(pages):
            s = _dot_nt(w, kbuf[slot, u].astype(BF16))
            if u == pages - 1 and n == groups - 1:
                s = s + jnp.where(final, ptab_sc[...], mask_sc[...])
            else:
                s = s + mask_sc[...]
            parts.append(s)
        return jnp.concatenate(parts, axis=-1)

    def softmax(s):
        m_prev = m_sc[...]
        m_new = jnp.maximum(m_prev, jnp.max(s, axis=-1, keepdims=True))
        alpha = jnp.exp2(m_prev - m_new)
        p = jnp.exp2(s - m_new)
        l_sc[...] = alpha * l_sc[...] + jnp.sum(p, axis=-1, keepdims=True)
        m_sc[...] = m_new
        return alpha, p.astype(BF16)

    def values(n, alpha, p):
        slot = (st * groups + n) % RING_SLOTS
        acc = _dot(p[:, 0:pcols], vbuf[slot, 0].astype(BF16))
        for u in range(1, pages):
            acc += _dot(p[:, u * pcols:(u + 1) * pcols], vbuf[slot, u].astype(BF16))
        acc_sc[...] = alpha * acc_sc[...] + acc

    def end():
        pl.when(final)(finish_element)

        @pl.when(st == total_steps - 1)
        def _():
            for extra in range(RING_SLOTS - 1):
                for c in page_copies(total + extra):
                    c.wait()

    def finish_element():
        lam = _lam_value(lamq_ref, lamk_ref)
        accn = acc_sc[...] * (1.0 / l_sc[...])
        o_ref[...] = jnp.zeros(o_ref.shape, F32)
        for t in range(tq):
            for h in range(HEADS):
                r = t * 2 * HEADS + 2 * h
                o_ref[t:t + 1, h * DV:(h + 1) * DV] = accn[r:r + 1, :] - lam * accn[r + 1:r + 2, :]

    return begin, scores, softmax, values, end


def _gla_kernel(s0_ref, q_ref, k_ref, v_ref, lg_ref, tri_ref, o_ref, sout_ref, s_sc, *, valid):
    c = pl.program_id(1)
    nseq = q_ref.shape[0]

    @pl.when(c == 0)
    def _():
        for i in range(nseq):
            s_sc[i] = s0_ref[0 if s0_ref.shape[0] != nseq else i]

    rows_in = min(q_ref.shape[1], CHUNK)
    for r0 in range(0, q_ref.shape[1], CHUNK):
        _gla_chunk(q_ref, k_ref, v_ref, lg_ref, tri_ref, o_ref, s_sc, r0, rows_in, valid)

    @pl.when(c == pl.num_programs(1) - 1)
    def _():
        sout_ref[...] = s_sc[...]


def _gla_chunk(q_ref, k_ref, v_ref, lg_ref, tri_ref, o_ref, s_sc, r0, rows_in, valid):
    nseq = q_ref.shape[0]

    def padded(a):
        if rows_in == CHUNK:
            return a
        return jnp.concatenate([a, jnp.zeros((CHUNK - rows_in, a.shape[1]), a.dtype)], axis=0)

    wide = lambda ref: jnp.concatenate([padded(ref[i, r0:r0 + rows_in, :]) for i in range(nseq)], axis=1)
    q = wide(q_ref)
    k = wide(k_ref)
    lg = wide(lg_ref)
    vb = wide(v_ref)
    if valid < CHUNK:
        live = lax.broadcasted_iota(jnp.int32, (CHUNK, 1), 0) < valid
        k = jnp.where(live, k, 0.0)
        lg = jnp.where(live, lg, 0.0)
        vb = jnp.where(live, vb, jnp.zeros_like(vb))

    lg_hi = lg.astype(BF16)
    lg_lo = (lg - lg_hi.astype(F32)).astype(BF16)
    tri = tri_ref[...]
    b = _dot(tri, lg_hi) + _dot(tri, lg_lo)
    b_mid = b[CHUNK // 2 - 1:CHUNK // 2, :]
    qs = q * jnp.exp(b)
    qt = q * jnp.exp(b - b_mid)
    kt = (k * jnp.exp(b_mid - b)).astype(BF16)
    b_t = b.T
    b_last = b_t[:, CHUNK - 1:CHUNK]
    kl_t = (k.T * jnp.exp(b_last - b_t)).astype(BF16)
    decay = jnp.exp(b_last)

    row = lax.broadcasted_iota(jnp.int32, (CHUNK, CHUNK), 0)
    col = lax.broadcasted_iota(jnp.int32, (CHUNK, CHUNK), 1)
    causal = row >= col
    lane = lax.broadcasted_iota(jnp.int32, (1, LANES), 1)
    heads = [(i, h) for i in range(nseq) for h in range(HEADS)]
    cols = lambda i, h: slice(i * GW + (h // 2) * LANES, i * GW + (h // 2 + 1) * LANES)
    mine = lambda h: (lane // DK) == (h % 2)
    v_of = lambda i, h: vb[:, i * VW + h * DV:i * VW + (h + 1) * DV]
    s_old = [s_sc[i] for i in range(nseq)]
    s_bf = [s.astype(BF16) for s in s_old]
    a = [_dot_nt(jnp.where(mine(h), qt[:, cols(i, h)], 0.0).astype(BF16), kt[:, cols(i, h)]) for i, h in heads]
    inter = [_dot(jnp.where(mine(h), qs[:, cols(i, h)], 0.0).astype(BF16),
                  s_bf[i][(h // 2) * LANES:(h // 2 + 1) * LANES, :]) for i, h in heads]
    upd = [_dot(kl_t[i * GW + h * DK:i * GW + (h + 1) * DK, :], v_of(i, h)) for i, h in heads]
    for n, (i, h) in enumerate(heads):
        o_h = inter[n] + _dot(jnp.where(causal, a[n], 0.0).astype(BF16), v_of(i, h))
        o_ref[i, r0:r0 + rows_in, h * DV:(h + 1) * DV] = o_h[0:rows_in].astype(o_ref.dtype)
        rows = slice(i * GW + h * DK, i * GW + (h + 1) * DK)
        s_sc[i, h * DK:(h + 1) * DK, :] = decay[rows, :] * s_old[i][h * DK:(h + 1) * DK, :] + upd[n]


def _gla(s0, gq, gk, gv, lg, tri, nb, rows_in, valid):
    nc = gq.shape[0] // (nb * rows_in)
    per = min(nb, GLA_SEQS_PER_STEP)
    cps = GLA_CHUNKS_PER_STEP if nc % GLA_CHUNKS_PER_STEP == 0 else 1
    assert nb % per == 0 and (cps == 1 or rows_in == CHUNK)
    s0_blk = per if s0.shape[0] == nb else 1
    s0_map = (lambda b, c: (b, 0, 0)) if s0.shape[0] == nb else (lambda b, c: (0, 0, 0))
    seqs = lambda a: a.reshape(nb, nc * rows_in, a.shape[-1])
    rowblk = lambda w: pl.BlockSpec((per, rows_in * cps, w), lambda b, c: (b, c, 0))
    o, s_fin = pl.pallas_call(
        functools.partial(_gla_kernel, valid=valid),
        grid=(nb // per, nc // cps),
        in_specs=[pl.BlockSpec((s0_blk, GW, DV), s0_map), rowblk(GW), rowblk(GW), rowblk(VW), rowblk(GW),
                  pl.BlockSpec(tri.shape, lambda b, c: (0, 0))],
        out_specs=[rowblk(VW), pl.BlockSpec((per, GW, DV), lambda b, c: (b, 0, 0))],
        out_shape=[jax.ShapeDtypeStruct((nb, nc * rows_in, VW), BF16), jax.ShapeDtypeStruct((nb, GW, DV), F32)],
        scratch_shapes=[pltpu.VMEM((per, GW, DV), F32)],
        compiler_params=pltpu.CompilerParams(dimension_semantics=("arbitrary", "arbitrary"),
                                             vmem_limit_bytes=VMEM_LIMIT),
        name="gla",
    )(s0, seqs(gq), seqs(gk), seqs(gv), seqs(lg), tri)
    return o.reshape(nb * nc * rows_in, VW), s_fin


def _merge_kernel(x_ref, *rest, seq_tiles):
    if seq_tiles is None:
        od_ref, og_ref, gate_ref, dnw_ref, gnw_ref, wtop_ref, wbot_ref, y_ref = rest
        od = od_ref[...].astype(F32)
    else:
        oda_ref, odb_ref, og_ref, gate_ref, dnw_ref, gnw_ref, wtop_ref, wbot_ref, y_ref = rest
        od = jnp.where(pl.program_id(0) % seq_tiles < seq_tiles // 2, oda_ref[...], odb_ref[...]).astype(F32)

    def head_norm(o, w):
        parts = []
        for h in range(HEADS):
            sl = o[:, h * DV:(h + 1) * DV]
            parts.append(sl * lax.rsqrt(jnp.mean(sl * sl, axis=-1, keepdims=True) + RMS_EPS))
        return jnp.concatenate(parts, axis=-1) * w

    gate = gate_ref[...].astype(F32)
    mix_d = (head_norm(od, dnw_ref[...]) * gate[:, 0:VW]).astype(BF16)
    mix_g = (head_norm(og_ref[...].astype(F32), gnw_ref[...]) * gate[:, VW:]).astype(BF16)
    y_ref[...] = x_ref[...] + _dot(mix_d, wtop_ref[...]) + _dot(mix_g, wbot_ref[...])


def _merge(x, od_parts, og, gates, dnw, gnw, wtop, wbot, tr, seq_tiles=None):
    rows = x.shape[0]
    full = lambda a: pl.BlockSpec(a.shape, lambda i: (0,) * a.ndim)
    row = lambda w: pl.BlockSpec((tr, w), lambda i: (i, 0))
    if seq_tiles is None:
        od_specs = [row(VW)]
    else:
        assert seq_tiles % 2 == 0
        half = seq_tiles // 2
        od_specs = [pl.BlockSpec((tr, VW), lambda i: ((i // seq_tiles) * half + jnp.minimum(i % seq_tiles, half - 1), 0)),
                    pl.BlockSpec((tr, VW), lambda i: ((i // seq_tiles) * half + jnp.maximum(i % seq_tiles - half, 0), 0))]
    return pl.pallas_call(
        functools.partial(_merge_kernel, seq_tiles=seq_tiles),
        grid=(rows // tr,),
        in_specs=[row(D_MODEL)] + od_specs + [row(VW), row(2 * VW), full(dnw), full(gnw), full(wtop), full(wbot)],
        out_specs=row(D_MODEL),
        out_shape=jax.ShapeDtypeStruct((rows, D_MODEL), F32),
        compiler_params=pltpu.CompilerParams(dimension_semantics=("arbitrary",), vmem_limit_bytes=VMEM_LIMIT),
        name="merge",
    )(x, *od_parts, og, gates, dnw, gnw, wtop, wbot)


def kernel(x_prompt, x_sample, cache_k, cache_v, state_gla, page_table, meta_tokens, rel_bias, norm_w, w_in,
           q_norm_w, k_norm_w, lam_q, lam_k, diff_norm_w, gla_wa2, gla_ba, gla_norm_w, w_out):
    batch, seq, _ = x_prompt.shape
    nb, tq, _ = x_sample.shape
    tile = 512
    rpb = 16

    w = w_in[0]
    wm = w.astype(BF16)
    wa1 = jnp.pad(w[:, Z_MAIN:], ((0, 0), (0, LANES - GLA_GATE_RANK))).astype(BF16)
    wa2 = jnp.pad(gla_wa2[0], ((0, LANES - GLA_GATE_RANK), (0, 0))).astype(BF16)
    ba = gla_ba[0][None]
    nw = norm_w[0][None]
    qw = jnp.tile(q_norm_w[0].reshape(-1), HEADS)[None] * (DIFF_SCALE * LOG2E)
    kw = jnp.tile(k_norm_w[0].reshape(-1), HEADS)[None]
    grp = np.arange(256) // DK
    gmat = jnp.asarray(grp[:, None] == grp[None, :], BF16)
    consts = (nw, wm, wa1, wa2, ba, qw, kw, gmat)
    dnw = jnp.tile(diff_norm_w[0], HEADS)[None] * (1.0 - LAM_INIT)
    gnw = jnp.tile(gla_norm_w[0], HEADS)[None]
    wtop = w_out[0][:VW].astype(BF16)
    wbot = w_out[0][VW:].astype(BF16)
    t_idx = np.arange(CHUNK)
    tri = jnp.asarray(t_idx[:, None] >= t_idx[None, :], BF16)
    lq, lk = lam_q[0], lam_k[0]

    xs = jnp.pad(x_sample, ((0, 0), (0, rpb - tq), (0, 0))).reshape(nb * rpb, D_MODEL)
    xm = jnp.pad(meta_tokens, ((0, CHUNK - N_META), (0, 0)))
    x_small = jnp.concatenate([xs, xm], axis=0)
    ns = nb * rpb
    small = _inproj(x_small, x_small.shape[0], consts)
    s_qn, s_kn, s_knb, s_v, s_vb, s_gate, s_gq, s_gk, s_gv, s_lg = [a[:ns] for a in small]
    m_qn, m_kn, m_knb, m_v, m_vb, m_gate, m_gq, m_gk, m_gv, m_lg = [a[ns:] for a in small]

    xp = x_prompt.reshape(batch * seq, D_MODEL)
    meta_rows = (m_kn[:N_META].reshape(N_META * HEADS, 2 * DK), m_v[:N_META].reshape(N_META * HEADS, DV))
    p_qn, k_rows, p_knb, v_rows, p_vb, p_gate, p_gq, p_gk, p_gv, p_lg = _inproj(
        xp, tile, consts, seq_tiles=seq // tile, meta_rows=meta_rows)

    n_pool, page_size = cache_k.shape[1], cache_k.shape[2]
    ck = cache_k.reshape(n_pool, page_size * HEADS, 2 * DK)
    cv = cache_v.reshape(n_pool, page_size * HEADS, DV)
    o_first, o_second, os_d = _attn(
        page_table, rel_bias, lq, lk, p_qn.reshape(batch, seq, QKW), p_knb.reshape(batch, seq, QKW),
        p_vb.reshape(batch, seq, VW), m_knb, m_vb, s_qn, s_kn.reshape(ns * HEADS, 2 * DK),
        s_v.reshape(ns * HEADS, DV), ck, cv, tile, tq, ATTN_PAGES)
    o_d = (o_first, o_second)

    zero_state = jnp.zeros((1, GW, DV), F32)
    _, s_meta = _gla(zero_state, m_gq, m_gk, m_gv, m_lg, tri, 1, CHUNK, N_META)
    o_g, s_fin = _gla(s_meta, p_gq, p_gk, p_gv, p_lg, tri, batch, CHUNK, CHUNK)
    os_g, s_new = _gla(state_gla[0].reshape(nb, GW, DV), s_gq, s_gk, s_gv, s_lg, tri, nb, rpb, tq)

    half_rows = batch * seq // 2
    y_prompt = _merge(xp, [o.reshape(half_rows, VW) for o in o_d], o_g, p_gate, dnw, gnw, wtop, wbot, MERGE_ROWS,
                      seq_tiles=seq // MERGE_ROWS)
    y_small = _merge(xs, [os_d], os_g, s_gate, dnw, gnw, wtop, wbot, ns)

    y_prompt = y_prompt.reshape(batch, seq, D_MODEL)
    y_sample = y_small.reshape(nb, rpb, D_MODEL)[:, :tq]
    k_prompt = k_rows.reshape(1, batch, seq + N_META, HEADS, 2 * DK)
    v_prompt = v_rows.reshape(1, batch, seq + N_META, HEADS, DV)
    s_prompt = s_fin.reshape(1, batch, HEADS, DK, DV)
    k_sample = s_kn.reshape(nb, rpb, HEADS, 2 * DK)[None, :, :tq]
    v_sample = s_v.reshape(nb, rpb, HEADS, DV)[None, :, :tq]
    s_sample = s_new.reshape(1, nb, HEADS, DK, DV)
    return (y_prompt, y_sample, k_prompt, v_prompt, s_prompt, k_sample, v_sample, s_sample)
```

```python
import functools
import math

import numpy as np
import jax
import jax.numpy as jnp
from jax import lax
from jax.experimental import pallas as pl
from jax.experimental.pallas import tpu as pltpu

D_MODEL = 1024
N_META = 16
HEADS = 4
DK = 64
DV = 128
DIFF_SCALE = DK ** -0.5
GLA_GATE_RANK = 16
GLA_GATE_NORM = 16.0
NUM_BUCKETS = 32
MAX_DISTANCE = 128
RMS_EPS = 1e-6
LAM_INIT = 0.8 - 0.6 * math.exp(-0.3 * 0)
QKW = HEADS * 2 * DK
VW = HEADS * DV
GW = HEADS * DK
Z_MAIN = 3 * QKW + VW + 2 * GW + 2 * VW
LANES = 128
CHUNK = 128
NEG = -1e30
LOG2E = math.log2(math.e)
RING_SLOTS = 3
PAGE_DMA_PRIORITY = 1
ATTN_PAGES = 16
MERGE_ROWS = 1024
GLA_CHUNKS_PER_STEP = 4
GLA_SEQS_PER_STEP = 4
VMEM_LIMIT = 56 * 1024 * 1024

F32 = jnp.float32
BF16 = jnp.bfloat16


def _dot(a, b):
    return jnp.dot(a, b, preferred_element_type=F32)


def _dot_nt(a, b):
    return lax.dot_general(a, b, (((1,), (1,)), ((), ())), preferred_element_type=F32)


def _dot_tn(a, b):
    return lax.dot_general(a, b, (((0,), (0,)), ((), ())), preferred_element_type=F32)


def _lam_value(lamq_ref, lamk_ref):
    e = jnp.exp(jnp.sum(lamq_ref[...] * lamk_ref[...], axis=-1, keepdims=True))
    return e[0:1, :] - e[1:2, :] + LAM_INIT


def _inproj_kernel(x_ref, nw_ref, wm_ref, wa1_ref, wa2_ref, ba_ref, qw_ref, kw_ref, g_ref, *rest, seq_tiles):
    if seq_tiles is None:
        qn_ref, kn_ref, knb_ref, v_ref, vb_ref, gate_ref, gq_ref, gk_ref, gv_ref, lg_ref = rest
    else:
        (mk_ref, mv_ref, qn_ref, kn_hbm, knb_ref, v_hbm, vb_ref, gate_ref, gq_ref, gk_ref, gv_ref, lg_ref,
         stage, sem, meta_sem) = rest
    x = x_ref[...]
    ms = jnp.mean(x * x, axis=-1, keepdims=True)
    hn = (x * lax.rsqrt(ms + RMS_EPS) * nw_ref[...]).astype(BF16)

    def proj(lo, hi):
        return _dot(hn, wm_ref[:, lo:hi])

    def group_norm(z, w):
        sq = (z * z).astype(BF16)
        ss = jnp.concatenate([_dot(sq[:, c:c + 256], g_ref[...]) for c in range(0, QKW, 256)], axis=-1)
        return z * lax.rsqrt(ss * (1.0 / DK) + RMS_EPS) * w

    qn_ref[...] = group_norm(proj(0, QKW), qw_ref[...]).astype(BF16)
    kn = group_norm(proj(QKW, 2 * QKW), kw_ref[...])
    knb_ref[...] = kn.astype(BF16)
    v = proj(2 * QKW, 2 * QKW + VW)
    vb_ref[...] = v.astype(BF16)
    if seq_tiles is None:
        kn_ref[...] = kn
        v_ref[...] = v
    o = 2 * QKW + VW
    dg = proj(o, o + VW)
    gate_ref[:, 0:VW] = (dg * jax.nn.sigmoid(dg)).astype(BF16)
    o += VW
    gq_ref[...] = proj(o, o + GW) * (DK ** -0.5)
    gk_ref[...] = proj(o + GW, o + 2 * GW)
    o += 2 * GW
    gv_ref[...] = proj(o, o + VW).astype(BF16)
    o += VW
    gg = proj(o, o + VW)
    gate_ref[:, VW:2 * VW] = (gg * jax.nn.sigmoid(gg)).astype(BF16)
    ga = _dot(hn, wa1_ref[...])
    xg = _dot(ga.astype(BF16), wa2_ref[...]) + ba_ref[...]
    lg_ref[...] = (jnp.minimum(xg, 0.0) - jnp.log(1.0 + jnp.exp(-jnp.abs(xg)))) * (1.0 / GLA_GATE_NORM)
    if seq_tiles is not None:
        _write_cache_rows(kn, v, mk_ref, mv_ref, kn_hbm, v_hbm, stage, sem, meta_sem, seq_tiles)


def _write_cache_rows(kn, v, mk_ref, mv_ref, k_hbm, v_hbm, stage, sem, meta_sem, seq_tiles):
    i = pl.program_id(0)
    n = pl.num_programs(0)
    tr = kn.shape[0]
    seq_rows = (seq_tiles * tr + N_META) * HEADS

    def tile_copies(step):
        row0 = (step // seq_tiles) * seq_rows + (N_META + (step % seq_tiles) * tr) * HEADS
        dst = pl.ds(pl.multiple_of(row0, 8), tr * HEADS)
        return [pltpu.make_async_copy(stage.at[step % 2, 0], k_hbm.at[dst], sem.at[step % 2, 0]),
                pltpu.make_async_copy(stage.at[step % 2, 1], v_hbm.at[dst], sem.at[step % 2, 1])]

    @pl.when(i >= 2)
    def _():
        for c in tile_copies(i - 2):
            c.wait()

    slot = i % 2
    for h in range(HEADS):
        stage[slot, 0, pl.ds(h, tr, stride=HEADS), :] = kn[:, h * LANES:(h + 1) * LANES]
        stage[slot, 1, pl.ds(h, tr, stride=HEADS), :] = v[:, h * LANES:(h + 1) * LANES]
    for c in tile_copies(i):
        c.start()

    @pl.when(i % seq_tiles == 0)
    def _():
        dst = pl.ds(pl.multiple_of((i // seq_tiles) * seq_rows, 8), N_META * HEADS)
        meta = [pltpu.make_async_copy(mk_ref, k_hbm.at[dst], meta_sem.at[0]),
                pltpu.make_async_copy(mv_ref, v_hbm.at[dst], meta_sem.at[1])]
        for c in meta:
            c.start()
        for c in meta:
            c.wait()

    @pl.when(i == n - 1)
    def _():
        for c in tile_copies(i - 1) + tile_copies(i):
            c.wait()


def _inproj(x, tr, consts, seq_tiles=None, meta_rows=None):
    rows = x.shape[0]
    steps = rows // tr
    full = lambda a: pl.BlockSpec(a.shape, lambda i: (0,) * a.ndim)
    row = lambda w: pl.BlockSpec((tr, w), lambda i: (i, 0))
    outs = [(QKW, BF16), (QKW, F32), (QKW, BF16), (VW, F32), (VW, BF16), (2 * VW, BF16),
            (GW, F32), (GW, F32), (VW, BF16), (GW, F32)]
    out_specs = [row(w) for w, _ in outs]
    out_shape = [jax.ShapeDtypeStruct((rows, w), dt) for w, dt in outs]
    extra_in, extra_specs, scratch = [], [], []
    if seq_tiles is not None:
        assert steps % seq_tiles == 0 and steps >= 2
        cache_rows = (steps // seq_tiles) * (seq_tiles * tr + N_META) * HEADS
        for o in (1, 3):
            out_specs[o] = pl.BlockSpec(memory_space=pl.ANY)
            out_shape[o] = jax.ShapeDtypeStruct((cache_rows, LANES), F32)
        extra_in = list(meta_rows)
        extra_specs = [full(a) for a in meta_rows]
        scratch = [pltpu.VMEM((2, 2, tr * HEADS, LANES), F32), pltpu.SemaphoreType.DMA((2, 2)),
                   pltpu.SemaphoreType.DMA((2,))]
    return pl.pallas_call(
        functools.partial(_inproj_kernel, seq_tiles=seq_tiles),
        grid=(steps,),
        in_specs=[row(D_MODEL)] + [full(a) for a in consts] + extra_specs,
        out_specs=out_specs,
        out_shape=out_shape,
        scratch_shapes=scratch,
        compiler_params=pltpu.CompilerParams(dimension_semantics=("arbitrary",), vmem_limit_bytes=VMEM_LIMIT),
        name="inproj",
    )(x, *consts, *extra_in)


def _bucket_ranges():
    n = np.arange(MAX_DISTANCE)
    max_exact = NUM_BUCKETS // 2
    nf = np.maximum(n, 1).astype(np.float32)
    large = max_exact + (np.log(nf / np.float32(max_exact)) / np.float32(math.log(MAX_DISTANCE / max_exact))
                         * np.float32(NUM_BUCKETS - max_exact)).astype(np.int32)
    bucket = np.where(n < max_exact, n, np.minimum(large, NUM_BUCKETS - 1))
    return [(int(n[bucket == b].min()), int(n[bucket == b].max())) for b in range(NUM_BUCKETS)]


_BUCKET_RANGES = _bucket_ranges()


def _bias_table(n, rb_ref, h):
    far = rb_ref[NUM_BUCKETS - 1, h]
    t = jnp.zeros(n.shape, F32)
    for b, (lo, hi) in enumerate(_BUCKET_RANGES[:-1]):
        val = (rb_ref[b, h] - far) * LOG2E
        cond = (n == lo) if lo == hi else ((n >= lo) & (n <= hi))
        t = jnp.where(cond, val, t)
    return jnp.where(n < 0, NEG, t)


def _attn_kernel(pt_ref, rb_ref, lamq_ref, lamk_ref, qa_ref, qb_ref, k_ref, v_ref, km_ref, vm_ref,
                 sq_ref, skn_ref, svn_ref, ck_hbm, cv_hbm, oa_ref, ob_ref, so_ref,
                 bias_sc, mt_sc, qs_sc, m_sc, l_sc, acc_sc, *sample_scratch, tile, nq, pages, tq, groups):
    h = pl.program_id(0)
    b = pl.program_id(1)
    ip = pl.program_id(2)
    half = nq // 2
    st = (h * pl.num_programs(1) + b) * half + ip
    total_steps = pl.num_programs(0) * pl.num_programs(1) * half
    sample_begin, sample_scores, sample_softmax, sample_values, sample_end = _sample_stream(
        st, total_steps, pt_ref, rb_ref, lamq_ref, lamk_ref, sq_ref, skn_ref, svn_ref, ck_hbm, cv_hbm, so_ref,
        *sample_scratch, pages=pages, tq=tq, groups=groups)
    sample_begin()

    @pl.when((b == 0) & (ip == 0))
    def _():
        kk = lax.broadcasted_iota(jnp.int32, (tile, tile), 0)
        qq = lax.broadcasted_iota(jnp.int32, (tile, tile), 1)
        bias_sc[0] = jnp.zeros((tile, tile), F32)
        bias_sc[1] = _bias_table(qq - kk + tile, rb_ref, h)
        bias_sc[2] = _bias_table(qq - kk, rb_ref, h)
        km = lax.broadcasted_iota(jnp.int32, (N_META, tile), 0)
        qm = lax.broadcasted_iota(jnp.int32, (N_META, tile), 1)
        mt_sc[...] = _bias_table(N_META + qm - km, rb_ref, h)

    lane = lax.broadcasted_iota(jnp.int32, (1, LANES), 1)
    for t, q_ref in enumerate((qa_ref, qb_ref)):
        q = q_ref[...]
        zero = jnp.zeros_like(q)
        qs_sc[t] = jnp.concatenate([jnp.where(lane < DK, q, zero), jnp.where(lane >= DK, q, zero)], axis=0)

    def plan(n):
        if n == 0:
            return 0, ip, 2
        if n == 1:
            return 1, nq - 2 - ip, 1
        if n == 2:
            return 1, nq - 1 - ip, 2
        has_sub = ip >= 1
        if n == 3:
            return jnp.where(has_sub, 0, 1), jnp.where(has_sub, ip - 1, 0), jnp.where(has_sub, 1, 0)
        j = n - 4
        short_far = jnp.maximum(ip - 1, 0)
        long_first = jnp.where(has_sub, 0, 1)
        return jnp.where(j < short_far, 0, 1), jnp.where(j < short_far, j, j - short_far + long_first), None

    def rows(key_tile):
        return pl.ds(pl.multiple_of(key_tile * tile, tile), tile)

    def scores(slot, key_tile, kind):
        s = _dot_nt(k_ref[rows(key_tile), :], qs_sc[slot])
        if kind is None:
            return s
        bias = bias_sc[kind]
        return s + jnp.concatenate([bias, bias], axis=1)

    meta_bias = jnp.where(ip == 0, mt_sc[...], 0.0)
    for t in range(2):
        s = _dot_nt(km_ref[...], qs_sc[t])
        if t == 0:
            s = s + jnp.concatenate([meta_bias, meta_bias], axis=1)
        m = jnp.max(s, axis=0, keepdims=True)
        p = jnp.exp2(s - m)
        m_sc[t] = m
        l_sc[t] = jnp.sum(p, axis=0, keepdims=True)
        acc_sc[t] = _dot_tn(vm_ref[...], p.astype(BF16))

    plans = [plan(n) for n in range(nq + 1)]
    spread = (nq + 1) // groups
    assert spread >= 2
    s_next = scores(*plans[0])
    for n in range(nq + 1):
        s_cur = s_next
        slot, key_tile, _ = plans[n]
        if n < nq:
            s_next = scores(*plans[n + 1])
        m_prev = m_sc[slot]
        m_new = jnp.maximum(m_prev, jnp.max(s_cur, axis=0, keepdims=True))
        alpha = jnp.exp2(m_prev - m_new)
        if n // spread < groups:
            if n % spread == 0:
                sample_s = sample_scores(n // spread)
            elif n % spread == 1:
                sample_values(n // spread, *sample_softmax(sample_s))
        p = jnp.exp2(s_cur - m_new)
        l_sc[slot] = alpha * l_sc[slot] + jnp.sum(p, axis=0, keepdims=True)
        acc_sc[slot] = alpha * acc_sc[slot] + _dot_tn(v_ref[rows(key_tile), :], p.astype(BF16))
        m_sc[slot] = m_new

    lam = _lam_value(lamq_ref, lamk_ref)
    for t, o_ref in enumerate((oa_ref, ob_ref)):
        acc = acc_sc[t]
        inv = 1.0 / l_sc[t]
        o_t = acc[:, 0:tile] * inv[:, 0:tile] - (lam * inv[:, tile:]) * acc[:, tile:]
        o_ref[...] = o_t.T.astype(o_ref.dtype)
    sample_end()


def _attn(page_table, rel_bias, lam_q, lam_k, qn, knb, vb, km, vm, s_qn, s_kn, s_v, cache_k, cache_v, tile, tq, pages):
    batch, seq, _ = qn.shape
    nq = seq // tile
    half = nq // 2
    assert nq % 2 == 0
    steps = HEADS * batch * half
    nb, n_pages = page_table.shape
    rpb = s_qn.shape[0] // nb
    pcols = cache_k.shape[1]
    nrow = 2 * HEADS * tq
    total_groups = nb * (n_pages // pages)
    groups = total_groups // steps
    assert groups * steps == total_groups and (n_pages // pages) % groups == 0
    assert groups <= nq + 1 and total_groups >= RING_SLOTS - 1
    steps_per_elem = (n_pages // pages) // groups
    small = lambda a: pl.BlockSpec(a.shape, lambda h, b, ip, pt: (0,) * a.ndim)
    kv = pl.BlockSpec((None, seq, LANES), lambda h, b, ip, pt: (b, 0, h))
    meta = pl.BlockSpec((N_META, LANES), lambda h, b, ip, pt: (0, h))
    short = lambda h, b, ip, pt: (b, ip, h)
    elem = lambda h, b, ip, pt: (((h * batch + b) * half + ip) // steps_per_elem, 0)
    grid_spec = pltpu.PrefetchScalarGridSpec(
        num_scalar_prefetch=1,
        grid=(HEADS, batch, half),
        in_specs=[pl.BlockSpec(memory_space=pltpu.SMEM), small(lam_q), small(lam_k),
                  pl.BlockSpec((None, tile, LANES), short),
                  pl.BlockSpec((None, tile, LANES), lambda h, b, ip, pt: (b, nq - 1 - ip, h)),
                  kv, kv, meta, meta,
                  pl.BlockSpec((rpb, QKW), elem), pl.BlockSpec((rpb * HEADS, LANES), elem),
                  pl.BlockSpec((rpb * HEADS, LANES), elem),
                  pl.BlockSpec(memory_space=pl.ANY), pl.BlockSpec(memory_space=pl.ANY)],
        out_specs=[pl.BlockSpec((None, tile, LANES), short),
                   pl.BlockSpec((None, tile, LANES), lambda h, b, ip, pt: (b, half - 1 - ip, h)),
                   pl.BlockSpec((rpb, VW), elem)],
        scratch_shapes=[pltpu.VMEM((3, tile, tile), F32), pltpu.VMEM((N_META, tile), F32),
                        pltpu.VMEM((2, 2 * tile, LANES), BF16),
                        pltpu.VMEM((2, 1, 2 * tile), F32), pltpu.VMEM((2, 1, 2 * tile), F32),
                        pltpu.VMEM((2, LANES, 2 * tile), F32),
                        pltpu.VMEM((RING_SLOTS, pages, pcols, LANES), F32),
                        pltpu.VMEM((RING_SLOTS, pages, pcols, LANES), F32),
                        pltpu.SemaphoreType.DMA((RING_SLOTS, 2, pages)),
                        pltpu.VMEM((nrow, LANES), BF16), pltpu.VMEM((nrow, pcols), F32), pltpu.VMEM((nrow, pcols), F32),
                        pltpu.VMEM((nrow, LANES), F32), pltpu.VMEM((nrow, 1), F32), pltpu.VMEM((nrow, 1), F32),
                        pltpu.VMEM((nrow, LANES), F32)],
    )
    return pl.pallas_call(
        functools.partial(_attn_kernel, tile=tile, nq=nq, pages=pages, tq=tq, groups=groups),
        grid_spec=grid_spec,
        out_shape=[jax.ShapeDtypeStruct((batch, seq // 2, VW), BF16)] * 2
                  + [jax.ShapeDtypeStruct((nb * rpb, VW), F32)],
        compiler_params=pltpu.CompilerParams(dimension_semantics=("arbitrary",) * 3, vmem_limit_bytes=VMEM_LIMIT),
        name="attn",
    )(page_table, rel_bias, lam_q, lam_k, qn, qn, knb, vb, km, vm, s_qn, s_kn, s_v, cache_k, cache_v)


def _sample_stream(st, total_steps, pt_ref, rb_ref, lamq_ref, lamk_ref, q_ref, kn_ref, vn_ref, ck_hbm, cv_hbm, o_ref,
                   kbuf, vbuf, sem, w_sc, mask_sc, ptab_sc, ntab_sc, m_sc, l_sc, acc_sc, *, pages, tq, groups):
    nrow = 2 * HEADS * tq
    pcols = kbuf.shape[2]
    page_size = pcols // HEADS
    ng = pt_ref.shape[1] // pages
    steps_per_elem = ng // groups
    first = st % steps_per_elem == 0
    final = st % steps_per_elem == steps_per_elem - 1
    total = total_steps * groups

    def page_copies(gi):
        slot = gi % RING_SLOTS
        src = jnp.minimum(gi, total - 1)
        out = []
        for u in range(pages):
            page = pt_ref[src // ng, (src % ng) * pages + u]
            out.append(pltpu.make_async_copy(ck_hbm.at[page], kbuf.at[slot, u], sem.at[slot, 0, u]))
            out.append(pltpu.make_async_copy(cv_hbm.at[page], vbuf.at[slot, u], sem.at[slot, 1, u]))
        return out

    def build_tables():
        def tables(cols, offset):
            r = lax.broadcasted_iota(jnp.int32, (nrow, cols), 0)
            c = lax.broadcasted_iota(jnp.int32, (nrow, cols), 1)
            tok = r // (2 * HEADS)
            head = (r % (2 * HEADS)) // 2
            n = offset + tok - c // HEADS
            t = jnp.zeros((nrow, cols), F32)
            for h in range(HEADS):
                t = jnp.where(head == h, _bias_table(n, rb_ref, h), t)
            return jnp.where(head == c % HEADS, t, NEG), c // HEADS

        mask_sc[...] = jnp.where(tables(pcols, MAX_DISTANCE)[0] > 0.5 * NEG, 0.0, NEG)
        ptab_sc[...] = tables(pcols, page_size)[0]
        nt, slot = tables(LANES, 0)
        ntab_sc[...] = jnp.where(slot < tq, nt, NEG)

    def update(s, pv):
        m_prev = m_sc[...]
        m_new = jnp.maximum(m_prev, jnp.max(s, axis=-1, keepdims=True))
        alpha = jnp.exp2(m_prev - m_new)
        p = jnp.exp2(s - m_new)
        l_sc[...] = alpha * l_sc[...] + jnp.sum(p, axis=-1, keepdims=True)
        acc_sc[...] = alpha * acc_sc[...] + pv(p.astype(BF16))
        m_sc[...] = m_new

    def start_element():
        q = q_ref[...].astype(F32)
        r8 = lax.broadcasted_iota(jnp.int32, (2 * HEADS, LANES), 0)
        lane = lax.broadcasted_iota(jnp.int32, (2 * HEADS, LANES), 1)
        blocks = []
        for t in range(tq):
            blk = jnp.zeros((2 * HEADS, LANES), F32)
            for h in range(HEADS):
                blk = jnp.where(r8 // 2 == h, jnp.broadcast_to(q[t:t + 1, h * LANES:(h + 1) * LANES], blk.shape), blk)
            blocks.append(jnp.where(lane // DK == r8 % 2, blk, 0.0))
        w_sc[...] = jnp.concatenate(blocks, axis=0).astype(BF16)
        m_sc[...] = jnp.full(m_sc.shape, NEG, F32)
        l_sc[...] = jnp.zeros(l_sc.shape, F32)
        acc_sc[...] = jnp.zeros(acc_sc.shape, F32)
        pad = jnp.zeros((LANES - kn_ref.shape[0], LANES), F32)
        kn = jnp.concatenate([kn_ref[...], pad], axis=0).astype(BF16)
        vn = jnp.concatenate([vn_ref[...], pad], axis=0).astype(BF16)
        s = _dot_nt(w_sc[...], kn) + ntab_sc[...]
        update(s, lambda p: _dot(p, vn))

    def begin():
        @pl.when(st == 0)
        def _():
            for ahead in range(RING_SLOTS - 1):
                for c in page_copies(ahead):
                    c.start(priority=PAGE_DMA_PRIORITY)
            build_tables()

        pl.when(first)(start_element)

    def scores(n):
        gi = st * groups + n
        for c in page_copies(gi + RING_SLOTS - 1):
            c.start(priority=PAGE_DMA_PRIORITY)
        for c in page_copies(gi):
            c.wait()
        slot = gi % RING_SLOTS
        w = w_sc[...]
        parts = []
        for u in range(pages):
            s = _dot_nt(w, kbuf[slot, u].astype(BF16))
            if u == pages - 1 and n == groups - 1:
                s = s + jnp.where(final, ptab_sc[...], mask_sc[...])
            else:
                s = s + mask_sc[...]
            parts.append(s)
        return jnp.concatenate(parts, axis=-1)

    def softmax(s):
        m_prev = m_sc[...]
        m_new = jnp.maximum(m_prev, jnp.max(s, axis=-1, keepdims=True))
        alpha = jnp.exp2(m_prev - m_new)
        p = jnp.exp2(s - m_new)
        l_sc[...] = alpha * l_sc[...] + jnp.sum(p, axis=-1, keepdims=True)
        m_sc[...] = m_new
        return alpha, p.astype(BF16)

    def values(n, alpha, p):
        slot = (st * groups + n) % RING_SLOTS
        acc = _dot(p[:, 0:pcols], vbuf[slot, 0].astype(BF16))
        for u in range(1, pages):
            acc += _dot(p[:, u * pcols:(u + 1) * pcols], vbuf[slot, u].astype(BF16))
        acc_sc[...] = alpha * acc_sc[...] + acc

    def end():
        pl.when(final)(finish_element)

        @pl.when(st == total_steps - 1)
        def _():
            for extra in range(RING_SLOTS - 1):
                for c in page_copies(total + extra):
                    c.wait()

    def finish_element():
        lam = _lam_value(lamq_ref, lamk_ref)
        accn = acc_sc[...] * (1.0 / l_sc[...])
        o_ref[...] = jnp.zeros(o_ref.shape, F32)
        for t in range(tq):
            for h in range(HEADS):
                r = t * 2 * HEADS + 2 * h
                o_ref[t:t + 1, h * DV:(h + 1) * DV] = accn[r:r + 1, :] - lam * accn[r + 1:r + 2, :]

    return begin, scores, softmax, values, end


def _gla_kernel(s0_ref, q_ref, k_ref, v_ref, lg_ref, tri_ref, o_ref, sout_ref, s_sc, *, valid):
    c = pl.program_id(1)
    nseq = q_ref.shape[0]

    @pl.when(c == 0)
    def _():
        for i in range(nseq):
            s_sc[i] = s0_ref[0 if s0_ref.shape[0] != nseq else i]

    rows_in = min(q_ref.shape[1], CHUNK)
    for r0 in range(0, q_ref.shape[1], CHUNK):
        _gla_chunk(q_ref, k_ref, v_ref, lg_ref, tri_ref, o_ref, s_sc, r0, rows_in, valid)

    @pl.when(c == pl.num_programs(1) - 1)
    def _():
        sout_ref[...] = s_sc[...]


def _gla_chunk(q_ref, k_ref, v_ref, lg_ref, tri_ref, o_ref, s_sc, r0, rows_in, valid):
    nseq = q_ref.shape[0]

    def padded(a):
        if rows_in == CHUNK:
            return a
        return jnp.concatenate([a, jnp.zeros((CHUNK - rows_in, a.shape[1]), a.dtype)], axis=0)

    wide = lambda ref: jnp.concatenate([padded(ref[i, r0:r0 + rows_in, :]) for i in range(nseq)], axis=1)
    q = wide(q_ref)
    k = wide(k_ref)
    lg = wide(lg_ref)
    vb = wide(v_ref)
    if valid < CHUNK:
        live = lax.broadcasted_iota(jnp.int32, (CHUNK, 1), 0) < valid
        k = jnp.where(live, k, 0.0)
        lg = jnp.where(live, lg, 0.0)
        vb = jnp.where(live, vb, jnp.zeros_like(vb))

    lg_hi = lg.astype(BF16)
    lg_lo = (lg - lg_hi.astype(F32)).astype(BF16)
    tri = tri_ref[...]
    b = _dot(tri, lg_hi) + _dot(tri, lg_lo)
    b_mid = b[CHUNK // 2 - 1:CHUNK // 2, :]
    qs = q * jnp.exp(b)
    qt = q * jnp.exp(b - b_mid)
    kt = (k * jnp.exp(b_mid - b)).astype(BF16)
    b_t = b.T
    b_last = b_t[:, CHUNK - 1:CHUNK]
    kl_t = (k.T * jnp.exp(b_last - b_t)).astype(BF16)
    decay = jnp.exp(b_last)

    row = lax.broadcasted_iota(jnp.int32, (CHUNK, CHUNK), 0)
    col = lax.broadcasted_iota(jnp.int32, (CHUNK, CHUNK), 1)
    causal = row >= col
    lane = lax.broadcasted_iota(jnp.int32, (1, LANES), 1)
    heads = [(i, h) for i in range(nseq) for h in range(HEADS)]
    cols = lambda i, h: slice(i * GW + (h // 2) * LANES, i * GW + (h // 2 + 1) * LANES)
    mine = lambda h: (lane // DK) == (h % 2)
    v_of = lambda i, h: vb[:, i * VW + h * DV:i * VW + (h + 1) * DV]
    s_old = [s_sc[i] for i in range(nseq)]
    s_bf = [s.astype(BF16) for s in s_old]
    a = [_dot_nt(jnp.where(mine(h), qt[:, cols(i, h)], 0.0).astype(BF16), kt[:, cols(i, h)]) for i, h in heads]
    inter = [_dot(jnp.where(mine(h), qs[:, cols(i, h)], 0.0).astype(BF16),
                  s_bf[i][(h // 2) * LANES:(h // 2 + 1) * LANES, :]) for i, h in heads]
    upd = [_dot(kl_t[i * GW + h * DK:i * GW + (h + 1) * DK, :], v_of(i, h)) for i, h in heads]
    for n, (i, h) in enumerate(heads):
        o_h = inter[n] + _dot(jnp.where(causal, a[n], 0.0).astype(BF16), v_of(i, h))
        o_ref[i, r0:r0 + rows_in, h * DV:(h + 1) * DV] = o_h[0:rows_in].astype(o_ref.dtype)
        rows = slice(i * GW + h * DK, i * GW + (h + 1) * DK)
        s_sc[i, h * DK:(h + 1) * DK, :] = decay[rows, :] * s_old[i][h * DK:(h + 1) * DK, :] + upd[n]


def _gla(s0, gq, gk, gv, lg, tri, nb, rows_in, valid):
    nc = gq.shape[0] // (nb * rows_in)
    per = min(nb, GLA_SEQS_PER_STEP)
    cps = GLA_CHUNKS_PER_STEP if nc % GLA_CHUNKS_PER_STEP == 0 else 1
    assert nb % per == 0 and (cps == 1 or rows_in == CHUNK)
    s0_blk = per if s0.shape[0] == nb else 1
    s0_map = (lambda b, c: (b, 0, 0)) if s0.shape[0] == nb else (lambda b, c: (0, 0, 0))
    seqs = lambda a: a.reshape(nb, nc * rows_in, a.shape[-1])
    rowblk = lambda w: pl.BlockSpec((per, rows_in * cps, w), lambda b, c: (b, c, 0))
    o, s_fin = pl.pallas_call(
        functools.partial(_gla_kernel, valid=valid),
        grid=(nb // per, nc // cps),
        in_specs=[pl.BlockSpec((s0_blk, GW, DV), s0_map), rowblk(GW), rowblk(GW), rowblk(VW), rowblk(GW),
                  pl.BlockSpec(tri.shape, lambda b, c: (0, 0))],
        out_specs=[rowblk(VW), pl.BlockSpec((per, GW, DV), lambda b, c: (b, 0, 0))],
        out_shape=[jax.ShapeDtypeStruct((nb, nc * rows_in, VW), BF16), jax.ShapeDtypeStruct((nb, GW, DV), F32)],
        scratch_shapes=[pltpu.VMEM((per, GW, DV), F32)],
        compiler_params=pltpu.CompilerParams(dimension_semantics=("arbitrary", "arbitrary"),
                                             vmem_limit_bytes=VMEM_LIMIT),
        name="gla",
    )(s0, seqs(gq), seqs(gk), seqs(gv), seqs(lg), tri)
    return o.reshape(nb * nc * rows_in, VW), s_fin


def _merge_kernel(x_ref, *rest, seq_tiles):
    if seq_tiles is None:
        od_ref, og_ref, gate_ref, dnw_ref, gnw_ref, wtop_ref, wbot_ref, y_ref = rest
        od = od_ref[...].astype(F32)
    else:
        oda_ref, odb_ref, og_ref, gate_ref, dnw_ref, gnw_ref, wtop_ref, wbot_ref, y_ref = rest
        od = jnp.where(pl.program_id(0) % seq_tiles < seq_tiles // 2, oda_ref[...], odb_ref[...]).astype(F32)

    def head_norm(o, w):
        parts = []
        for h in range(HEADS):
            sl = o[:, h * DV:(h + 1) * DV]
            parts.append(sl * lax.rsqrt(jnp.mean(sl * sl, axis=-1, keepdims=True) + RMS_EPS))
        return jnp.concatenate(parts, axis=-1) * w

    gate = gate_ref[...].astype(F32)
    mix_d = (head_norm(od, dnw_ref[...]) * gate[:, 0:VW]).astype(BF16)
    mix_g = (head_norm(og_ref[...].astype(F32), gnw_ref[...]) * gate[:, VW:]).astype(BF16)
    y_ref[...] = x_ref[...] + _dot(mix_d, wtop_ref[...]) + _dot(mix_g, wbot_ref[...])


def _merge(x, od_parts, og, gates, dnw, gnw, wtop, wbot, tr, seq_tiles=None):
    rows = x.shape[0]
    full = lambda a: pl.BlockSpec(a.shape, lambda i: (0,) * a.ndim)
    row = lambda w: pl.BlockSpec((tr, w), lambda i: (i, 0))
    if seq_tiles is None:
        od_specs = [row(VW)]
    else:
        assert seq_tiles % 2 == 0
        half = seq_tiles // 2
        od_specs = [pl.BlockSpec((tr, VW), lambda i: ((i // seq_tiles) * half + jnp.minimum(i % seq_tiles, half - 1), 0)),
                    pl.BlockSpec((tr, VW), lambda i: ((i // seq_tiles) * half + jnp.maximum(i % seq_tiles - half, 0), 0))]
    return pl.pallas_call(
        functools.partial(_merge_kernel, seq_tiles=seq_tiles),
        grid=(rows // tr,),
        in_specs=[row(D_MODEL)] + od_specs + [row(VW), row(2 * VW), full(dnw), full(gnw), full(wtop), full(wbot)],
        out_specs=row(D_MODEL),
        out_shape=jax.ShapeDtypeStruct((rows, D_MODEL), F32),
        compiler_params=pltpu.CompilerParams(dimension_semantics=("arbitrary",), vmem_limit_bytes=VMEM_LIMIT),
        name="merge",
    )(x, *od_parts, og, gates, dnw, gnw, wtop, wbot)


def kernel(x_prompt, x_sample, cache_k, cache_v, state_gla, page_table, meta_tokens, rel_bias, norm_w, w_in,
           q_norm_w, k_norm_w, lam_q, lam_k, diff_norm_w, gla_wa2, gla_ba, gla_norm_w, w_out):
    batch, seq, _ = x_prompt.shape
    nb, tq, _ = x_sample.shape
    tile = 512
    rpb = 16

    w = w_in[0]
    wm = w.astype(BF16)
    wa1 = jnp.pad(w[:, Z_MAIN:], ((0, 0), (0, LANES - GLA_GATE_RANK))).astype(BF16)
    wa2 = jnp.pad(gla_wa2[0], ((0, LANES - GLA_GATE_RANK), (0, 0))).astype(BF16)
    ba = gla_ba[0][None]
    nw = norm_w[0][None]
    qw = jnp.tile(q_norm_w[0].reshape(-1), HEADS)[None] * (DIFF_SCALE * LOG2E)
    kw = jnp.tile(k_norm_w[0].reshape(-1), HEADS)[None]
    grp = np.arange(256) // DK
    gmat = jnp.asarray(grp[:, None] == grp[None, :], BF16)
    consts = (nw, wm, wa1, wa2, ba, qw, kw, gmat)
    dnw = jnp.tile(diff_norm_w[0], HEADS)[None] * (1.0 - LAM_INIT)
    gnw = jnp.tile(gla_norm_w[0], HEADS)[None]
    wtop = w_out[0][:VW].astype(BF16)
    wbot = w_out[0][VW:].astype(BF16)
    t_idx = np.arange(CHUNK)
    tri = jnp.asarray(t_idx[:, None] >= t_idx[None, :], BF16)
    lq, lk = lam_q[0], lam_k[0]

    xs = jnp.pad(x_sample, ((0, 0), (0, rpb - tq), (0, 0))).reshape(nb * rpb, D_MODEL)
    xm = jnp.pad(meta_tokens, ((0, CHUNK - N_META), (0, 0)))
    x_small = jnp.concatenate([xs, xm], axis=0)
    ns = nb * rpb
    small = _inproj(x_small, x_small.shape[0], consts)
    s_qn, s_kn, s_knb, s_v, s_vb, s_gate, s_gq, s_gk, s_gv, s_lg = [a[:ns] for a in small]
    m_qn, m_kn, m_knb, m_v, m_vb, m_gate, m_gq, m_gk, m_gv, m_lg = [a[ns:] for a in small]

    xp = x_prompt.reshape(batch * seq, D_MODEL)
    meta_rows = (m_kn[:N_META].reshape(N_META * HEADS, 2 * DK), m_v[:N_META].reshape(N_META * HEADS, DV))
    p_qn, k_rows, p_knb, v_rows, p_vb, p_gate, p_gq, p_gk, p_gv, p_lg = _inproj(
        xp, tile, consts, seq_tiles=seq // tile, meta_rows=meta_rows)

    n_pool, page_size = cache_k.shape[1], cache_k.shape[2]
    ck = cache_k.reshape(n_pool, page_size * HEADS, 2 * DK)
    cv = cache_v.reshape(n_pool, page_size * HEADS, DV)
    o_first, o_second, os_d = _attn(
        page_table, rel_bias, lq, lk, p_qn.reshape(batch, seq, QKW), p_knb.reshape(batch, seq, QKW),
        p_vb.reshape(batch, seq, VW), m_knb, m_vb, s_qn, s_kn.reshape(ns * HEADS, 2 * DK),
        s_v.reshape(ns * HEADS, DV), ck, cv, tile, tq, ATTN_PAGES)
    o_d = (o_first, o_second)

    zero_state = jnp.zeros((1, GW, DV), F32)
    _, s_meta = _gla(zero_state, m_gq, m_gk, m_gv, m_lg, tri, 1, CHUNK, N_META)
    o_g, s_fin = _gla(s_meta, p_gq, p_gk, p_gv, p_lg, tri, batch, CHUNK, CHUNK)
    os_g, s_new = _gla(state_gla[0].reshape(nb, GW, DV), s_gq, s_gk, s_gv, s_lg, tri, nb, rpb, tq)

    half_rows = batch * seq // 2
    y_prompt = _merge(xp, [o.reshape(half_rows, VW) for o in o_d], o_g, p_gate, dnw, gnw, wtop, wbot, MERGE_ROWS,
                      seq_tiles=seq // MERGE_ROWS)
    y_small = _merge(xs, [os_d], os_g, s_gate, dnw, gnw, wtop, wbot, ns)

    y_prompt = y_prompt.reshape(batch, seq, D_MODEL)
    y_sample = y_small.reshape(nb, rpb, D_MODEL)[:, :tq]
    k_prompt = k_rows.reshape(1, batch, seq + N_META, HEADS, 2 * DK)
    v_prompt = v_rows.reshape(1, batch, seq + N_META, HEADS, DV)
    s_prompt = s_fin.reshape(1, batch, HEADS, DK, DV)
    k_sample = s_kn.reshape(nb, rpb, HEADS, 2 * DK)[None, :, :tq]
    v_sample = s_v.reshape(nb, rpb, HEADS, DV)[None, :, :tq]
    s_sample = s_new.reshape(1, nb, HEADS, DK, DV)
    return (y_prompt, y_sample, k_prompt, v_prompt, s_prompt, k_sample, v_sample, s_sample)
```
